```python
import math
import jax
import jax.numpy as jnp
from jax import lax
import numpy as np

D_MODEL = 4096
BATCH = 1
SEQ = 8192
DEPTH = 1

HEAD_DIM = 128
CHUNK = 128
Q_BLOCK = 128
MIX_WIDTH = D_MODEL
GMLP_WIDTH = MIX_WIDTH // 4
GMLP_HEADS = GMLP_WIDTH // HEAD_DIM
DIFF_WIDTH = MIX_WIDTH // 2
DIFF_V_DIM = 2 * HEAD_DIM
DIFF_HEADS = DIFF_WIDTH // DIFF_V_DIM
DIFF_QK_WIDTH = DIFF_HEADS * 2 * HEAD_DIM
MEM_WIDTH = MIX_WIDTH - GMLP_WIDTH - DIFF_WIDTH
MEM_HEADS = 4
MEM_HEAD_DIM = MEM_WIDTH // MEM_HEADS
MEM_LEN = 256
IN_WIDTH = 2 * GMLP_WIDTH + 2 * DIFF_QK_WIDTH + DIFF_WIDTH + MEM_WIDTH
N_EXPERTS = 32
TOP_K = 4
D_FF_EXPERT = (3 * D_MODEL) // 8
SWIGLU_LIMIT = 7.0
SWIGLU_ALPHA = 1.702
MOE_BLOCK = 256
ROPE_THETA = 10000.0
EPS = 1e-6

kernel_name = 'hybrid_gmlp_diffattn_memxattn_moe'


def rms_norm(x, g):
    xf = x.astype(jnp.float32)
    y = xf * lax.rsqrt(jnp.mean(xf * xf, axis=-1, keepdims=True) + EPS)
    return (y * g.astype(jnp.float32)).astype(x.dtype)


def layer_norm(x, g, b):
    xf = x.astype(jnp.float32)
    mu = jnp.mean(xf, axis=-1, keepdims=True)
    xc = xf - mu
    y = xc * lax.rsqrt(jnp.mean(xc * xc, axis=-1, keepdims=True) + EPS)
    return (y * g.astype(jnp.float32) + b.astype(jnp.float32)).astype(x.dtype)


def rope_tables(positions):
    inv_freq = 1.0 / (ROPE_THETA ** (jnp.arange(0, HEAD_DIM, 2, dtype=jnp.float32) / HEAD_DIM))
    ang = positions.astype(jnp.float32)[..., None] * inv_freq
    return jnp.cos(ang), jnp.sin(ang)


def apply_rope(x, cos, sin):
    shape = cos.shape[:2] + (1,) * (x.ndim - 3) + cos.shape[-1:]
    c = cos.reshape(shape).astype(x.dtype)
    s = sin.reshape(shape).astype(x.dtype)
    x1, x2 = jnp.split(x, 2, axis=-1)
    return jnp.concatenate([x1 * c - x2 * s, x2 * c + x1 * s], axis=-1)


def chunked_spatial_gating(z, ln_g, ln_b, w_s, b_s):
    B, S, _ = z.shape
    z = jax.nn.gelu(z, approximate=False)
    u, v = jnp.split(z, 2, axis=-1)
    v = layer_norm(v, ln_g, ln_b).reshape(B, S // CHUNK, CHUNK, GMLP_HEADS, HEAD_DIM)
    causal = jnp.tril(jnp.ones((CHUNK, CHUNK), dtype=bool))
    w = jnp.where(causal[None], w_s, jnp.zeros_like(w_s)).astype(v.dtype)
    mixed = jnp.einsum('hts,bcshd->bcthd', w, v) + b_s.T.astype(v.dtype)[None, None, :, :, None]
    return u * mixed.reshape(B, S, GMLP_WIDTH)


def differential_attention(q1, q2, k1, k2, v, lam):
    B, S, H, _ = q1.shape
    nb = S // Q_BLOCK
    scale = HEAD_DIM ** -0.5

    def to_blocks(q):
        return q.reshape(B, nb, Q_BLOCK, H, HEAD_DIM).transpose(1, 0, 3, 2, 4)

    k1t = k1.transpose(0, 2, 1, 3)
    k2t = k2.transpose(0, 2, 1, 3)
    vt = v.transpose(0, 2, 1, 3)
    k_pos = jnp.arange(S)

    def one_block(args):
        qb1, qb2, blk = args
        q_pos = blk * Q_BLOCK + jnp.arange(Q_BLOCK)
        mask = k_pos[None, :] <= q_pos[:, None]

        def probs(qb, kt):
            s = jnp.einsum('bhqd,bhkd->bhqk', qb, kt).astype(jnp.float32) * scale
            return jax.nn.softmax(jnp.where(mask, s, -jnp.inf), axis=-1)

        p = probs(qb1, k1t) - lam * probs(qb2, k2t)
        return jnp.einsum('bhqk,bhkd->bhqd', p.astype(vt.dtype), vt)

    out = lax.map(one_block, (to_blocks(q1), to_blocks(q2), jnp.arange(nb)))
    return out.transpose(1, 0, 3, 2, 4).reshape(B, S, H, DIFF_V_DIM)


def memory_attention(qm, km, vm):
    s = jnp.einsum('bshd,bmhd->bhsm', qm, km).astype(jnp.float32) * (MEM_HEAD_DIM ** -0.5)
    p = jax.nn.softmax(s, axis=-1)
    return jnp.einsum('bhsm,bmhd->bshd', p.astype(vm.dtype), vm)


def moe_ffn(h, w_router, b_router, w_gate_up, b_gate_up, w_down, b_down):
    B, S, D = h.shape
    N = B * S
    t = h.reshape(N, D)
    logits = (t @ w_router).astype(jnp.float32) + b_router.astype(jnp.float32)
    top_v, top_i = lax.top_k(logits, TOP_K)
    gates = jax.nn.softmax(top_v, axis=-1)
    A = N * TOP_K
    flat_e = top_i.reshape(A).astype(jnp.int32)
    flat_g = gates.reshape(A)
    flat_t = jnp.arange(A, dtype=jnp.int32) // TOP_K
    order = jnp.argsort(flat_e)
    se = flat_e[order]
    st = flat_t[order]
    sg = flat_g[order]
    counts = jnp.bincount(flat_e, length=N_EXPERTS).astype(jnp.int32)
    padded = (counts + MOE_BLOCK - 1) // MOE_BLOCK * MOE_BLOCK
    start = jnp.cumsum(counts) - counts
    pend = jnp.cumsum(padded)
    pstart = pend - padded
    dest = pstart[se] + jnp.arange(A, dtype=jnp.int32) - start[se]
    n_blocks = (A + N_EXPERTS * (MOE_BLOCK - 1) + MOE_BLOCK - 1) // MOE_BLOCK
    P = n_blocks * MOE_BLOCK
    buf_t = jnp.zeros((P,), jnp.int32).at[dest].set(st)
    buf_g = jnp.zeros((P,), jnp.float32).at[dest].set(sg)
    blk_start = jnp.arange(n_blocks, dtype=jnp.int32) * MOE_BLOCK
    blk_e = jnp.minimum(jnp.searchsorted(pend, blk_start, side='right'), N_EXPERTS - 1)

    def expert_block(args):
        tok, g, e = args
        xb = t[tok]
        gu = xb @ w_gate_up[e] + b_gate_up[e]
        gate, up = jnp.split(gu, 2, axis=-1)
        gate = jnp.minimum(gate, SWIGLU_LIMIT)
        up = jnp.clip(up, -SWIGLU_LIMIT, SWIGLU_LIMIT)
        act = (up + 1) * (gate * jax.nn.sigmoid(SWIGLU_ALPHA * gate))
        y = act @ w_down[e] + b_down[e]
        return y * g[:, None].astype(y.dtype)

    ys = lax.map(expert_block, (buf_t.reshape(n_blocks, MOE_BLOCK), buf_g.reshape(n_blocks, MOE_BLOCK), blk_e))
    out = jax.ops.segment_sum(ys.reshape(P, D), buf_t, num_segments=N)
    return out.reshape(B, S, D).astype(h.dtype)


def hybrid_layer(x, mem, cos, sin, lam_init,
                 attn_norm_g, w_in, gmlp_ln_g, gmlp_ln_b, gmlp_w_s, gmlp_b_s,
                 diff_q_norm_g, diff_k_norm_g, lq1, lk1, lq2, lk2, diff_subln_g,
                 mem_norm_g, w_mem_kv, mem_q_norm_g, mem_k_norm_g, w_out,
                 ffn_norm_g, w_router, b_router, w_gate_up, b_gate_up, w_down, b_down):
    B, S, _ = x.shape
    h = rms_norm(x, attn_norm_g)
    z = h @ w_in
    c0 = 2 * GMLP_WIDTH
    c1 = c0 + DIFF_QK_WIDTH
    c2 = c1 + DIFF_QK_WIDTH
    c3 = c2 + DIFF_WIDTH
    z_g, z_q, z_k, z_v, z_m = jnp.split(z, [c0, c1, c2, c3], axis=-1)

    y_g = chunked_spatial_gating(z_g, gmlp_ln_g, gmlp_ln_b, gmlp_w_s, gmlp_b_s)

    q = apply_rope(rms_norm(z_q.reshape(B, S, DIFF_HEADS, 2, HEAD_DIM), diff_q_norm_g), cos, sin)
    k = apply_rope(rms_norm(z_k.reshape(B, S, DIFF_HEADS, 2, HEAD_DIM), diff_k_norm_g), cos, sin)
    v = z_v.reshape(B, S, DIFF_HEADS, DIFF_V_DIM)
    f32 = jnp.float32
    lam = (jnp.exp(jnp.sum(lq1.astype(f32) * lk1.astype(f32)))
           - jnp.exp(jnp.sum(lq2.astype(f32) * lk2.astype(f32))) + lam_init)
    o = differential_attention(q[..., 0, :], q[..., 1, :], k[..., 0, :], k[..., 1, :], v, lam)
    o = rms_norm(o, diff_subln_g) * (1.0 - lam_init)
    y_d = o.reshape(B, S, DIFF_WIDTH)

    qm = rms_norm(z_m.reshape(B, S, MEM_HEADS, MEM_HEAD_DIM), mem_q_norm_g)
    kv = rms_norm(mem, mem_norm_g) @ w_mem_kv
    km, vm = jnp.split(kv, 2, axis=-1)
    M = mem.shape[1]
    km = rms_norm(km.reshape(B, M, MEM_HEADS, MEM_HEAD_DIM), mem_k_norm_g)
    vm = vm.reshape(B, M, MEM_HEADS, MEM_HEAD_DIM)
    y_m = memory_attention(qm, km, vm).reshape(B, S, MEM_WIDTH)

    x = x + jnp.concatenate([y_g, y_d, y_m], axis=-1) @ w_out
    x = x + moe_ffn(rms_norm(x, ffn_norm_g), w_router, b_router, w_gate_up, b_gate_up, w_down, b_down)
    return x


def setup_inputs(seed: int = 0) -> dict:
    key = jax.random.key(seed)
    ks = jax.random.split(key, 32)
    f32 = jnp.float32

    def nrm(k, shape, scale):
        return jax.random.normal(k, shape, f32) * scale

    def gain(k, shape):
        return 1.0 + 0.01 * jax.random.normal(k, shape, f32)

    L = DEPTH
    return {
        'x': jax.random.normal(ks[0], (BATCH, SEQ, D_MODEL), f32),
        'mem': jax.random.normal(ks[1], (BATCH, MEM_LEN, D_MODEL), f32),
        'positions': jnp.broadcast_to(jnp.arange(SEQ, dtype=jnp.int32), (BATCH, SEQ)),
        'attn_norm_g': gain(ks[2], (L, D_MODEL)),
        'w_in': nrm(ks[3], (L, D_MODEL, IN_WIDTH), D_MODEL ** -0.5),
        'gmlp_ln_g': gain(ks[4], (L, GMLP_WIDTH)),
        'gmlp_ln_b': nrm(ks[5], (L, GMLP_WIDTH), 0.02),
        'gmlp_w_s': nrm(ks[6], (L, GMLP_HEADS, CHUNK, CHUNK), CHUNK ** -0.5),
        'gmlp_b_s': gain(ks[7], (L, GMLP_HEADS, CHUNK)),
        'diff_q_norm_g': gain(ks[8], (L, HEAD_DIM)),
        'diff_k_norm_g': gain(ks[9], (L, HEAD_DIM)),
        'diff_lambda_q1': nrm(ks[10], (L, HEAD_DIM), 0.1),
        'diff_lambda_k1': nrm(ks[11], (L, HEAD_DIM), 0.1),
        'diff_lambda_q2': nrm(ks[12], (L, HEAD_DIM), 0.1),
        'diff_lambda_k2': nrm(ks[13], (L, HEAD_DIM), 0.1),
        'diff_subln_g': gain(ks[14], (L, DIFF_V_DIM)),
        'mem_norm_g': gain(ks[15], (L, D_MODEL)),
        'w_mem_kv': nrm(ks[16], (L, D_MODEL, 2 * MEM_WIDTH), D_MODEL ** -0.5),
        'mem_q_norm_g': gain(ks[17], (L, MEM_HEAD_DIM)),
        'mem_k_norm_g': gain(ks[18], (L, MEM_HEAD_DIM)),
        'w_out': nrm(ks[19], (L, MIX_WIDTH, D_MODEL), MIX_WIDTH ** -0.5),
        'ffn_norm_g': gain(ks[20], (L, D_MODEL)),
        'w_router': nrm(ks[21], (L, D_MODEL, N_EXPERTS), D_MODEL ** -0.5),
        'b_router': nrm(ks[22], (L, N_EXPERTS), 0.01),
        'w_gate_up': nrm(ks[23], (L, N_EXPERTS, D_MODEL, 2 * D_FF_EXPERT), D_MODEL ** -0.5),
        'b_gate_up': nrm(ks[24], (L, N_EXPERTS, 2 * D_FF_EXPERT), 0.02),
        'w_down': nrm(ks[25], (L, N_EXPERTS, D_FF_EXPERT, D_MODEL), D_FF_EXPERT ** -0.5),
        'b_down': nrm(ks[26], (L, N_EXPERTS, D_MODEL), 0.02),
    }


def reference(x, mem, positions, attn_norm_g, w_in, gmlp_ln_g, gmlp_ln_b, gmlp_w_s, gmlp_b_s,
              diff_q_norm_g, diff_k_norm_g, diff_lambda_q1, diff_lambda_k1, diff_lambda_q2,
              diff_lambda_k2, diff_subln_g, mem_norm_g, w_mem_kv, mem_q_norm_g, mem_k_norm_g,
              w_out, ffn_norm_g, w_router, b_router, w_gate_up, b_gate_up, w_down, b_down):
    cos, sin = rope_tables(positions)
    for i in range(DEPTH):
        lam_init = 0.8 - 0.6 * math.exp(-0.3 * i)
        x = hybrid_layer(x, mem, cos, sin, lam_init,
                         attn_norm_g[i], w_in[i], gmlp_ln_g[i], gmlp_ln_b[i], gmlp_w_s[i], gmlp_b_s[i],
                         diff_q_norm_g[i], diff_k_norm_g[i], diff_lambda_q1[i], diff_lambda_k1[i],
                         diff_lambda_q2[i], diff_lambda_k2[i], diff_subln_g[i],
                         mem_norm_g[i], w_mem_kv[i], mem_q_norm_g[i], mem_k_norm_g[i], w_out[i],
                         ffn_norm_g[i], w_router[i], b_router[i], w_gate_up[i], b_gate_up[i],
                         w_down[i], b_down[i])
    return x
```

```python
import functools
import math

import jax
import jax.numpy as jnp
from jax import lax
from jax.experimental import pallas as pl
from jax.experimental.pallas import tpu as pltpu

F32 = jnp.float32
BF16 = jnp.bfloat16

D_MODEL = 4096
SEQ = 8192
HEAD_DIM = 128
CHUNK = 128
GMLP_WIDTH = 1024
GMLP_HEADS = 8
DIFF_HEADS = 8
DIFF_V_DIM = 256
DIFF_QK_WIDTH = 2048
DIFF_WIDTH = 2048
MEM_WIDTH = 1024
MEM_HEADS = 4
MEM_HEAD_DIM = 256
MEM_LEN = 256
IN_WIDTH = 9216
N_EXPERTS = 32
TOP_K = 4
D_FF = 1536
SWIGLU_LIMIT = 7.0
SWIGLU_ALPHA = 1.702
ROPE_THETA = 10000.0
EPS = 1e-6

COL_Q = 2 * GMLP_WIDTH
COL_K = COL_Q + DIFF_QK_WIDTH
COL_V = COL_K + DIFF_QK_WIDTH
COL_M = COL_V + DIFF_WIDTH

MOE_TILE = 256
MOE_ROWS = ((SEQ * TOP_K + N_EXPERTS * (MOE_TILE - 1) + MOE_TILE - 1) // MOE_TILE) * MOE_TILE
MOE_NT = MOE_ROWS // MOE_TILE

MIB = 1 << 20


def _params(sem, vmem_mib):
    return pltpu.CompilerParams(dimension_semantics=sem, vmem_limit_bytes=vmem_mib * MIB)


def _rms(x, g):
    ms = jnp.mean(x * x, axis=-1, keepdims=True)
    return x * lax.rsqrt(ms + EPS) * g


def _rmsnorm_kernel(x_ref, g_ref, o_ref):
    o_ref[...] = _rms(x_ref[...], g_ref[...]).astype(o_ref.dtype)


def _rmsnorm_rows(x, g, tm):
    m, d = x.shape
    return pl.pallas_call(
        _rmsnorm_kernel,
        grid=(m // tm,),
        in_specs=[pl.BlockSpec((tm, d), lambda i: (i, 0)), pl.BlockSpec((1, d), lambda i: (0, 0))],
        out_specs=pl.BlockSpec((tm, d), lambda i: (i, 0)),
        out_shape=jax.ShapeDtypeStruct((m, d), BF16),
        compiler_params=_params(("arbitrary",), 40),
        name="rmsnorm_rows",
    )(x, g.reshape(1, d))


def _matmul_kernel(*refs, widths, has_res):
    n = len(widths)
    lhs = refs[:n]
    w_ref = refs[n]
    res_ref = refs[n + 1] if has_res else None
    o_ref = refs[n + 1 + has_res]
    wb_ref = refs[n + 2 + has_res]

    @pl.when(pl.program_id(1) == 0)
    def _():
        wb_ref[...] = w_ref[...].astype(BF16)

    acc = None
    off = 0
    for r, kw in zip(lhs, widths):
        part = jnp.dot(r[...], wb_ref[off:off + kw, :], preferred_element_type=F32)
        acc = part if acc is None else acc + part
        off += kw
    if has_res:
        acc = acc + res_ref[...]
    o_ref[...] = acc.astype(o_ref.dtype)


def _matmul(lhs_list, w, res, out_dtype, tm, tn, name):
    m = lhs_list[0].shape[0]
    k, n = w.shape
    widths = tuple(a.shape[1] for a in lhs_list)
    assert sum(widths) == k
    in_specs = [pl.BlockSpec((tm, kw), lambda j, i: (i, 0)) for kw in widths]
    in_specs.append(pl.BlockSpec((k, tn), lambda j, i: (0, j)))
    args = list(lhs_list) + [w]
    if res is not None:
        in_specs.append(pl.BlockSpec((tm, tn), lambda j, i: (i, j)))
        args.append(res)
    return pl.pallas_call(
        functools.partial(_matmul_kernel, widths=widths, has_res=res is not None),
        grid=(n // tn, m // tm),
        in_specs=in_specs,
        out_specs=pl.BlockSpec((tm, tn), lambda j, i: (i, j)),
        out_shape=jax.ShapeDtypeStruct((m, n), out_dtype),
        scratch_shapes=[pltpu.VMEM((k, tn), BF16)],
        compiler_params=_params(("arbitrary", "arbitrary"), 56),
        name=name,
    )(*args)


def _gmlp_kernel(zu_ref, zv_ref, lng_ref, lnb_ref, ws_ref, bst_ref, o_ref, *, tm):
    inv_sqrt2 = 1.0 / math.sqrt(2.0)

    def gelu(x):
        return 0.5 * x * (1.0 + lax.erf(x * inv_sqrt2))

    u = gelu(zu_ref[...])
    v = gelu(zv_ref[...])
    mu = jnp.mean(v, axis=-1, keepdims=True)
    vc = v - mu
    var = jnp.mean(vc * vc, axis=-1, keepdims=True)
    vn = (vc * lax.rsqrt(var + EPS) * lng_ref[...] + lnb_ref[...]).astype(BF16)
    row = lax.broadcasted_iota(jnp.int32, (CHUNK, CHUNK), 0)
    col = lax.broadcasted_iota(jnp.int32, (CHUNK, CHUNK), 1)
    causal = col <= row
    bst = bst_ref[...]
    for h in range(GMLP_HEADS):
        w = jnp.where(causal, ws_ref[h], 0.0).astype(BF16)
        b = bst[:, h:h + 1]
        for c in range(tm // CHUNK):
            vs = vn[c * CHUNK:(c + 1) * CHUNK, h * HEAD_DIM:(h + 1) * HEAD_DIM]
            mixed = jnp.dot(w, vs, preferred_element_type=F32) + b
            o_ref[c * CHUNK:(c + 1) * CHUNK, h * HEAD_DIM:(h + 1) * HEAD_DIM] = (
                u[c * CHUNK:(c + 1) * CHUNK, h * HEAD_DIM:(h + 1) * HEAD_DIM] * mixed).astype(o_ref.dtype)


def _gmlp(z, ln_g, ln_b, w_s, b_s, tm=256):
    return pl.pallas_call(
        functools.partial(_gmlp_kernel, tm=tm),
        grid=(SEQ // tm,),
        in_specs=[
            pl.BlockSpec((tm, GMLP_WIDTH), lambda i: (i, 0)),
            pl.BlockSpec((tm, GMLP_WIDTH), lambda i: (i, 1)),
            pl.BlockSpec((1, GMLP_WIDTH), lambda i: (0, 0)),
            pl.BlockSpec((1, GMLP_WIDTH), lambda i: (0, 0)),
            pl.BlockSpec((GMLP_HEADS, CHUNK, CHUNK), lambda i: (0, 0, 0)),
            pl.BlockSpec((CHUNK, GMLP_HEADS), lambda i: (0, 0)),
        ],
        out_specs=pl.BlockSpec((tm, GMLP_WIDTH), lambda i: (i, 0)),
        out_shape=jax.ShapeDtypeStruct((SEQ, GMLP_WIDTH), BF16),
        compiler_params=_params(("arbitrary",), 40),
        name="gmlp",
    )(z, z, ln_g.reshape(1, -1), ln_b.reshape(1, -1), w_s, b_s.T)


def _qkv_prep_kernel(pos_ref, freq_ref, sign_ref, gq_ref, gk_ref, zq_ref, zk_ref, zv_ref, q_ref, k_ref, v_ref):
    ang = pos_ref[...].astype(F32) * freq_ref[...]
    cos = jnp.cos(ang)
    sin = jnp.sin(ang) * sign_ref[...]
    scale = HEAD_DIM ** -0.5

    def prep(z_ref, g_ref, o_ref, mult):
        for s in range(DIFF_QK_WIDTH // HEAD_DIM):
            seg = _rms(z_ref[:, s * HEAD_DIM:(s + 1) * HEAD_DIM], g_ref[...])
            rot = pltpu.roll(seg, HEAD_DIM // 2, 1)
            out = seg * cos + rot * sin
            if mult != 1.0:
                out = out * mult
            o_ref[:, s * HEAD_DIM:(s + 1) * HEAD_DIM] = out.astype(o_ref.dtype)

    prep(zq_ref, gq_ref, q_ref, scale)
    prep(zk_ref, gk_ref, k_ref, 1.0)
    v_ref[...] = zv_ref[...].astype(v_ref.dtype)


def _qkv_prep(z, positions, gq, gk, tm=256):
    half = jnp.arange(0, HEAD_DIM, 2, dtype=F32) / HEAD_DIM
    inv_freq = 1.0 / (ROPE_THETA ** half)
    freq = jnp.concatenate([inv_freq, inv_freq]).reshape(1, HEAD_DIM)
    sign = jnp.concatenate([-jnp.ones((HEAD_DIM // 2,), F32), jnp.ones((HEAD_DIM // 2,), F32)]).reshape(1, HEAD_DIM)
    wide = DIFF_QK_WIDTH
    row_spec = lambda cb: pl.BlockSpec((tm, wide), lambda i: (i, cb))
    vec = pl.BlockSpec((1, HEAD_DIM), lambda i: (0, 0))
    out_spec = pl.BlockSpec((tm, wide), lambda i: (i, 0))
    shp = jax.ShapeDtypeStruct((SEQ, wide), BF16)
    return pl.pallas_call(
        _qkv_prep_kernel,
        grid=(SEQ // tm,),
        in_specs=[pl.BlockSpec((tm, 1), lambda i: (i, 0)), vec, vec, vec, vec,
                  row_spec(COL_Q // wide), row_spec(COL_K // wide), row_spec(COL_V // wide)],
        out_specs=[out_spec, out_spec, out_spec],
        out_shape=[shp, shp, shp],
        compiler_params=_params(("arbitrary",), 40),
        name="qkv_prep",
    )(positions.reshape(SEQ, 1), freq, sign, gq.reshape(1, -1), gk.reshape(1, -1), z, z, z)


def _diff_attn_kernel(lq1_ref, lk1_ref, lq2_ref, lk2_ref, sg_ref, q_ref, k_ref, v_ref, o_ref,
                      m_ref, l_ref, acc_ref, *, tq, lam_init):
    i = pl.program_id(1)
    m_ref[...] = jnp.full(m_ref.shape, -jnp.inf, F32)
    l_ref[...] = jnp.zeros(l_ref.shape, F32)
    acc_ref[...] = jnp.zeros(acc_ref.shape, F32)
    row = lax.broadcasted_iota(jnp.int32, (tq, tq), 0)
    col = lax.broadcasted_iota(jnp.int32, (tq, tq), 1)
    causal = col <= row

    def block(j, masked):
        start = pl.multiple_of(j * tq, tq)
        kb = k_ref[pl.ds(start, tq), :]
        vb = v_ref[pl.ds(start, tq), :]
        for a in range(2):
            q = q_ref[:, a * HEAD_DIM:(a + 1) * HEAD_DIM]
            k = kb[:, a * HEAD_DIM:(a + 1) * HEAD_DIM]
            s = lax.dot_general(q, k, (((1,), (1,)), ((), ())), preferred_element_type=F32)
            if masked:
                s = jnp.where(causal, s, -jnp.inf)
            m_prev = m_ref[a]
            m_new = jnp.maximum(m_prev, jnp.max(s, axis=-1, keepdims=True))
            alpha = jnp.exp(m_prev - m_new)
            p = jnp.exp(s - m_new)
            l_ref[a] = alpha * l_ref[a] + jnp.sum(p, axis=-1, keepdims=True)
            acc_ref[a] = alpha * acc_ref[a] + jnp.dot(p.astype(BF16), vb, preferred_element_type=F32)
            m_ref[a] = m_new

    def body(j, carry):
        block(j, False)
        return carry

    lax.fori_loop(0, i, body, 0)
    block(i, True)

    lam = (jnp.exp(jnp.sum(lq1_ref[...] * lk1_ref[...], axis=-1, keepdims=True))
           - jnp.exp(jnp.sum(lq2_ref[...] * lk2_ref[...], axis=-1, keepdims=True)) + lam_init)
    o = acc_ref[0] / l_ref[0] - lam * (acc_ref[1] / l_ref[1])
    o_ref[...] = (_rms(o, sg_ref[...]) * (1.0 - lam_init)).astype(o_ref.dtype)


def _diff_attn(q, k, v, lq1, lk1, lq2, lk2, subln_g, lam_init, tq=512):
    vec = pl.BlockSpec((1, HEAD_DIM), lambda h, i: (0, 0))
    kv_spec = pl.BlockSpec((SEQ, DIFF_V_DIM), lambda h, i: (0, h))
    return pl.pallas_call(
        functools.partial(_diff_attn_kernel, tq=tq, lam_init=lam_init),
        grid=(DIFF_HEADS, SEQ // tq),
        in_specs=[vec, vec, vec, vec, pl.BlockSpec((1, DIFF_V_DIM), lambda h, i: (0, 0)),
                  pl.BlockSpec((tq, DIFF_V_DIM), lambda h, i: (i, h)), kv_spec, kv_spec],
        out_specs=pl.BlockSpec((tq, DIFF_V_DIM), lambda h, i: (i, h)),
        out_shape=jax.ShapeDtypeStruct((SEQ, DIFF_WIDTH), BF16),
        scratch_shapes=[pltpu.VMEM((2, tq, 1), F32), pltpu.VMEM((2, tq, 1), F32),
                        pltpu.VMEM((2, tq, DIFF_V_DIM), F32)],
        compiler_params=_params(("arbitrary", "arbitrary"), 48),
        name="diff_attn",
    )(lq1.reshape(1, -1), lk1.reshape(1, -1), lq2.reshape(1, -1), lk2.reshape(1, -1),
      subln_g.reshape(1, -1), q, k, v)


def _mem_attn_kernel(zm_ref, kv_ref, gq_ref, gk_ref, o_ref):
    scale = MEM_HEAD_DIM ** -0.5
    for h in range(MEM_HEADS):
        lo, hi = h * MEM_HEAD_DIM, (h + 1) * MEM_HEAD_DIM
        q = (_rms(zm_ref[:, lo:hi], gq_ref[...]) * scale).astype(BF16)
        k = _rms(kv_ref[:, lo:hi], gk_ref[...]).astype(BF16)
        v = kv_ref[:, MEM_WIDTH + lo:MEM_WIDTH + hi].astype(BF16)
        s = lax.dot_general(q, k, (((1,), (1,)), ((), ())), preferred_element_type=F32)
        p = jnp.exp(s - jnp.max(s, axis=-1, keepdims=True))
        l = jnp.sum(p, axis=-1, keepdims=True)
        o = jnp.dot(p.astype(BF16), v, preferred_element_type=F32) / l
        o_ref[:, lo:hi] = o.astype(o_ref.dtype)


def _mem_attn(z, kv, gq, gk, tm=512):
    vec = pl.BlockSpec((1, MEM_HEAD_DIM), lambda i: (0, 0))
    return pl.pallas_call(
        _mem_attn_kernel,
        grid=(SEQ // tm,),
        in_specs=[pl.BlockSpec((tm, MEM_WIDTH), lambda i: (i, COL_M // MEM_WIDTH)),
                  pl.BlockSpec((MEM_LEN, 2 * MEM_WIDTH), lambda i: (0, 0)), vec, vec],
        out_specs=pl.BlockSpec((tm, MEM_WIDTH), lambda i: (i, 0)),
        out_shape=jax.ShapeDtypeStruct((SEQ, MEM_WIDTH), BF16),
        compiler_params=_params(("arbitrary",), 40),
        name="mem_attn",
    )(z, kv, gq.reshape(1, -1), gk.reshape(1, -1))


def _router_kernel(x_ref, g_ref, wr_ref, br_ref, h_ref, gate_ref, exp_ref, rank_ref, cnt_ref, carry_ref, *, tm):
    @pl.when(pl.program_id(0) == 0)
    def _():
        carry_ref[...] = jnp.zeros(carry_ref.shape, F32)

    h = _rms(x_ref[...], g_ref[...])
    half = D_MODEL // 2
    h_ref[...] = pltpu.pack_elementwise([h[:, :half], h[:, half:]], packed_dtype=BF16)
    logits = jnp.dot(h, wr_ref[...], precision=lax.Precision.HIGHEST, preferred_element_type=F32) + br_ref[...]
    lane = lax.broadcasted_iota(jnp.int32, logits.shape, 1)
    slot = lax.broadcasted_iota(jnp.int32, (tm, TOP_K), 1)
    work = logits
    sel = jnp.zeros(logits.shape, jnp.bool_)
    hits = []
    top = None
    nums = jnp.zeros((tm, TOP_K), F32)
    experts = jnp.zeros((tm, TOP_K), jnp.int32)
    denom = jnp.zeros((tm, 1), F32)
    for k in range(TOP_K):
        m = jnp.max(work, axis=-1, keepdims=True)
        idx = jnp.min(jnp.where(work == m, lane, N_EXPERTS), axis=-1, keepdims=True)
        hit = lane == idx
        if k == 0:
            top = m
        e = jnp.exp(m - top)
        nums = jnp.where(slot == k, e, nums)
        experts = jnp.where(slot == k, idx, experts)
        denom = denom + e
        work = jnp.where(hit, -jnp.inf, work)
        sel = jnp.logical_or(sel, hit)
        hits.append(hit)
    gate_ref[...] = nums / denom
    exp_ref[...] = experts
    r = lax.broadcasted_iota(jnp.int32, (tm, tm), 0)
    c = lax.broadcasted_iota(jnp.int32, (tm, tm), 1)
    strict = jnp.where(c < r, 1.0, 0.0).astype(BF16)
    chosen = jnp.where(sel, 1.0, 0.0)
    rank = jnp.dot(strict, chosen.astype(BF16), preferred_element_type=F32) + carry_ref[...]
    ranks = jnp.zeros((tm, TOP_K), F32)
    for k in range(TOP_K):
        ranks = jnp.where(slot == k, jnp.sum(jnp.where(hits[k], rank, 0.0), axis=-1, keepdims=True), ranks)
    rank_ref[...] = ranks.astype(jnp.int32)
    carry_ref[...] = carry_ref[...] + jnp.sum(chosen, axis=0, keepdims=True)
    cnt_ref[...] = carry_ref[...].astype(jnp.int32)


def _router(x, g, w_router, b_router, tm=256):
    small = lambda: pl.BlockSpec((tm, TOP_K), lambda i: (i, 0))
    return pl.pallas_call(
        functools.partial(_router_kernel, tm=tm),
        grid=(SEQ // tm,),
        in_specs=[pl.BlockSpec((tm, D_MODEL), lambda i: (i, 0)), pl.BlockSpec((1, D_MODEL), lambda i: (0, 0)),
                  pl.BlockSpec((D_MODEL, N_EXPERTS), lambda i: (0, 0)), pl.BlockSpec((1, N_EXPERTS), lambda i: (0, 0))],
        out_specs=[pl.BlockSpec((tm, D_MODEL // 2), lambda i: (i, 0)), small(), small(), small(),
                   pl.BlockSpec((1, N_EXPERTS), lambda i: (0, 0))],
        out_shape=[jax.ShapeDtypeStruct((SEQ, D_MODEL // 2), jnp.uint32),
                   jax.ShapeDtypeStruct((SEQ, TOP_K), F32),
                   jax.ShapeDtypeStruct((SEQ, TOP_K), jnp.int32),
                   jax.ShapeDtypeStruct((SEQ, TOP_K), jnp.int32),
                   jax.ShapeDtypeStruct((1, N_EXPERTS), jnp.int32)],
        scratch_shapes=[pltpu.VMEM((1, N_EXPERTS), F32)],
        compiler_params=_params(("arbitrary",), 40),
        name="router",
    )(x, g.reshape(1, -1), w_router, b_router.reshape(1, -1))


def _gather_kernel(nused_ref, idx_ref, src_ref, o_ref, sem, *, rows):
    @pl.when(pl.program_id(0) < nused_ref[0])
    def _():
        def issue(r, carry):
            pltpu.make_async_copy(src_ref.at[pl.ds(idx_ref[0, 0, r], 1)], o_ref.at[pl.ds(r, 1)], sem).start()
            return carry

        lax.fori_loop(0, rows, issue, 0)

        def drain(r, carry):
            pltpu.make_async_copy(src_ref.at[pl.ds(0, 1)], o_ref.at[pl.ds(r, 1)], sem).wait()
            return carry

        lax.fori_loop(0, rows, drain, 0)

    @pl.when(pl.program_id(0) >= nused_ref[0])
    def _():
        o_ref[...] = jnp.zeros(o_ref.shape, o_ref.dtype)


def _gather_rows(src, idx, n_used):
    width = src.shape[1]
    last = lambda i, nu: jnp.minimum(i, nu[0] - 1)
    return pl.pallas_call(
        functools.partial(_gather_kernel, rows=MOE_TILE),
        grid_spec=pltpu.PrefetchScalarGridSpec(
            num_scalar_prefetch=1,
            grid=(MOE_NT,),
            in_specs=[pl.BlockSpec((1, 1, MOE_TILE), lambda i, nu: (last(i, nu), 0, 0), memory_space=pltpu.SMEM),
                      pl.BlockSpec(memory_space=pl.ANY)],
            out_specs=pl.BlockSpec((MOE_TILE, width), lambda i, nu: (i, 0)),
            scratch_shapes=[pltpu.SemaphoreType.DMA(())],
        ),
        out_shape=jax.ShapeDtypeStruct((MOE_ROWS, width), src.dtype),
        compiler_params=_params(("arbitrary",), 40),
        name="moe_gather",
    )(n_used, idx.reshape(MOE_NT, 1, MOE_TILE), src)


def _unpack_bf16(words):
    lo = pltpu.unpack_elementwise(words, index=0, packed_dtype=BF16, unpacked_dtype=F32)
    hi = pltpu.unpack_elementwise(words, index=1, packed_dtype=BF16, unpacked_dtype=F32)
    return lo, hi


def _tile_is_first(blk_ref, t):
    return jnp.logical_or(t == 0, blk_ref[t] != blk_ref[jnp.maximum(t - 1, 0)])


def _moe_up_kernel(blk_ref, nused_ref, x_ref, wg_ref, wu_ref, bg_ref, bu_ref, o_ref, wgb_ref, wub_ref):
    t = pl.program_id(1)

    @pl.when(_tile_is_first(blk_ref, t))
    def _():
        wgb_ref[...] = wg_ref[0].astype(BF16)
        wub_ref[...] = wu_ref[0].astype(BF16)

    @pl.when(t < nused_ref[0])
    def _():
        lo, hi = _unpack_bf16(x_ref[...])
        lo = lo.astype(BF16)
        hi = hi.astype(BF16)
        half = D_MODEL // 2

        def proj(wb_ref, b_ref):
            return (jnp.dot(lo, wb_ref[:half, :], preferred_element_type=F32)
                    + jnp.dot(hi, wb_ref[half:, :], preferred_element_type=F32) + b_ref[0])

        gate = jnp.minimum(proj(wgb_ref, bg_ref), SWIGLU_LIMIT)
        up = jnp.clip(proj(wub_ref, bu_ref), -SWIGLU_LIMIT, SWIGLU_LIMIT)
        act = (up + 1.0) * (gate * jax.nn.sigmoid(SWIGLU_ALPHA * gate))
        o_ref[...] = act.astype(o_ref.dtype)

    @pl.when(t >= nused_ref[0])
    def _():
        o_ref[...] = jnp.zeros(o_ref.shape, o_ref.dtype)


def _moe_up(xs, blk_e, n_used, w_gate_up, b_gate_up, tn=512):
    nj = D_FF // tn
    last = lambda t, nu: jnp.minimum(t, nu[0] - 1)
    b3 = b_gate_up.reshape(N_EXPERTS, 1, 2 * D_FF)
    return pl.pallas_call(
        _moe_up_kernel,
        grid_spec=pltpu.PrefetchScalarGridSpec(
            num_scalar_prefetch=2,
            grid=(nj, MOE_NT),
            in_specs=[
                pl.BlockSpec((MOE_TILE, D_MODEL // 2), lambda j, t, be, nu: (last(t, nu), 0)),
                pl.BlockSpec((1, D_MODEL, tn), lambda j, t, be, nu: (be[t], 0, j)),
                pl.BlockSpec((1, D_MODEL, tn), lambda j, t, be, nu: (be[t], 0, nj + j)),
                pl.BlockSpec((1, 1, tn), lambda j, t, be, nu: (be[t], 0, j)),
                pl.BlockSpec((1, 1, tn), lambda j, t, be, nu: (be[t], 0, nj + j)),
            ],
            out_specs=pl.BlockSpec((MOE_TILE, tn), lambda j, t, be, nu: (t, j)),
            scratch_shapes=[pltpu.VMEM((D_MODEL, tn), BF16), pltpu.VMEM((D_MODEL, tn), BF16)],
        ),
        out_shape=jax.ShapeDtypeStruct((MOE_ROWS, D_FF), BF16),
        compiler_params=_params(("arbitrary", "arbitrary"), 56),
        name="moe_up",
    )(blk_e, n_used, xs, w_gate_up, w_gate_up, b3, b3)


def _moe_down_kernel(blk_ref, nused_ref, a_ref, w_ref, b_ref, o_ref, wb_ref, *, tn):
    t = pl.program_id(1)

    @pl.when(_tile_is_first(blk_ref, t))
    def _():
        wb_ref[...] = w_ref[0].astype(BF16)

    @pl.when(t < nused_ref[0])
    def _():
        y = jnp.dot(a_ref[...], wb_ref[...], preferred_element_type=F32) + b_ref[0]
        o_ref[...] = pltpu.pack_elementwise([y[:, :tn // 2], y[:, tn // 2:]], packed_dtype=BF16)

    @pl.when(t >= nused_ref[0])
    def _():
        o_ref[...] = jnp.zeros(o_ref.shape, o_ref.dtype)


def _moe_down(act, blk_e, n_used, w_down, b_down, tn=1024):
    nj = D_MODEL // tn
    last = lambda t, nu: jnp.minimum(t, nu[0] - 1)
    return pl.pallas_call(
        functools.partial(_moe_down_kernel, tn=tn),
        grid_spec=pltpu.PrefetchScalarGridSpec(
            num_scalar_prefetch=2,
            grid=(nj, MOE_NT),
            in_specs=[
                pl.BlockSpec((MOE_TILE, D_FF), lambda j, t, be, nu: (last(t, nu), 0)),
                pl.BlockSpec((1, D_FF, tn), lambda j, t, be, nu: (be[t], 0, j)),
                pl.BlockSpec((1, 1, tn), lambda j, t, be, nu: (be[t], 0, j)),
            ],
            out_specs=pl.BlockSpec((MOE_TILE, tn // 2), lambda j, t, be, nu: (t, j)),
            scratch_shapes=[pltpu.VMEM((D_FF, tn), BF16)],
        ),
        out_shape=jax.ShapeDtypeStruct((MOE_ROWS, D_MODEL // 2), jnp.uint32),
        compiler_params=_params(("arbitrary", "arbitrary"), 56),
        name="moe_down",
    )(blk_e, n_used, act, w_down, b_down.reshape(N_EXPERTS, 1, D_MODEL))


def _combine_kernel(dest_ref, ys_ref, x_ref, g_ref, o_ref, buf_ref, sem, *, tc, tn):
    def issue(r, carry):
        for k in range(TOP_K):
            pltpu.make_async_copy(ys_ref.at[pl.ds(dest_ref[0, 0, r * TOP_K + k], 1)],
                                  buf_ref.at[k, pl.ds(r, 1)], sem).start()
        return carry

    lax.fori_loop(0, tc, issue, 0)

    def drain(r, carry):
        for k in range(TOP_K):
            pltpu.make_async_copy(ys_ref.at[pl.ds(0, 1)], buf_ref.at[k, pl.ds(r, 1)], sem).wait()
        return carry

    lax.fori_loop(0, tc, drain, 0)

    g = g_ref[...]
    half = tn // 2
    lo_sum = None
    hi_sum = None
    for k in range(TOP_K):
        lo, hi = _unpack_bf16(buf_ref[k])
        gk = g[:, k:k + 1]
        lo_sum = gk * lo if lo_sum is None else lo_sum + gk * lo
        hi_sum = gk * hi if hi_sum is None else hi_sum + gk * hi
    for j in range(D_MODEL // tn):
        o_ref[:, j * tn:j * tn + half] = x_ref[:, j * tn:j * tn + half] + lo_sum[:, j * half:(j + 1) * half]
        o_ref[:, j * tn + half:(j + 1) * tn] = x_ref[:, j * tn + half:(j + 1) * tn] + hi_sum[:, j * half:(j + 1) * half]


def _combine(ys, x2, dest4, gates4, tn, tc=128):
    nb = SEQ // tc
    return pl.pallas_call(
        functools.partial(_combine_kernel, tc=tc, tn=tn),
        grid=(nb,),
        in_specs=[pl.BlockSpec((1, 1, tc * TOP_K), lambda i: (i, 0, 0), memory_space=pltpu.SMEM),
                  pl.BlockSpec(memory_space=pl.ANY),
                  pl.BlockSpec((tc, D_MODEL), lambda i: (i, 0)),
                  pl.BlockSpec((tc, TOP_K), lambda i: (i, 0))],
        out_specs=pl.BlockSpec((tc, D_MODEL), lambda i: (i, 0)),
        out_shape=jax.ShapeDtypeStruct((SEQ, D_MODEL), F32),
        scratch_shapes=[pltpu.VMEM((TOP_K, tc, D_MODEL // 2), jnp.uint32), pltpu.SemaphoreType.DMA(())],
        compiler_params=_params(("arbitrary",), 40),
        name="moe_combine",
    )(dest4.reshape(nb, 1, tc * TOP_K), ys, x2, gates4)


def kernel(x, mem, positions, attn_norm_g, w_in, gmlp_ln_g, gmlp_ln_b, gmlp_w_s, gmlp_b_s, diff_q_norm_g, diff_k_norm_g, diff_lambda_q1, diff_lambda_k1, diff_lambda_q2, diff_lambda_k2, diff_subln_g, mem_norm_g, w_mem_kv, mem_q_norm_g, mem_k_norm_g, w_out, ffn_norm_g, w_router, b_router, w_gate_up, b_gate_up, w_down, b_down):
    depth = attn_norm_g.shape[0]
    xs = x.reshape(SEQ, D_MODEL)
    mem2 = mem.reshape(MEM_LEN, D_MODEL)
    for i in range(depth):
        lam_init = 0.8 - 0.6 * math.exp(-0.3 * i)
        h = _rmsnorm_rows(xs, attn_norm_g[i], tm=256)
        z = _matmul([h], w_in[i], None, F32, tm=1024, tn=512, name="in_proj")

        y_g = _gmlp(z, gmlp_ln_g[i], gmlp_ln_b[i], gmlp_w_s[i], gmlp_b_s[i])

        q, k, v = _qkv_prep(z, positions, diff_q_norm_g[i], diff_k_norm_g[i])
        y_d = _diff_attn(q, k, v, diff_lambda_q1[i], diff_lambda_k1[i], diff_lambda_q2[i], diff_lambda_k2[i],
                         diff_subln_g[i], lam_init)

        hm = _rmsnorm_rows(mem2, mem_norm_g[i], tm=256)
        kv = _matmul([hm], w_mem_kv[i], None, F32, tm=MEM_LEN, tn=512, name="mem_kv_proj")
        y_m = _mem_attn(z, kv, mem_q_norm_g[i], mem_k_norm_g[i])

        x2 = _matmul([y_g, y_d, y_m], w_out[i], xs, F32, tm=1024, tn=512, name="out_proj")

        hp, gates4, experts4, rank4, counts = _router(x2, ffn_norm_g[i], w_router[i], b_router[i])

        counts = counts[0]
        padded = (counts + MOE_TILE - 1) // MOE_TILE * MOE_TILE
        pend = jnp.cumsum(padded)
        pstart = pend - padded
        dest4 = pstart[experts4] + rank4
        tok4 = lax.broadcasted_iota(jnp.int32, (SEQ, TOP_K), 0)
        buf_t = jnp.zeros((MOE_ROWS,), jnp.int32).at[dest4.reshape(-1)].set(tok4.reshape(-1))
        n_used = (pend[-1] // MOE_TILE).astype(jnp.int32).reshape(1)
        blk_start = jnp.arange(MOE_NT, dtype=jnp.int32) * MOE_TILE
        blk_e = jnp.minimum(jnp.searchsorted(pend, blk_start, side='right'), N_EXPERTS - 1).astype(jnp.int32)

        xg = _gather_rows(hp, buf_t, n_used)
        act = _moe_up(xg, blk_e, n_used, w_gate_up[i], b_gate_up[i])
        ys = _moe_down(act, blk_e, n_used, w_down[i], b_down[i])
        xs = _combine(ys, x2, dest4, gates4, tn=1024)
    return xs.reshape(x.shape)
```

```python
import functools
import math

import jax
import jax.numpy as jnp
from jax import lax
from jax.experimental import pallas as pl
from jax.experimental.pallas import tpu as pltpu

F32 = jnp.float32
BF16 = jnp.bfloat16

D_MODEL = 4096
SEQ = 8192
HEAD_DIM = 128
CHUNK = 128
GMLP_WIDTH = 1024
GMLP_HEADS = 8
DIFF_HEADS = 8
DIFF_V_DIM = 256
DIFF_QK_WIDTH = 2048
DIFF_WIDTH = 2048
MEM_WIDTH = 1024
MEM_HEADS = 4
MEM_HEAD_DIM = 256
MEM_LEN = 256
IN_WIDTH = 9216
N_EXPERTS = 32
TOP_K = 4
D_FF = 1536
SWIGLU_LIMIT = 7.0
SWIGLU_ALPHA = 1.702
ROPE_THETA = 10000.0
EPS = 1e-6

COL_Q = 2 * GMLP_WIDTH
COL_K = COL_Q + DIFF_QK_WIDTH
COL_V = COL_K + DIFF_QK_WIDTH
COL_M = COL_V + DIFF_WIDTH

MOE_TILE = 256
MOE_ROWS = ((SEQ * TOP_K + N_EXPERTS * (MOE_TILE - 1) + MOE_TILE - 1) // MOE_TILE) * MOE_TILE
MOE_NT = MOE_ROWS // MOE_TILE

MIB = 1 << 20


def _params(sem, vmem_mib):
    return pltpu.CompilerParams(dimension_semantics=sem, vmem_limit_bytes=vmem_mib * MIB)


def _rms(x, g):
    ms = jnp.mean(x * x, axis=-1, keepdims=True)
    return x * lax.rsqrt(ms + EPS) * g


SLAB_LANES = 128
SLAB_ROWS = D_MODEL // 2 // SLAB_LANES


def _unpack_bf16(words):
    lo = pltpu.unpack_elementwise(words, index=0, packed_dtype=BF16, unpacked_dtype=F32)
    hi = pltpu.unpack_elementwise(words, index=1, packed_dtype=BF16, unpacked_dtype=F32)
    return lo, hi


def _store_slabs(o_ref, h, rows):
    half = D_MODEL // 2
    for s in range(SLAB_ROWS):
        lo = h[:, s * SLAB_LANES:(s + 1) * SLAB_LANES]
        hi = h[:, half + s * SLAB_LANES:half + (s + 1) * SLAB_LANES]
        o_ref[pl.ds(s, rows, stride=SLAB_ROWS), :] = pltpu.pack_elementwise([lo, hi], packed_dtype=BF16)


def _load_slabs(x_ref, rows):
    los, his = [], []
    for s in range(SLAB_ROWS):
        lo, hi = _unpack_bf16(x_ref[pl.ds(s, rows, stride=SLAB_ROWS), :])
        los.append(lo.astype(BF16))
        his.append(hi.astype(BF16))
    return jnp.concatenate(los + his, axis=1)


def _rmsnorm_kernel(x_ref, g_ref, o_ref):
    o_ref[...] = _rms(x_ref[...], g_ref[...]).astype(o_ref.dtype)


def _rmsnorm_rows(x, g, tm):
    m, d = x.shape
    return pl.pallas_call(
        _rmsnorm_kernel,
        grid=(m // tm,),
        in_specs=[pl.BlockSpec((tm, d), lambda i: (i, 0)), pl.BlockSpec((1, d), lambda i: (0, 0))],
        out_specs=pl.BlockSpec((tm, d), lambda i: (i, 0)),
        out_shape=jax.ShapeDtypeStruct((m, d), BF16),
        compiler_params=_params(("arbitrary",), 40),
        name="rmsnorm_rows",
    )(x, g.reshape(1, d))


def _matmul_kernel(*refs, widths, has_res):
    n = len(widths)
    lhs = refs[:n]
    w_ref = refs[n]
    res_ref = refs[n + 1] if has_res else None
    o_ref = refs[n + 1 + has_res]

    acc = None
    off = 0
    for r, kw in zip(lhs, widths):
        part = jnp.dot(r[...], w_ref[off:off + kw, :].astype(BF16), preferred_element_type=F32)
        acc = part if acc is None else acc + part
        off += kw
    if has_res:
        acc = acc + res_ref[...]
    o_ref[...] = acc.astype(o_ref.dtype)


def _matmul(lhs_list, w, res, out_dtype, tm, tn, name):
    m = lhs_list[0].shape[0]
    k, n = w.shape
    widths = tuple(a.shape[1] for a in lhs_list)
    assert sum(widths) == k
    in_specs = [pl.BlockSpec((tm, kw), lambda j, i: (i, 0)) for kw in widths]
    in_specs.append(pl.BlockSpec((k, tn), lambda j, i: (0, j)))
    args = list(lhs_list) + [w]
    if res is not None:
        in_specs.append(pl.BlockSpec((tm, tn), lambda j, i: (i, j)))
        args.append(res)
    return pl.pallas_call(
        functools.partial(_matmul_kernel, widths=widths, has_res=res is not None),
        grid=(n // tn, m // tm),
        in_specs=in_specs,
        out_specs=pl.BlockSpec((tm, tn), lambda j, i: (i, j)),
        out_shape=jax.ShapeDtypeStruct((m, n), out_dtype),
        compiler_params=_params(("arbitrary", "arbitrary"), 56),
        name=name,
    )(*args)


def _gmlp_kernel(zu_ref, zv_ref, lng_ref, lnb_ref, ws_ref, bst_ref, o_ref, *, tm):
    inv_sqrt2 = 1.0 / math.sqrt(2.0)

    def gelu(x):
        return 0.5 * x * (1.0 + lax.erf(x * inv_sqrt2))

    u = gelu(zu_ref[...])
    v = gelu(zv_ref[...])
    mu = jnp.mean(v, axis=-1, keepdims=True)
    vc = v - mu
    var = jnp.mean(vc * vc, axis=-1, keepdims=True)
    vn = (vc * lax.rsqrt(var + EPS) * lng_ref[...] + lnb_ref[...]).astype(BF16)
    row = lax.broadcasted_iota(jnp.int32, (CHUNK, CHUNK), 0)
    col = lax.broadcasted_iota(jnp.int32, (CHUNK, CHUNK), 1)
    causal = col <= row
    bst = bst_ref[...]
    for h in range(GMLP_HEADS):
        w = jnp.where(causal, ws_ref[h], 0.0).astype(BF16)
        b = bst[:, h:h + 1]
        for c in range(tm // CHUNK):
            vs = vn[c * CHUNK:(c + 1) * CHUNK, h * HEAD_DIM:(h + 1) * HEAD_DIM]
            mixed = jnp.dot(w, vs, preferred_element_type=F32) + b
            o_ref[c * CHUNK:(c + 1) * CHUNK, h * HEAD_DIM:(h + 1) * HEAD_DIM] = (
                u[c * CHUNK:(c + 1) * CHUNK, h * HEAD_DIM:(h + 1) * HEAD_DIM] * mixed).astype(o_ref.dtype)


def _gmlp(z, ln_g, ln_b, w_s, b_s, tm=256):
    return pl.pallas_call(
        functools.partial(_gmlp_kernel, tm=tm),
        grid=(SEQ // tm,),
        in_specs=[
            pl.BlockSpec((tm, GMLP_WIDTH), lambda i: (i, 0)),
            pl.BlockSpec((tm, GMLP_WIDTH), lambda i: (i, 1)),
            pl.BlockSpec((1, GMLP_WIDTH), lambda i: (0, 0)),
            pl.BlockSpec((1, GMLP_WIDTH), lambda i: (0, 0)),
            pl.BlockSpec((GMLP_HEADS, CHUNK, CHUNK), lambda i: (0, 0, 0)),
            pl.BlockSpec((CHUNK, GMLP_HEADS), lambda i: (0, 0)),
        ],
        out_specs=pl.BlockSpec((tm, GMLP_WIDTH), lambda i: (i, 0)),
        out_shape=jax.ShapeDtypeStruct((SEQ, GMLP_WIDTH), BF16),
        compiler_params=_params(("arbitrary",), 40),
        name="gmlp",
    )(z, z, ln_g.reshape(1, -1), ln_b.reshape(1, -1), w_s, b_s.T)


def _qkv_prep_kernel(pos_ref, freq_ref, sign_ref, gq_ref, gk_ref, zq_ref, zk_ref, zv_ref, q_ref, k_ref, v_ref):
    ang = pos_ref[...].astype(F32) * freq_ref[...]
    cos = jnp.cos(ang)
    sin = jnp.sin(ang) * sign_ref[...]
    scale = HEAD_DIM ** -0.5

    def prep(z_ref, g_ref, o_ref, mult):
        for s in range(DIFF_QK_WIDTH // HEAD_DIM):
            seg = _rms(z_ref[:, s * HEAD_DIM:(s + 1) * HEAD_DIM], g_ref[...])
            rot = pltpu.roll(seg, HEAD_DIM // 2, 1)
            out = seg * cos + rot * sin
            if mult != 1.0:
                out = out * mult
            o_ref[:, s * HEAD_DIM:(s + 1) * HEAD_DIM] = out.astype(o_ref.dtype)

    prep(zq_ref, gq_ref, q_ref, scale)
    prep(zk_ref, gk_ref, k_ref, 1.0)
    v_ref[...] = zv_ref[...].astype(v_ref.dtype)


def _qkv_prep(z, positions, gq, gk, tm=256):
    half = jnp.arange(0, HEAD_DIM, 2, dtype=F32) / HEAD_DIM
    inv_freq = 1.0 / (ROPE_THETA ** half)
    freq = jnp.concatenate([inv_freq, inv_freq]).reshape(1, HEAD_DIM)
    sign = jnp.concatenate([-jnp.ones((HEAD_DIM // 2,), F32), jnp.ones((HEAD_DIM // 2,), F32)]).reshape(1, HEAD_DIM)
    wide = DIFF_QK_WIDTH
    row_spec = lambda cb: pl.BlockSpec((tm, wide), lambda i: (i, cb))
    vec = pl.BlockSpec((1, HEAD_DIM), lambda i: (0, 0))
    out_spec = pl.BlockSpec((tm, wide), lambda i: (i, 0))
    shp = jax.ShapeDtypeStruct((SEQ, wide), BF16)
    return pl.pallas_call(
        _qkv_prep_kernel,
        grid=(SEQ // tm,),
        in_specs=[pl.BlockSpec((tm, 1), lambda i: (i, 0)), vec, vec, vec, vec,
                  row_spec(COL_Q // wide), row_spec(COL_K // wide), row_spec(COL_V // wide)],
        out_specs=[out_spec, out_spec, out_spec],
        out_shape=[shp, shp, shp],
        compiler_params=_params(("arbitrary",), 40),
        name="qkv_prep",
    )(positions.reshape(SEQ, 1), freq, sign, gq.reshape(1, -1), gk.reshape(1, -1), z, z, z)


def _diff_attn_kernel(lq1_ref, lk1_ref, lq2_ref, lk2_ref, sg_ref, q_ref, k_ref, v_ref, o_ref,
                      m_ref, l_ref, acc_ref, *, tq, lam_init):
    i = pl.program_id(1)
    m_ref[...] = jnp.full(m_ref.shape, -jnp.inf, F32)
    l_ref[...] = jnp.zeros(l_ref.shape, F32)
    acc_ref[...] = jnp.zeros(acc_ref.shape, F32)
    row = lax.broadcasted_iota(jnp.int32, (tq, tq), 0)
    col = lax.broadcasted_iota(jnp.int32, (tq, tq), 1)
    causal = col <= row

    def block(j, masked):
        start = pl.multiple_of(j * tq, tq)
        kb = k_ref[pl.ds(start, tq), :]
        vb = v_ref[pl.ds(start, tq), :]
        for a in range(2):
            q = q_ref[:, a * HEAD_DIM:(a + 1) * HEAD_DIM]
            k = kb[:, a * HEAD_DIM:(a + 1) * HEAD_DIM]
            s = lax.dot_general(q, k, (((1,), (1,)), ((), ())), preferred_element_type=F32)
            if masked:
                s = jnp.where(causal, s, -jnp.inf)
            m_prev = m_ref[a]
            m_new = jnp.maximum(m_prev, jnp.max(s, axis=-1, keepdims=True))
            alpha = jnp.exp(m_prev - m_new)
            p = jnp.exp(s - m_new)
            l_ref[a] = alpha * l_ref[a] + jnp.sum(p, axis=-1, keepdims=True)
            acc_ref[a] = alpha * acc_ref[a] + jnp.dot(p.astype(BF16), vb, preferred_element_type=F32)
            m_ref[a] = m_new

    def body(j, carry):
        block(j, False)
        return carry

    lax.fori_loop(0, i, body, 0)
    block(i, True)

    lam = (jnp.exp(jnp.sum(lq1_ref[...] * lk1_ref[...], axis=-1, keepdims=True))
           - jnp.exp(jnp.sum(lq2_ref[...] * lk2_ref[...], axis=-1, keepdims=True)) + lam_init)
    o = acc_ref[0] / l_ref[0] - lam * (acc_ref[1] / l_ref[1])
    o_ref[...] = (_rms(o, sg_ref[...]) * (1.0 - lam_init)).astype(o_ref.dtype)


def _diff_attn(q, k, v, lq1, lk1, lq2, lk2, subln_g, lam_init, tq=512):
    vec = pl.BlockSpec((1, HEAD_DIM), lambda h, i: (0, 0))
    kv_spec = pl.BlockSpec((SEQ, DIFF_V_DIM), lambda h, i: (0, h))
    return pl.pallas_call(
        functools.partial(_diff_attn_kernel, tq=tq, lam_init=lam_init),
        grid=(DIFF_HEADS, SEQ // tq),
        in_specs=[vec, vec, vec, vec, pl.BlockSpec((1, DIFF_V_DIM), lambda h, i: (0, 0)),
                  pl.BlockSpec((tq, DIFF_V_DIM), lambda h, i: (i, h)), kv_spec, kv_spec],
        out_specs=pl.BlockSpec((tq, DIFF_V_DIM), lambda h, i: (i, h)),
        out_shape=jax.ShapeDtypeStruct((SEQ, DIFF_WIDTH), BF16),
        scratch_shapes=[pltpu.VMEM((2, tq, 1), F32), pltpu.VMEM((2, tq, 1), F32),
                        pltpu.VMEM((2, tq, DIFF_V_DIM), F32)],
        compiler_params=_params(("arbitrary", "arbitrary"), 48),
        name="diff_attn",
    )(lq1.reshape(1, -1), lk1.reshape(1, -1), lq2.reshape(1, -1), lk2.reshape(1, -1),
      subln_g.reshape(1, -1), q, k, v)


def _mem_attn_kernel(zm_ref, kv_ref, gq_ref, gk_ref, o_ref):
    scale = MEM_HEAD_DIM ** -0.5
    for h in range(MEM_HEADS):
        lo, hi = h * MEM_HEAD_DIM, (h + 1) * MEM_HEAD_DIM
        q = (_rms(zm_ref[:, lo:hi], gq_ref[...]) * scale).astype(BF16)
        k = _rms(kv_ref[:, lo:hi], gk_ref[...]).astype(BF16)
        v = kv_ref[:, MEM_WIDTH + lo:MEM_WIDTH + hi].astype(BF16)
        s = lax.dot_general(q, k, (((1,), (1,)), ((), ())), preferred_element_type=F32)
        p = jnp.exp(s - jnp.max(s, axis=-1, keepdims=True))
        l = jnp.sum(p, axis=-1, keepdims=True)
        o = jnp.dot(p.astype(BF16), v, preferred_element_type=F32) / l
        o_ref[:, lo:hi] = o.astype(o_ref.dtype)


def _mem_attn(z, kv, gq, gk, tm=512):
    vec = pl.BlockSpec((1, MEM_HEAD_DIM), lambda i: (0, 0))
    return pl.pallas_call(
        _mem_attn_kernel,
        grid=(SEQ // tm,),
        in_specs=[pl.BlockSpec((tm, MEM_WIDTH), lambda i: (i, COL_M // MEM_WIDTH)),
                  pl.BlockSpec((MEM_LEN, 2 * MEM_WIDTH), lambda i: (0, 0)), vec, vec],
        out_specs=pl.BlockSpec((tm, MEM_WIDTH), lambda i: (i, 0)),
        out_shape=jax.ShapeDtypeStruct((SEQ, MEM_WIDTH), BF16),
        compiler_params=_params(("arbitrary",), 40),
        name="mem_attn",
    )(z, kv, gq.reshape(1, -1), gk.reshape(1, -1))


def _router_kernel(x_ref, g_ref, wr_ref, br_ref, h_ref, gate_ref, exp_ref, rank_ref, cnt_ref, carry_ref, *, tm):
    @pl.when(pl.program_id(0) == 0)
    def _():
        carry_ref[...] = jnp.zeros(carry_ref.shape, F32)

    h = _rms(x_ref[...], g_ref[...])
    _store_slabs(h_ref, h, tm)
    logits = jnp.dot(h, wr_ref[...], precision=lax.Precision.HIGHEST, preferred_element_type=F32) + br_ref[...]
    lane = lax.broadcasted_iota(jnp.int32, logits.shape, 1)
    slot = lax.broadcasted_iota(jnp.int32, (tm, TOP_K), 1)
    work = logits
    sel = jnp.zeros(logits.shape, jnp.bool_)
    hits = []
    top = None
    nums = jnp.zeros((tm, TOP_K), F32)
    experts = jnp.zeros((tm, TOP_K), jnp.int32)
    denom = jnp.zeros((tm, 1), F32)
    for k in range(TOP_K):
        m = jnp.max(work, axis=-1, keepdims=True)
        idx = jnp.min(jnp.where(work == m, lane, N_EXPERTS), axis=-1, keepdims=True)
        hit = lane == idx
        if k == 0:
            top = m
        e = jnp.exp(m - top)
        nums = jnp.where(slot == k, e, nums)
        experts = jnp.where(slot == k, idx, experts)
        denom = denom + e
        work = jnp.where(hit, -jnp.inf, work)
        sel = jnp.logical_or(sel, hit)
        hits.append(hit)
    gate_ref[...] = nums / denom
    exp_ref[...] = experts
    r = lax.broadcasted_iota(jnp.int32, (tm, tm), 0)
    c = lax.broadcasted_iota(jnp.int32, (tm, tm), 1)
    strict = jnp.where(c < r, 1.0, 0.0).astype(BF16)
    chosen = jnp.where(sel, 1.0, 0.0)
    rank = jnp.dot(strict, chosen.astype(BF16), preferred_element_type=F32) + carry_ref[...]
    ranks = jnp.zeros((tm, TOP_K), F32)
    for k in range(TOP_K):
        ranks = jnp.where(slot == k, jnp.sum(jnp.where(hits[k], rank, 0.0), axis=-1, keepdims=True), ranks)
    rank_ref[...] = ranks.astype(jnp.int32)
    carry_ref[...] = carry_ref[...] + jnp.sum(chosen, axis=0, keepdims=True)
    cnt_ref[...] = carry_ref[...].astype(jnp.int32)


def _router(x, g, w_router, b_router, tm=256):
    small = lambda: pl.BlockSpec((tm, TOP_K), lambda i: (i, 0))
    return pl.pallas_call(
        functools.partial(_router_kernel, tm=tm),
        grid=(SEQ // tm,),
        in_specs=[pl.BlockSpec((tm, D_MODEL), lambda i: (i, 0)), pl.BlockSpec((1, D_MODEL), lambda i: (0, 0)),
                  pl.BlockSpec((D_MODEL, N_EXPERTS), lambda i: (0, 0)), pl.BlockSpec((1, N_EXPERTS), lambda i: (0, 0))],
        out_specs=[pl.BlockSpec((tm * SLAB_ROWS, SLAB_LANES), lambda i: (i, 0)), small(), small(), small(),
                   pl.BlockSpec((1, N_EXPERTS), lambda i: (0, 0))],
        out_shape=[jax.ShapeDtypeStruct((SEQ * SLAB_ROWS, SLAB_LANES), jnp.uint32),
                   jax.ShapeDtypeStruct((SEQ, TOP_K), F32),
                   jax.ShapeDtypeStruct((SEQ, TOP_K), jnp.int32),
                   jax.ShapeDtypeStruct((SEQ, TOP_K), jnp.int32),
                   jax.ShapeDtypeStruct((1, N_EXPERTS), jnp.int32)],
        scratch_shapes=[pltpu.VMEM((1, N_EXPERTS), F32)],
        compiler_params=_params(("arbitrary",), 40),
        name="router",
    )(x, g.reshape(1, -1), w_router, b_router.reshape(1, -1))


def _slot_copy(src_ref, src_row, dst_ref, dst_row, sem):
    return pltpu.make_async_copy(src_ref.at[pl.ds(src_row * SLAB_ROWS, SLAB_ROWS)],
                                 dst_ref.at[pl.ds(dst_row * SLAB_ROWS, SLAB_ROWS)], sem)


def _tile_fill_copy(zero_ref, dst_ref, tile, sem):
    rows = MOE_TILE * SLAB_ROWS
    return pltpu.make_async_copy(zero_ref, dst_ref.at[pl.ds(tile * rows, rows)], sem)


def _dispatch_kernel(pstart_ref, fill_ref, exp_ref, rank_ref, h_ref, o_ref, zero_ref, sem, *, tc):
    @pl.when(pl.program_id(0) == 0)
    def _():
        zero_ref[...] = jnp.zeros(zero_ref.shape, zero_ref.dtype)

        def fill(t, carry):
            @pl.when(fill_ref[t] != 0)
            def _():
                _tile_fill_copy(zero_ref, o_ref, t, sem).start()
            return carry

        def fill_wait(t, carry):
            @pl.when(fill_ref[t] != 0)
            def _():
                _tile_fill_copy(zero_ref, o_ref, t, sem).wait()
            return carry

        lax.fori_loop(0, MOE_NT, fill, 0)
        lax.fori_loop(0, MOE_NT, fill_wait, 0)

    def issue(r, carry):
        for k in range(TOP_K):
            a = r * TOP_K + k
            slot = pstart_ref[exp_ref[0, 0, a]] + rank_ref[0, 0, a]
            _slot_copy(h_ref, r, o_ref, slot, sem).start()
        return carry

    def drain(r, carry):
        for k in range(TOP_K):
            _slot_copy(h_ref, r, o_ref, 0, sem).wait()
        return carry

    lax.fori_loop(0, tc, issue, 0)
    lax.fori_loop(0, tc, drain, 0)


def _dispatch(hp, experts4, rank4, pstart, fill_tiles, tc=128):
    nb = SEQ // tc
    idx_spec = pl.BlockSpec((1, 1, tc * TOP_K), lambda i, ps, ft: (i, 0, 0), memory_space=pltpu.SMEM)
    return pl.pallas_call(
        functools.partial(_dispatch_kernel, tc=tc),
        grid_spec=pltpu.PrefetchScalarGridSpec(
            num_scalar_prefetch=2,
            grid=(nb,),
            in_specs=[idx_spec, idx_spec,
                      pl.BlockSpec((tc * SLAB_ROWS, SLAB_LANES), lambda i, ps, ft: (i, 0))],
            out_specs=pl.BlockSpec(memory_space=pl.ANY),
            scratch_shapes=[pltpu.VMEM((MOE_TILE * SLAB_ROWS, SLAB_LANES), jnp.uint32), pltpu.SemaphoreType.DMA(())],
        ),
        out_shape=jax.ShapeDtypeStruct((MOE_ROWS * SLAB_ROWS, SLAB_LANES), jnp.uint32),
        compiler_params=_params(("arbitrary",), 40),
        name="moe_dispatch",
    )(pstart, fill_tiles, experts4.reshape(nb, 1, tc * TOP_K), rank4.reshape(nb, 1, tc * TOP_K), hp)


def _moe_up_kernel(blk_ref, nused_ref, x_ref, wg_ref, wu_ref, bg_ref, bu_ref, o_ref):
    t = pl.program_id(1)

    @pl.when(t < nused_ref[0])
    def _():
        x = _load_slabs(x_ref, MOE_TILE)

        def proj(w_ref, b_ref):
            return jnp.dot(x, w_ref[0].astype(BF16), preferred_element_type=F32) + b_ref[0]

        gate = jnp.minimum(proj(wg_ref, bg_ref), SWIGLU_LIMIT)
        up = jnp.clip(proj(wu_ref, bu_ref), -SWIGLU_LIMIT, SWIGLU_LIMIT)
        act = (up + 1.0) * (gate * jax.nn.sigmoid(SWIGLU_ALPHA * gate))
        o_ref[...] = act.astype(o_ref.dtype)

    @pl.when(t >= nused_ref[0])
    def _():
        o_ref[...] = jnp.zeros(o_ref.shape, o_ref.dtype)


def _moe_up(xs, blk_e, n_used, w_gate_up, b_gate_up, tn=512):
    nj = D_FF // tn
    last = lambda t, nu: jnp.minimum(t, nu[0] - 1)
    b3 = b_gate_up.reshape(N_EXPERTS, 1, 2 * D_FF)
    return pl.pallas_call(
        _moe_up_kernel,
        grid_spec=pltpu.PrefetchScalarGridSpec(
            num_scalar_prefetch=2,
            grid=(nj, MOE_NT),
            in_specs=[
                pl.BlockSpec((MOE_TILE * SLAB_ROWS, SLAB_LANES), lambda j, t, be, nu: (last(t, nu), 0)),
                pl.BlockSpec((1, D_MODEL, tn), lambda j, t, be, nu: (be[t], 0, j)),
                pl.BlockSpec((1, D_MODEL, tn), lambda j, t, be, nu: (be[t], 0, nj + j)),
                pl.BlockSpec((1, 1, tn), lambda j, t, be, nu: (be[t], 0, j)),
                pl.BlockSpec((1, 1, tn), lambda j, t, be, nu: (be[t], 0, nj + j)),
            ],
            out_specs=pl.BlockSpec((MOE_TILE, tn), lambda j, t, be, nu: (t, j)),
        ),
        out_shape=jax.ShapeDtypeStruct((MOE_ROWS, D_FF), BF16),
        compiler_params=_params(("arbitrary", "arbitrary"), 56),
        name="moe_up",
    )(blk_e, n_used, xs, w_gate_up, w_gate_up, b3, b3)


DOWN_TN = D_MODEL // 2
DOWN_WORDS = DOWN_TN // 2
DOWN_ROWS = DOWN_WORDS // SLAB_LANES


def _moe_down_kernel(blk_ref, nused_ref, a_ref, w_ref, b_ref, o_ref):
    t = pl.program_id(1)

    @pl.when(t < nused_ref[0])
    def _():
        y = jnp.dot(a_ref[...], w_ref[0].astype(BF16), preferred_element_type=F32) + b_ref[0]
        words = pltpu.pack_elementwise([y[:, :DOWN_WORDS], y[:, DOWN_WORDS:]], packed_dtype=BF16)
        for s in range(DOWN_ROWS):
            o_ref[:, s, :] = words[:, s * SLAB_LANES:(s + 1) * SLAB_LANES]

    @pl.when(t >= nused_ref[0])
    def _():
        o_ref[...] = jnp.zeros(o_ref.shape, o_ref.dtype)


def _moe_down(act, blk_e, n_used, w_down, b_down):
    nj = D_MODEL // DOWN_TN
    last = lambda t, nu: jnp.minimum(t, nu[0] - 1)
    return pl.pallas_call(
        _moe_down_kernel,
        grid_spec=pltpu.PrefetchScalarGridSpec(
            num_scalar_prefetch=2,
            grid=(nj, MOE_NT),
            in_specs=[
                pl.BlockSpec((MOE_TILE, D_FF), lambda j, t, be, nu: (last(t, nu), 0)),
                pl.BlockSpec((1, D_FF, DOWN_TN), lambda j, t, be, nu: (be[t], 0, j)),
                pl.BlockSpec((1, 1, DOWN_TN), lambda j, t, be, nu: (be[t], 0, j)),
            ],
            out_specs=pl.BlockSpec((MOE_TILE, DOWN_ROWS, SLAB_LANES), lambda j, t, be, nu: (t, j, 0)),
        ),
        out_shape=jax.ShapeDtypeStruct((MOE_ROWS, SLAB_ROWS, SLAB_LANES), jnp.uint32),
        compiler_params=_params(("arbitrary", "arbitrary"), 56),
        name="moe_down",
    )(blk_e, n_used, act, w_down, b_down.reshape(N_EXPERTS, 1, D_MODEL)).reshape(MOE_ROWS * SLAB_ROWS, SLAB_LANES)


def _combine_kernel(pstart_ref, exp_ref, rank_ref, ys_ref, x_ref, g_ref, o_ref, buf_ref, sem, *, tc):
    def issue(r, carry):
        for k in range(TOP_K):
            a = r * TOP_K + k
            slot = pstart_ref[exp_ref[0, 0, a]] + rank_ref[0, 0, a]
            _slot_copy(ys_ref, slot, buf_ref.at[k], r, sem).start()
        return carry

    def drain(r, carry):
        for k in range(TOP_K):
            _slot_copy(ys_ref, 0, buf_ref.at[k], r, sem).wait()
        return carry

    lax.fori_loop(0, tc, issue, 0)
    lax.fori_loop(0, tc, drain, 0)

    g = g_ref[...]
    for s in range(SLAB_ROWS):
        lo_sum = None
        hi_sum = None
        for k in range(TOP_K):
            lo, hi = _unpack_bf16(buf_ref[k, pl.ds(s, tc, stride=SLAB_ROWS), :])
            gk = g[:, k:k + 1]
            lo_sum = gk * lo if lo_sum is None else lo_sum + gk * lo
            hi_sum = gk * hi if hi_sum is None else hi_sum + gk * hi
        c = (s // DOWN_ROWS) * DOWN_TN + (s % DOWN_ROWS) * SLAB_LANES
        o_ref[:, c:c + SLAB_LANES] = x_ref[:, c:c + SLAB_LANES] + lo_sum
        c += DOWN_WORDS
        o_ref[:, c:c + SLAB_LANES] = x_ref[:, c:c + SLAB_LANES] + hi_sum


def _combine(ys, x2, experts4, rank4, gates4, pstart, tc=128):
    nb = SEQ // tc
    idx_spec = pl.BlockSpec((1, 1, tc * TOP_K), lambda i, ps: (i, 0, 0), memory_space=pltpu.SMEM)
    return pl.pallas_call(
        functools.partial(_combine_kernel, tc=tc),
        grid_spec=pltpu.PrefetchScalarGridSpec(
            num_scalar_prefetch=1,
            grid=(nb,),
            in_specs=[idx_spec, idx_spec,
                      pl.BlockSpec(memory_space=pl.ANY),
                      pl.BlockSpec((tc, D_MODEL), lambda i, ps: (i, 0)),
                      pl.BlockSpec((tc, TOP_K), lambda i, ps: (i, 0))],
            out_specs=pl.BlockSpec((tc, D_MODEL), lambda i, ps: (i, 0)),
            scratch_shapes=[pltpu.VMEM((TOP_K, tc * SLAB_ROWS, SLAB_LANES), jnp.uint32), pltpu.SemaphoreType.DMA(())],
        ),
        out_shape=jax.ShapeDtypeStruct((SEQ, D_MODEL), F32),
        compiler_params=_params(("arbitrary",), 40),
        name="moe_combine",
    )(pstart, experts4.reshape(nb, 1, tc * TOP_K), rank4.reshape(nb, 1, tc * TOP_K), ys, x2, gates4)


def kernel(x, mem, positions, attn_norm_g, w_in, gmlp_ln_g, gmlp_ln_b, gmlp_w_s, gmlp_b_s, diff_q_norm_g, diff_k_norm_g, diff_lambda_q1, diff_lambda_k1, diff_lambda_q2, diff_lambda_k2, diff_subln_g, mem_norm_g, w_mem_kv, mem_q_norm_g, mem_k_norm_g, w_out, ffn_norm_g, w_router, b_router, w_gate_up, b_gate_up, w_down, b_down):
    depth = attn_norm_g.shape[0]
    xs = x.reshape(SEQ, D_MODEL)
    mem2 = mem.reshape(MEM_LEN, D_MODEL)
    for i in range(depth):
        lam_init = 0.8 - 0.6 * math.exp(-0.3 * i)
        h = _rmsnorm_rows(xs, attn_norm_g[i], tm=256)
        z = _matmul([h], w_in[i], None, F32, tm=1024, tn=512, name="in_proj")

        y_g = _gmlp(z, gmlp_ln_g[i], gmlp_ln_b[i], gmlp_w_s[i], gmlp_b_s[i])

        q, k, v = _qkv_prep(z, positions, diff_q_norm_g[i], diff_k_norm_g[i])
        y_d = _diff_attn(q, k, v, diff_lambda_q1[i], diff_lambda_k1[i], diff_lambda_q2[i], diff_lambda_k2[i],
                         diff_subln_g[i], lam_init)

        hm = _rmsnorm_rows(mem2, mem_norm_g[i], tm=256)
        kv = _matmul([hm], w_mem_kv[i], None, F32, tm=MEM_LEN, tn=512, name="mem_kv_proj")
        y_m = _mem_attn(z, kv, mem_q_norm_g[i], mem_k_norm_g[i])

        x2 = _matmul([y_g, y_d, y_m], w_out[i], xs, F32, tm=1024, tn=512, name="out_proj")

        hp, gates4, experts4, rank4, counts = _router(x2, ffn_norm_g[i], w_router[i], b_router[i])

        counts = counts[0]
        padded = (counts + MOE_TILE - 1) // MOE_TILE * MOE_TILE
        pend = jnp.cumsum(padded)
        pstart = pend - padded
        n_used = (pend[-1] // MOE_TILE).astype(jnp.int32).reshape(1)
        blk_start = jnp.arange(MOE_NT, dtype=jnp.int32) * MOE_TILE
        blk_e = jnp.minimum(jnp.sum(pend[None, :] <= blk_start[:, None], axis=1), N_EXPERTS - 1).astype(jnp.int32)
        blk_end = blk_start + MOE_TILE
        fill_tiles = jnp.logical_or(jnp.any(blk_end[:, None] == pend[None, :], axis=1),
                                    blk_start >= pend[-1]).astype(jnp.int32)

        xg = _dispatch(hp, experts4, rank4, pstart, fill_tiles)
        act = _moe_up(xg, blk_e, n_used, w_gate_up[i], b_gate_up[i])
        ys = _moe_down(act, blk_e, n_used, w_down[i], b_down[i])
        xs = _combine(ys, x2, experts4, rank4, gates4, pstart)
    return xs.reshape(x.shape)
```

```python
import functools
import math

import jax
import jax.numpy as jnp
from jax import lax
from jax.experimental import pallas as pl
from jax.experimental.pallas import tpu as pltpu

F32 = jnp.float32
BF16 = jnp.bfloat16

D_MODEL = 4096
SEQ = 8192
HEAD_DIM = 128
CHUNK = 128
GMLP_WIDTH = 1024
GMLP_HEADS = 8
DIFF_HEADS = 8
DIFF_V_DIM = 256
DIFF_QK_WIDTH = 2048
DIFF_WIDTH = 2048
MEM_WIDTH = 1024
MEM_HEADS = 4
MEM_HEAD_DIM = 256
MEM_LEN = 256
IN_WIDTH = 9216
N_EXPERTS = 32
TOP_K = 4
D_FF = 1536
SWIGLU_LIMIT = 7.0
SWIGLU_ALPHA = 1.702
ROPE_THETA = 10000.0
EPS = 1e-6

COL_Q = 2 * GMLP_WIDTH
COL_K = COL_Q + DIFF_QK_WIDTH
COL_V = COL_K + DIFF_QK_WIDTH
COL_M = COL_V + DIFF_WIDTH

MOE_TILE = 256
MOE_ROWS = ((SEQ * TOP_K + N_EXPERTS * (MOE_TILE - 1) + MOE_TILE - 1) // MOE_TILE) * MOE_TILE
MOE_NT = MOE_ROWS // MOE_TILE

MIB = 1 << 20


def _params(sem, vmem_mib):
    return pltpu.CompilerParams(dimension_semantics=sem, vmem_limit_bytes=vmem_mib * MIB)


def _rms(x, g):
    ms = jnp.mean(x * x, axis=-1, keepdims=True)
    return x * lax.rsqrt(ms + EPS) * g


SLAB_LANES = 128
SLAB_ROWS = D_MODEL // 2 // SLAB_LANES


def _unpack_bf16(words):
    lo = pltpu.unpack_elementwise(words, index=0, packed_dtype=BF16, unpacked_dtype=F32)
    hi = pltpu.unpack_elementwise(words, index=1, packed_dtype=BF16, unpacked_dtype=F32)
    return lo, hi


def _store_slabs(o_ref, h, rows):
    half = D_MODEL // 2
    for s in range(SLAB_ROWS):
        lo = h[:, s * SLAB_LANES:(s + 1) * SLAB_LANES]
        hi = h[:, half + s * SLAB_LANES:half + (s + 1) * SLAB_LANES]
        o_ref[pl.ds(s, rows, stride=SLAB_ROWS), :] = pltpu.pack_elementwise([lo, hi], packed_dtype=BF16)


def _load_slabs(x_ref, rows):
    los, his = [], []
    for s in range(SLAB_ROWS):
        lo, hi = _unpack_bf16(x_ref[pl.ds(s, rows, stride=SLAB_ROWS), :])
        los.append(lo.astype(BF16))
        his.append(hi.astype(BF16))
    return jnp.concatenate(los + his, axis=1)


def _rmsnorm_kernel(x_ref, g_ref, o_ref):
    o_ref[...] = _rms(x_ref[...], g_ref[...]).astype(o_ref.dtype)


def _rmsnorm_rows(x, g, tm):
    m, d = x.shape
    return pl.pallas_call(
        _rmsnorm_kernel,
        grid=(m // tm,),
        in_specs=[pl.BlockSpec((tm, d), lambda i: (i, 0)), pl.BlockSpec((1, d), lambda i: (0, 0))],
        out_specs=pl.BlockSpec((tm, d), lambda i: (i, 0)),
        out_shape=jax.ShapeDtypeStruct((m, d), BF16),
        compiler_params=_params(("arbitrary",), 40),
        name="rmsnorm_rows",
    )(x, g.reshape(1, d))


def _matmul_kernel(*refs, widths, has_res):
    n = len(widths)
    lhs = refs[:n]
    w_ref = refs[n]
    res_ref = refs[n + 1] if has_res else None
    o_ref = refs[n + 1 + has_res]

    acc = None
    off = 0
    for r, kw in zip(lhs, widths):
        part = jnp.dot(r[...], w_ref[off:off + kw, :].astype(BF16), preferred_element_type=F32)
        acc = part if acc is None else acc + part
        off += kw
    if has_res:
        acc = acc + res_ref[...]
    o_ref[...] = acc.astype(o_ref.dtype)


def _matmul(lhs_list, w, res, out_dtype, tm, tn, name):
    m = lhs_list[0].shape[0]
    k, n = w.shape
    widths = tuple(a.shape[1] for a in lhs_list)
    assert sum(widths) == k
    in_specs = [pl.BlockSpec((tm, kw), lambda j, i: (i, 0)) for kw in widths]
    in_specs.append(pl.BlockSpec((k, tn), lambda j, i: (0, j)))
    args = list(lhs_list) + [w]
    if res is not None:
        in_specs.append(pl.BlockSpec((tm, tn), lambda j, i: (i, j)))
        args.append(res)
    return pl.pallas_call(
        functools.partial(_matmul_kernel, widths=widths, has_res=res is not None),
        grid=(n // tn, m // tm),
        in_specs=in_specs,
        out_specs=pl.BlockSpec((tm, tn), lambda j, i: (i, j)),
        out_shape=jax.ShapeDtypeStruct((m, n), out_dtype),
        compiler_params=_params(("arbitrary", "arbitrary"), 56),
        name=name,
    )(*args)


def _gmlp_kernel(zu_ref, zv_ref, lng_ref, lnb_ref, ws_ref, bst_ref, o_ref, *, tm):
    inv_sqrt2 = 1.0 / math.sqrt(2.0)

    def gelu(x):
        return 0.5 * x * (1.0 + lax.erf(x * inv_sqrt2))

    u = gelu(zu_ref[...])
    v = gelu(zv_ref[...])
    mu = jnp.mean(v, axis=-1, keepdims=True)
    vc = v - mu
    var = jnp.mean(vc * vc, axis=-1, keepdims=True)
    vn = (vc * lax.rsqrt(var + EPS) * lng_ref[...] + lnb_ref[...]).astype(BF16)
    row = lax.broadcasted_iota(jnp.int32, (CHUNK, CHUNK), 0)
    col = lax.broadcasted_iota(jnp.int32, (CHUNK, CHUNK), 1)
    causal = col <= row
    bst = bst_ref[...]
    for h in range(GMLP_HEADS):
        w = jnp.where(causal, ws_ref[h], 0.0).astype(BF16)
        b = bst[:, h:h + 1]
        for c in range(tm // CHUNK):
            vs = vn[c * CHUNK:(c + 1) * CHUNK, h * HEAD_DIM:(h + 1) * HEAD_DIM]
            mixed = jnp.dot(w, vs, preferred_element_type=F32) + b
            o_ref[c * CHUNK:(c + 1) * CHUNK, h * HEAD_DIM:(h + 1) * HEAD_DIM] = (
                u[c * CHUNK:(c + 1) * CHUNK, h * HEAD_DIM:(h + 1) * HEAD_DIM] * mixed).astype(o_ref.dtype)


def _gmlp(z, ln_g, ln_b, w_s, b_s, tm=256):
    return pl.pallas_call(
        functools.partial(_gmlp_kernel, tm=tm),
        grid=(SEQ // tm,),
        in_specs=[
            pl.BlockSpec((tm, GMLP_WIDTH), lambda i: (i, 0)),
            pl.BlockSpec((tm, GMLP_WIDTH), lambda i: (i, 1)),
            pl.BlockSpec((1, GMLP_WIDTH), lambda i: (0, 0)),
            pl.BlockSpec((1, GMLP_WIDTH), lambda i: (0, 0)),
            pl.BlockSpec((GMLP_HEADS, CHUNK, CHUNK), lambda i: (0, 0, 0)),
            pl.BlockSpec((CHUNK, GMLP_HEADS), lambda i: (0, 0)),
        ],
        out_specs=pl.BlockSpec((tm, GMLP_WIDTH), lambda i: (i, 0)),
        out_shape=jax.ShapeDtypeStruct((SEQ, GMLP_WIDTH), BF16),
        compiler_params=_params(("arbitrary",), 40),
        name="gmlp",
    )(z, z, ln_g.reshape(1, -1), ln_b.reshape(1, -1), w_s, b_s.T)


def _qkv_prep_kernel(pos_ref, freq_ref, sign_ref, gq_ref, gk_ref, zq_ref, zk_ref, zv_ref, q_ref, k_ref, v_ref):
    ang = pos_ref[...].astype(F32) * freq_ref[...]
    cos = jnp.cos(ang)
    sin = jnp.sin(ang) * sign_ref[...]
    scale = HEAD_DIM ** -0.5 * math.log2(math.e)

    def prep(z_ref, g_ref, o_ref, mult):
        for s in range(DIFF_QK_WIDTH // HEAD_DIM):
            seg = _rms(z_ref[:, s * HEAD_DIM:(s + 1) * HEAD_DIM], g_ref[...])
            rot = pltpu.roll(seg, HEAD_DIM // 2, 1)
            out = seg * cos + rot * sin
            if mult != 1.0:
                out = out * mult
            o_ref[:, s * HEAD_DIM:(s + 1) * HEAD_DIM] = out.astype(o_ref.dtype)

    prep(zq_ref, gq_ref, q_ref, scale)
    prep(zk_ref, gk_ref, k_ref, 1.0)
    v_ref[...] = zv_ref[...].astype(v_ref.dtype)


def _qkv_prep(z, positions, gq, gk, tm=256):
    half = jnp.arange(0, HEAD_DIM, 2, dtype=F32) / HEAD_DIM
    inv_freq = 1.0 / (ROPE_THETA ** half)
    freq = jnp.concatenate([inv_freq, inv_freq]).reshape(1, HEAD_DIM)
    sign = jnp.concatenate([-jnp.ones((HEAD_DIM // 2,), F32), jnp.ones((HEAD_DIM // 2,), F32)]).reshape(1, HEAD_DIM)
    wide = DIFF_QK_WIDTH
    row_spec = lambda cb: pl.BlockSpec((tm, wide), lambda i: (i, cb))
    vec = pl.BlockSpec((1, HEAD_DIM), lambda i: (0, 0))
    out_spec = pl.BlockSpec((tm, wide), lambda i: (i, 0))
    shp = jax.ShapeDtypeStruct((SEQ, wide), BF16)
    return pl.pallas_call(
        _qkv_prep_kernel,
        grid=(SEQ // tm,),
        in_specs=[pl.BlockSpec((tm, 1), lambda i: (i, 0)), vec, vec, vec, vec,
                  row_spec(COL_Q // wide), row_spec(COL_K // wide), row_spec(COL_V // wide)],
        out_specs=[out_spec, out_spec, out_spec],
        out_shape=[shp, shp, shp],
        compiler_params=_params(("arbitrary",), 40),
        name="qkv_prep",
    )(positions.reshape(SEQ, 1), freq, sign, gq.reshape(1, -1), gk.reshape(1, -1), z, z, z)


KV_UNROLL = 4
def _diff_attn_kernel(lq1_ref, lk1_ref, lq2_ref, lk2_ref, sg_ref, q_ref, k_ref, v_ref, o_ref,
                      m_ref, l_ref, acc_ref, *, tq, lam_init):
    i = pl.program_id(1)
    m_ref[...] = jnp.full(m_ref.shape, -jnp.inf, F32)
    l_ref[...] = jnp.zeros(l_ref.shape, F32)
    acc_ref[...] = jnp.zeros(acc_ref.shape, F32)
    lanes = tq // 128

    def block(j, masked):
        start = pl.multiple_of(j * tq, tq)
        kb = k_ref[pl.ds(start, tq), :]
        vb = v_ref[pl.ds(start, tq), :]
        for a in range(2):
            q = q_ref[:, a * HEAD_DIM:(a + 1) * HEAD_DIM]
            k = kb[:, a * HEAD_DIM:(a + 1) * HEAD_DIM]
            s = lax.dot_general(q, k, (((1,), (1,)), ((), ())), preferred_element_type=F32)
            if masked:
                row = lax.broadcasted_iota(jnp.int32, (tq, tq), 0)
                col = lax.broadcasted_iota(jnp.int32, (tq, tq), 1)
                s = jnp.where(col <= row, s, -jnp.inf)
            m_prev = m_ref[a]
            m_new = jnp.maximum(m_prev, jnp.max(s, axis=-1, keepdims=True))
            alpha = jnp.exp2(m_prev - m_new)
            p = jnp.exp2(s - jnp.concatenate([m_new] * lanes, axis=1))
            psum = p[:, 0:128]
            for c in range(1, lanes):
                psum = psum + p[:, c * 128:(c + 1) * 128]
            l_ref[a] = alpha * l_ref[a] + psum
            acc_ref[a] = (jnp.concatenate([alpha] * (DIFF_V_DIM // 128), axis=1) * acc_ref[a]
                          + jnp.dot(p.astype(BF16), vb, preferred_element_type=F32))
            m_ref[a] = m_new

    def group(g, carry):
        for u in range(KV_UNROLL):
            block(KV_UNROLL * g + u, False)
        return carry

    def single(j, carry):
        block(j, False)
        return carry

    groups = i // KV_UNROLL
    lax.fori_loop(0, groups, group, 0)
    lax.fori_loop(groups * KV_UNROLL, i, single, 0)
    block(i, True)

    lam = (jnp.exp(jnp.sum(lq1_ref[...] * lk1_ref[...], axis=-1, keepdims=True))
           - jnp.exp(jnp.sum(lq2_ref[...] * lk2_ref[...], axis=-1, keepdims=True)) + lam_init)
    l1 = jnp.sum(l_ref[0], axis=-1, keepdims=True)
    l2 = jnp.sum(l_ref[1], axis=-1, keepdims=True)
    o = acc_ref[0] / l1 - lam * (acc_ref[1] / l2)
    o_ref[...] = (_rms(o, sg_ref[...]) * (1.0 - lam_init)).astype(o_ref.dtype)


def _diff_attn(q, k, v, lq1, lk1, lq2, lk2, subln_g, lam_init, tq=512):
    vec = pl.BlockSpec((1, HEAD_DIM), lambda h, i: (0, 0))
    kv_spec = pl.BlockSpec((SEQ, DIFF_V_DIM), lambda h, i: (0, h))
    return pl.pallas_call(
        functools.partial(_diff_attn_kernel, tq=tq, lam_init=lam_init),
        grid=(DIFF_HEADS, SEQ // tq),
        in_specs=[vec, vec, vec, vec, pl.BlockSpec((1, DIFF_V_DIM), lambda h, i: (0, 0)),
                  pl.BlockSpec((tq, DIFF_V_DIM), lambda h, i: (i, h)), kv_spec, kv_spec],
        out_specs=pl.BlockSpec((tq, DIFF_V_DIM), lambda h, i: (i, h)),
        out_shape=jax.ShapeDtypeStruct((SEQ, DIFF_WIDTH), BF16),
        scratch_shapes=[pltpu.VMEM((2, tq, 128), F32), pltpu.VMEM((2, tq, 128), F32),
                        pltpu.VMEM((2, tq, DIFF_V_DIM), F32)],
        compiler_params=_params(("arbitrary", "arbitrary"), 48),
        name="diff_attn",
    )(lq1.reshape(1, -1), lk1.reshape(1, -1), lq2.reshape(1, -1), lk2.reshape(1, -1),
      subln_g.reshape(1, -1), q, k, v)


def _mem_attn_kernel(zm_ref, kv_ref, gq_ref, gk_ref, o_ref):
    scale = MEM_HEAD_DIM ** -0.5
    for h in range(MEM_HEADS):
        lo, hi = h * MEM_HEAD_DIM, (h + 1) * MEM_HEAD_DIM
        q = (_rms(zm_ref[:, lo:hi], gq_ref[...]) * scale).astype(BF16)
        k = _rms(kv_ref[:, lo:hi], gk_ref[...]).astype(BF16)
        v = kv_ref[:, MEM_WIDTH + lo:MEM_WIDTH + hi].astype(BF16)
        s = lax.dot_general(q, k, (((1,), (1,)), ((), ())), preferred_element_type=F32)
        p = jnp.exp(s - jnp.max(s, axis=-1, keepdims=True))
        l = jnp.sum(p, axis=-1, keepdims=True)
        o = jnp.dot(p.astype(BF16), v, preferred_element_type=F32) / l
        o_ref[:, lo:hi] = o.astype(o_ref.dtype)


def _mem_attn(z, kv, gq, gk, tm=512):
    vec = pl.BlockSpec((1, MEM_HEAD_DIM), lambda i: (0, 0))
    return pl.pallas_call(
        _mem_attn_kernel,
        grid=(SEQ // tm,),
        in_specs=[pl.BlockSpec((tm, MEM_WIDTH), lambda i: (i, COL_M // MEM_WIDTH)),
                  pl.BlockSpec((MEM_LEN, 2 * MEM_WIDTH), lambda i: (0, 0)), vec, vec],
        out_specs=pl.BlockSpec((tm, MEM_WIDTH), lambda i: (i, 0)),
        out_shape=jax.ShapeDtypeStruct((SEQ, MEM_WIDTH), BF16),
        compiler_params=_params(("arbitrary",), 40),
        name="mem_attn",
    )(z, kv, gq.reshape(1, -1), gk.reshape(1, -1))


def _router_kernel(x_ref, g_ref, wr_ref, br_ref, h_ref, gate_ref, exp_ref, rank_ref, cnt_ref, carry_ref, *, tm):
    @pl.when(pl.program_id(0) == 0)
    def _():
        carry_ref[...] = jnp.zeros(carry_ref.shape, F32)

    h = _rms(x_ref[...], g_ref[...])
    _store_slabs(h_ref, h, tm)
    logits = jnp.dot(h, wr_ref[...], precision=lax.Precision.HIGHEST, preferred_element_type=F32) + br_ref[...]
    lane = lax.broadcasted_iota(jnp.int32, logits.shape, 1)
    slot = lax.broadcasted_iota(jnp.int32, (tm, TOP_K), 1)
    work = logits
    sel = jnp.zeros(logits.shape, jnp.bool_)
    hits = []
    top = None
    nums = jnp.zeros((tm, TOP_K), F32)
    experts = jnp.zeros((tm, TOP_K), jnp.int32)
    denom = jnp.zeros((tm, 1), F32)
    for k in range(TOP_K):
        m = jnp.max(work, axis=-1, keepdims=True)
        idx = jnp.min(jnp.where(work == m, lane, N_EXPERTS), axis=-1, keepdims=True)
        hit = lane == idx
        if k == 0:
            top = m
        e = jnp.exp(m - top)
        nums = jnp.where(slot == k, e, nums)
        experts = jnp.where(slot == k, idx, experts)
        denom = denom + e
        work = jnp.where(hit, -jnp.inf, work)
        sel = jnp.logical_or(sel, hit)
        hits.append(hit)
    gate_ref[...] = nums / denom
    exp_ref[...] = experts
    r = lax.broadcasted_iota(jnp.int32, (tm, tm), 0)
    c = lax.broadcasted_iota(jnp.int32, (tm, tm), 1)
    strict = jnp.where(c < r, 1.0, 0.0).astype(BF16)
    chosen = jnp.where(sel, 1.0, 0.0)
    rank = jnp.dot(strict, chosen.astype(BF16), preferred_element_type=F32) + carry_ref[...]
    ranks = jnp.zeros((tm, TOP_K), F32)
    for k in range(TOP_K):
        ranks = jnp.where(slot == k, jnp.sum(jnp.where(hits[k], rank, 0.0), axis=-1, keepdims=True), ranks)
    rank_ref[...] = ranks.astype(jnp.int32)
    carry_ref[...] = carry_ref[...] + jnp.sum(chosen, axis=0, keepdims=True)
    cnt_ref[...] = carry_ref[...].astype(jnp.int32)


def _router(x, g, w_router, b_router, tm=256):
    small = lambda: pl.BlockSpec((tm, TOP_K), lambda i: (i, 0))
    return pl.pallas_call(
        functools.partial(_router_kernel, tm=tm),
        grid=(SEQ // tm,),
        in_specs=[pl.BlockSpec((tm, D_MODEL), lambda i: (i, 0)), pl.BlockSpec((1, D_MODEL), lambda i: (0, 0)),
                  pl.BlockSpec((D_MODEL, N_EXPERTS), lambda i: (0, 0)), pl.BlockSpec((1, N_EXPERTS), lambda i: (0, 0))],
        out_specs=[pl.BlockSpec((tm * SLAB_ROWS, SLAB_LANES), lambda i: (i, 0)), small(), small(), small(),
                   pl.BlockSpec((1, N_EXPERTS), lambda i: (0, 0))],
        out_shape=[jax.ShapeDtypeStruct((SEQ * SLAB_ROWS, SLAB_LANES), jnp.uint32),
                   jax.ShapeDtypeStruct((SEQ, TOP_K), F32),
                   jax.ShapeDtypeStruct((SEQ, TOP_K), jnp.int32),
                   jax.ShapeDtypeStruct((SEQ, TOP_K), jnp.int32),
                   jax.ShapeDtypeStruct((1, N_EXPERTS), jnp.int32)],
        scratch_shapes=[pltpu.VMEM((1, N_EXPERTS), F32)],
        compiler_params=_params(("arbitrary",), 40),
        name="router",
    )(x, g.reshape(1, -1), w_router, b_router.reshape(1, -1))


def _slot_copy(src_ref, src_row, dst_ref, dst_row, sem):
    return pltpu.make_async_copy(src_ref.at[pl.ds(src_row * SLAB_ROWS, SLAB_ROWS)],
                                 dst_ref.at[pl.ds(dst_row * SLAB_ROWS, SLAB_ROWS)], sem)


def _tile_fill_copy(zero_ref, dst_ref, tile, sem):
    rows = MOE_TILE * SLAB_ROWS
    return pltpu.make_async_copy(zero_ref, dst_ref.at[pl.ds(tile * rows, rows)], sem)


def _dispatch_kernel(pstart_ref, fill_ref, exp_ref, rank_ref, h_ref, o_ref, zero_ref, sem, *, tc):
    @pl.when(pl.program_id(0) == 0)
    def _():
        zero_ref[...] = jnp.zeros(zero_ref.shape, zero_ref.dtype)

        def fill(t, carry):
            @pl.when(fill_ref[t] != 0)
            def _():
                _tile_fill_copy(zero_ref, o_ref, t, sem).start()
            return carry

        def fill_wait(t, carry):
            @pl.when(fill_ref[t] != 0)
            def _():
                _tile_fill_copy(zero_ref, o_ref, t, sem).wait()
            return carry

        lax.fori_loop(0, MOE_NT, fill, 0)
        lax.fori_loop(0, MOE_NT, fill_wait, 0)

    def issue(r, carry):
        for k in range(TOP_K):
            a = r * TOP_K + k
            slot = pstart_ref[exp_ref[0, 0, a]] + rank_ref[0, 0, a]
            _slot_copy(h_ref, r, o_ref, slot, sem).start()
        return carry

    def drain(r, carry):
        for k in range(TOP_K):
            _slot_copy(h_ref, r, o_ref, 0, sem).wait()
        return carry

    lax.fori_loop(0, tc, issue, 0, unroll=4)
    lax.fori_loop(0, tc, drain, 0, unroll=4)


def _dispatch(hp, experts4, rank4, pstart, fill_tiles, tc=128):
    nb = SEQ // tc
    idx_spec = pl.BlockSpec((1, 1, tc * TOP_K), lambda i, ps, ft: (i, 0, 0), memory_space=pltpu.SMEM)
    return pl.pallas_call(
        functools.partial(_dispatch_kernel, tc=tc),
        grid_spec=pltpu.PrefetchScalarGridSpec(
            num_scalar_prefetch=2,
            grid=(nb,),
            in_specs=[idx_spec, idx_spec,
                      pl.BlockSpec((tc * SLAB_ROWS, SLAB_LANES), lambda i, ps, ft: (i, 0))],
            out_specs=pl.BlockSpec(memory_space=pl.ANY),
            scratch_shapes=[pltpu.VMEM((MOE_TILE * SLAB_ROWS, SLAB_LANES), jnp.uint32), pltpu.SemaphoreType.DMA(())],
        ),
        out_shape=jax.ShapeDtypeStruct((MOE_ROWS * SLAB_ROWS, SLAB_LANES), jnp.uint32),
        compiler_params=_params(("arbitrary",), 40),
        name="moe_dispatch",
    )(pstart, fill_tiles, experts4.reshape(nb, 1, tc * TOP_K), rank4.reshape(nb, 1, tc * TOP_K), hp)


def _moe_up_kernel(blk_ref, nused_ref, x_ref, wg_ref, wu_ref, bg_ref, bu_ref, o_ref):
    t = pl.program_id(1)

    @pl.when(t < nused_ref[0])
    def _():
        x = _load_slabs(x_ref, MOE_TILE)

        def proj(w_ref, b_ref):
            return jnp.dot(x, w_ref[0].astype(BF16), preferred_element_type=F32) + b_ref[0]

        gate = jnp.minimum(proj(wg_ref, bg_ref), SWIGLU_LIMIT)
        up = jnp.clip(proj(wu_ref, bu_ref), -SWIGLU_LIMIT, SWIGLU_LIMIT)
        act = (up + 1.0) * (gate * jax.nn.sigmoid(SWIGLU_ALPHA * gate))
        o_ref[...] = act.astype(o_ref.dtype)

    @pl.when(t >= nused_ref[0])
    def _():
        o_ref[...] = jnp.zeros(o_ref.shape, o_ref.dtype)


def _moe_up(xs, blk_e, n_used, w_gate_up, b_gate_up, tn=512):
    nj = D_FF // tn
    last = lambda t, nu: jnp.minimum(t, nu[0] - 1)
    b3 = b_gate_up.reshape(N_EXPERTS, 1, 2 * D_FF)
    return pl.pallas_call(
        _moe_up_kernel,
        grid_spec=pltpu.PrefetchScalarGridSpec(
            num_scalar_prefetch=2,
            grid=(nj, MOE_NT),
            in_specs=[
                pl.BlockSpec((MOE_TILE * SLAB_ROWS, SLAB_LANES), lambda j, t, be, nu: (last(t, nu), 0)),
                pl.BlockSpec((1, D_MODEL, tn), lambda j, t, be, nu: (be[t], 0, j)),
                pl.BlockSpec((1, D_MODEL, tn), lambda j, t, be, nu: (be[t], 0, nj + j)),
                pl.BlockSpec((1, 1, tn), lambda j, t, be, nu: (be[t], 0, j)),
                pl.BlockSpec((1, 1, tn), lambda j, t, be, nu: (be[t], 0, nj + j)),
            ],
            out_specs=pl.BlockSpec((MOE_TILE, tn), lambda j, t, be, nu: (t, j)),
        ),
        out_shape=jax.ShapeDtypeStruct((MOE_ROWS, D_FF), BF16),
        compiler_params=_params(("arbitrary", "arbitrary"), 56),
        name="moe_up",
    )(blk_e, n_used, xs, w_gate_up, w_gate_up, b3, b3)


DOWN_TN = D_MODEL // 2
DOWN_WORDS = DOWN_TN // 2
DOWN_ROWS = DOWN_WORDS // SLAB_LANES


def _moe_down_kernel(blk_ref, nused_ref, a_ref, w_ref, b_ref, o_ref):
    t = pl.program_id(1)

    @pl.when(t < nused_ref[0])
    def _():
        y = jnp.dot(a_ref[...], w_ref[0].astype(BF16), preferred_element_type=F32) + b_ref[0]
        words = pltpu.pack_elementwise([y[:, :DOWN_WORDS], y[:, DOWN_WORDS:]], packed_dtype=BF16)
        for s in range(DOWN_ROWS):
            o_ref[:, s, :] = words[:, s * SLAB_LANES:(s + 1) * SLAB_LANES]

    @pl.when(t >= nused_ref[0])
    def _():
        o_ref[...] = jnp.zeros(o_ref.shape, o_ref.dtype)


def _moe_down(act, blk_e, n_used, w_down, b_down):
    nj = D_MODEL // DOWN_TN
    last = lambda t, nu: jnp.minimum(t, nu[0] - 1)
    return pl.pallas_call(
        _moe_down_kernel,
        grid_spec=pltpu.PrefetchScalarGridSpec(
            num_scalar_prefetch=2,
            grid=(nj, MOE_NT),
            in_specs=[
                pl.BlockSpec((MOE_TILE, D_FF), lambda j, t, be, nu: (last(t, nu), 0)),
                pl.BlockSpec((1, D_FF, DOWN_TN), lambda j, t, be, nu: (be[t], 0, j)),
                pl.BlockSpec((1, 1, DOWN_TN), lambda j, t, be, nu: (be[t], 0, j)),
            ],
            out_specs=pl.BlockSpec((MOE_TILE, DOWN_ROWS, SLAB_LANES), lambda j, t, be, nu: (t, j, 0)),
        ),
        out_shape=jax.ShapeDtypeStruct((MOE_ROWS, SLAB_ROWS, SLAB_LANES), jnp.uint32),
        compiler_params=_params(("arbitrary", "arbitrary"), 56),
        name="moe_down",
    )(blk_e, n_used, act, w_down, b_down.reshape(N_EXPERTS, 1, D_MODEL)).reshape(MOE_ROWS * SLAB_ROWS, SLAB_LANES)


def _combine_kernel(pstart_ref, exp_ref, rank_ref, ys_ref, x_ref, g_ref, o_ref, buf_ref, sem, *, tc):
    def issue(r, carry):
        for k in range(TOP_K):
            a = r * TOP_K + k
            slot = pstart_ref[exp_ref[0, 0, a]] + rank_ref[0, 0, a]
            _slot_copy(ys_ref, slot, buf_ref.at[k], r, sem).start()
        return carry

    def drain(r, carry):
        for k in range(TOP_K):
            _slot_copy(ys_ref, 0, buf_ref.at[k], r, sem).wait()
        return carry

    lax.fori_loop(0, tc, issue, 0, unroll=4)
    lax.fori_loop(0, tc, drain, 0, unroll=4)

    g = g_ref[...]
    for s in range(SLAB_ROWS):
        lo_sum = None
        hi_sum = None
        for k in range(TOP_K):
            lo, hi = _unpack_bf16(buf_ref[k, pl.ds(s, tc, stride=SLAB_ROWS), :])
            gk = g[:, k:k + 1]
            lo_sum = gk * lo if lo_sum is None else lo_sum + gk * lo
            hi_sum = gk * hi if hi_sum is None else hi_sum + gk * hi
        c = (s // DOWN_ROWS) * DOWN_TN + (s % DOWN_ROWS) * SLAB_LANES
        o_ref[:, c:c + SLAB_LANES] = x_ref[:, c:c + SLAB_LANES] + lo_sum
        c += DOWN_WORDS
        o_ref[:, c:c + SLAB_LANES] = x_ref[:, c:c + SLAB_LANES] + hi_sum


def _combine(ys, x2, experts4, rank4, gates4, pstart, tc=128):
    nb = SEQ // tc
    idx_spec = pl.BlockSpec((1, 1, tc * TOP_K), lambda i, ps: (i, 0, 0), memory_space=pltpu.SMEM)
    return pl.pallas_call(
        functools.partial(_combine_kernel, tc=tc),
        grid_spec=pltpu.PrefetchScalarGridSpec(
            num_scalar_prefetch=1,
            grid=(nb,),
            in_specs=[idx_spec, idx_spec,
                      pl.BlockSpec(memory_space=pl.ANY),
                      pl.BlockSpec((tc, D_MODEL), lambda i, ps: (i, 0)),
                      pl.BlockSpec((tc, TOP_K), lambda i, ps: (i, 0))],
            out_specs=pl.BlockSpec((tc, D_MODEL), lambda i, ps: (i, 0)),
            scratch_shapes=[pltpu.VMEM((TOP_K, tc * SLAB_ROWS, SLAB_LANES), jnp.uint32), pltpu.SemaphoreType.DMA(())],
        ),
        out_shape=jax.ShapeDtypeStruct((SEQ, D_MODEL), F32),
        compiler_params=_params(("arbitrary",), 40),
        name="moe_combine",
    )(pstart, experts4.reshape(nb, 1, tc * TOP_K), rank4.reshape(nb, 1, tc * TOP_K), ys, x2, gates4)


def kernel(x, mem, positions, attn_norm_g, w_in, gmlp_ln_g, gmlp_ln_b, gmlp_w_s, gmlp_b_s, diff_q_norm_g, diff_k_norm_g, diff_lambda_q1, diff_lambda_k1, diff_lambda_q2, diff_lambda_k2, diff_subln_g, mem_norm_g, w_mem_kv, mem_q_norm_g, mem_k_norm_g, w_out, ffn_norm_g, w_router, b_router, w_gate_up, b_gate_up, w_down, b_down):
    depth = attn_norm_g.shape[0]
    xs = x.reshape(SEQ, D_MODEL)
    mem2 = mem.reshape(MEM_LEN, D_MODEL)
    for i in range(depth):
        lam_init = 0.8 - 0.6 * math.exp(-0.3 * i)
        h = _rmsnorm_rows(xs, attn_norm_g[i], tm=256)
        z = _matmul([h], w_in[i], None, F32, tm=1024, tn=512, name="in_proj")

        y_g = _gmlp(z, gmlp_ln_g[i], gmlp_ln_b[i], gmlp_w_s[i], gmlp_b_s[i])

        q, k, v = _qkv_prep(z, positions, diff_q_norm_g[i], diff_k_norm_g[i])
        y_d = _diff_attn(q, k, v, diff_lambda_q1[i], diff_lambda_k1[i], diff_lambda_q2[i], diff_lambda_k2[i],
                         diff_subln_g[i], lam_init)

        hm = _rmsnorm_rows(mem2, mem_norm_g[i], tm=256)
        kv = _matmul([hm], w_mem_kv[i], None, F32, tm=MEM_LEN, tn=512, name="mem_kv_proj")
        y_m = _mem_attn(z, kv, mem_q_norm_g[i], mem_k_norm_g[i])

        x2 = _matmul([y_g, y_d, y_m], w_out[i], xs, F32, tm=1024, tn=512, name="out_proj")

        hp, gates4, experts4, rank4, counts = _router(x2, ffn_norm_g[i], w_router[i], b_router[i])

        counts = counts[0]
        padded = (counts + MOE_TILE - 1) // MOE_TILE * MOE_TILE
        pend = jnp.cumsum(padded)
        pstart = pend - padded
        n_used = (pend[-1] // MOE_TILE).astype(jnp.int32).reshape(1)
        blk_start = jnp.arange(MOE_NT, dtype=jnp.int32) * MOE_TILE
        blk_e = jnp.minimum(jnp.sum(pend[None, :] <= blk_start[:, None], axis=1), N_EXPERTS - 1).astype(jnp.int32)
        blk_end = blk_start + MOE_TILE
        fill_tiles = jnp.logical_or(jnp.any(blk_end[:, None] == pend[None, :], axis=1),
                                    blk_start >= pend[-1]).astype(jnp.int32)

        xg = _dispatch(hp, experts4, rank4, pstart, fill_tiles)
        act = _moe_up(xg, blk_e, n_used, w_gate_up[i], b_gate_up[i])
        ys = _moe_down(act, blk_e, n_used, w_down[i], b_down[i])
        xs = _combine(ys, x2, experts4, rank4, gates4, pstart)
    return xs.reshape(x.shape)
```

```python
import functools
import math

import jax
import jax.numpy as jnp
from jax import lax
from jax.experimental import pallas as pl
from jax.experimental.pallas import tpu as pltpu

F32 = jnp.float32
BF16 = jnp.bfloat16

D_MODEL = 4096
SEQ = 8192
HEAD_DIM = 128
CHUNK = 128
GMLP_WIDTH = 1024
GMLP_HEADS = 8
DIFF_HEADS = 8
DIFF_V_DIM = 256
DIFF_QK_WIDTH = 2048
DIFF_WIDTH = 2048
MEM_WIDTH = 1024
MEM_HEADS = 4
MEM_HEAD_DIM = 256
MEM_LEN = 256
IN_WIDTH = 9216
N_EXPERTS = 32
TOP_K = 4
D_FF = 1536
SWIGLU_LIMIT = 7.0
SWIGLU_ALPHA = 1.702
ROPE_THETA = 10000.0
EPS = 1e-6

COL_Q = 2 * GMLP_WIDTH
COL_K = COL_Q + DIFF_QK_WIDTH
COL_V = COL_K + DIFF_QK_WIDTH
COL_M = COL_V + DIFF_WIDTH

MOE_TILE = 256
MOE_ROWS = ((SEQ * TOP_K + N_EXPERTS * (MOE_TILE - 1) + MOE_TILE - 1) // MOE_TILE) * MOE_TILE
MOE_NT = MOE_ROWS // MOE_TILE

MIB = 1 << 20


def _params(sem, vmem_mib):
    return pltpu.CompilerParams(dimension_semantics=sem, vmem_limit_bytes=vmem_mib * MIB)


def _rms(x, g):
    ms = jnp.mean(x * x, axis=-1, keepdims=True)
    return x * lax.rsqrt(ms + EPS) * g


SLAB_LANES = 128
SLAB_ROWS = D_MODEL // 2 // SLAB_LANES


def _unpack_bf16(words):
    lo = pltpu.unpack_elementwise(words, index=0, packed_dtype=BF16, unpacked_dtype=F32)
    hi = pltpu.unpack_elementwise(words, index=1, packed_dtype=BF16, unpacked_dtype=F32)
    return lo, hi


def _store_slabs(o_ref, h, rows):
    half = D_MODEL // 2
    for s in range(SLAB_ROWS):
        lo = h[:, s * SLAB_LANES:(s + 1) * SLAB_LANES]
        hi = h[:, half + s * SLAB_LANES:half + (s + 1) * SLAB_LANES]
        o_ref[pl.ds(s, rows, stride=SLAB_ROWS), :] = pltpu.pack_elementwise([lo, hi], packed_dtype=BF16)


def _load_slabs(x_ref, rows):
    los, his = [], []
    for s in range(SLAB_ROWS):
        lo, hi = _unpack_bf16(x_ref[pl.ds(s, rows, stride=SLAB_ROWS), :])
        los.append(lo.astype(BF16))
        his.append(hi.astype(BF16))
    return jnp.concatenate(los + his, axis=1)


def _rmsnorm_kernel(x_ref, g_ref, o_ref):
    o_ref[...] = _rms(x_ref[...], g_ref[...]).astype(o_ref.dtype)


def _rmsnorm_rows(x, g, tm):
    m, d = x.shape
    return pl.pallas_call(
        _rmsnorm_kernel,
        grid=(m // tm,),
        in_specs=[pl.BlockSpec((tm, d), lambda i: (i, 0)), pl.BlockSpec((1, d), lambda i: (0, 0))],
        out_specs=pl.BlockSpec((tm, d), lambda i: (i, 0)),
        out_shape=jax.ShapeDtypeStruct((m, d), BF16),
        compiler_params=_params(("arbitrary",), 40),
        name="rmsnorm_rows",
    )(x, g.reshape(1, d))


def _matmul_kernel(*refs, widths, has_res):
    n = len(widths)
    lhs = refs[:n]
    w_ref = refs[n]
    res_ref = refs[n + 1] if has_res else None
    o_ref = refs[n + 1 + has_res]

    acc = None
    off = 0
    for r, kw in zip(lhs, widths):
        part = jnp.dot(r[...], w_ref[off:off + kw, :].astype(BF16), preferred_element_type=F32)
        acc = part if acc is None else acc + part
        off += kw
    if has_res:
        acc = acc + res_ref[...]
    o_ref[...] = acc.astype(o_ref.dtype)


def _matmul(lhs_list, w, res, out_dtype, tm, tn, name):
    m = lhs_list[0].shape[0]
    k, n = w.shape
    widths = tuple(a.shape[1] for a in lhs_list)
    assert sum(widths) == k
    in_specs = [pl.BlockSpec((tm, kw), lambda j, i: (i, 0)) for kw in widths]
    in_specs.append(pl.BlockSpec((k, tn), lambda j, i: (0, j)))
    args = list(lhs_list) + [w]
    if res is not None:
        in_specs.append(pl.BlockSpec((tm, tn), lambda j, i: (i, j)))
        args.append(res)
    return pl.pallas_call(
        functools.partial(_matmul_kernel, widths=widths, has_res=res is not None),
        grid=(n // tn, m // tm),
        in_specs=in_specs,
        out_specs=pl.BlockSpec((tm, tn), lambda j, i: (i, j)),
        out_shape=jax.ShapeDtypeStruct((m, n), out_dtype),
        compiler_params=_params(("arbitrary", "arbitrary"), 56),
        name=name,
    )(*args)


def _gmlp_kernel(zu_ref, zv_ref, lng_ref, lnb_ref, ws_ref, bst_ref, o_ref, *, tm):
    inv_sqrt2 = 1.0 / math.sqrt(2.0)

    def gelu(x):
        return 0.5 * x * (1.0 + lax.erf(x * inv_sqrt2))

    u = gelu(zu_ref[...])
    v = gelu(zv_ref[...])
    mu = jnp.mean(v, axis=-1, keepdims=True)
    vc = v - mu
    var = jnp.mean(vc * vc, axis=-1, keepdims=True)
    vn = (vc * lax.rsqrt(var + EPS) * lng_ref[...] + lnb_ref[...]).astype(BF16)
    row = lax.broadcasted_iota(jnp.int32, (CHUNK, CHUNK), 0)
    col = lax.broadcasted_iota(jnp.int32, (CHUNK, CHUNK), 1)
    causal = col <= row
    bst = bst_ref[...]
    for h in range(GMLP_HEADS):
        w = jnp.where(causal, ws_ref[h], 0.0).astype(BF16)
        b = bst[:, h:h + 1]
        for c in range(tm // CHUNK):
            vs = vn[c * CHUNK:(c + 1) * CHUNK, h * HEAD_DIM:(h + 1) * HEAD_DIM]
            mixed = jnp.dot(w, vs, preferred_element_type=F32) + b
            o_ref[c * CHUNK:(c + 1) * CHUNK, h * HEAD_DIM:(h + 1) * HEAD_DIM] = (
                u[c * CHUNK:(c + 1) * CHUNK, h * HEAD_DIM:(h + 1) * HEAD_DIM] * mixed).astype(o_ref.dtype)


def _gmlp(z, ln_g, ln_b, w_s, b_s, tm=256):
    return pl.pallas_call(
        functools.partial(_gmlp_kernel, tm=tm),
        grid=(SEQ // tm,),
        in_specs=[
            pl.BlockSpec((tm, GMLP_WIDTH), lambda i: (i, 0)),
            pl.BlockSpec((tm, GMLP_WIDTH), lambda i: (i, 1)),
            pl.BlockSpec((1, GMLP_WIDTH), lambda i: (0, 0)),
            pl.BlockSpec((1, GMLP_WIDTH), lambda i: (0, 0)),
            pl.BlockSpec((GMLP_HEADS, CHUNK, CHUNK), lambda i: (0, 0, 0)),
            pl.BlockSpec((CHUNK, GMLP_HEADS), lambda i: (0, 0)),
        ],
        out_specs=pl.BlockSpec((tm, GMLP_WIDTH), lambda i: (i, 0)),
        out_shape=jax.ShapeDtypeStruct((SEQ, GMLP_WIDTH), BF16),
        compiler_params=_params(("arbitrary",), 40),
        name="gmlp",
    )(z, z, ln_g.reshape(1, -1), ln_b.reshape(1, -1), w_s, b_s.T)


def _qkv_prep_kernel(pos_ref, freq_ref, sign_ref, gq_ref, gk_ref, zq_ref, zk_ref, zv_ref, q_ref, k_ref, v_ref):
    ang = pos_ref[...].astype(F32) * freq_ref[...]
    cos = jnp.cos(ang)
    sin = jnp.sin(ang) * sign_ref[...]
    scale = HEAD_DIM ** -0.5 * math.log2(math.e)

    def prep(z_ref, g_ref, o_ref, mult):
        for s in range(DIFF_QK_WIDTH // HEAD_DIM):
            seg = _rms(z_ref[:, s * HEAD_DIM:(s + 1) * HEAD_DIM], g_ref[...])
            rot = pltpu.roll(seg, HEAD_DIM // 2, 1)
            out = seg * cos + rot * sin
            if mult != 1.0:
                out = out * mult
            o_ref[:, s * HEAD_DIM:(s + 1) * HEAD_DIM] = out.astype(o_ref.dtype)

    prep(zq_ref, gq_ref, q_ref, scale)
    prep(zk_ref, gk_ref, k_ref, 1.0)
    v_ref[...] = zv_ref[...].astype(v_ref.dtype)


def _qkv_prep(z, positions, gq, gk, tm=256):
    half = jnp.arange(0, HEAD_DIM, 2, dtype=F32) / HEAD_DIM
    inv_freq = 1.0 / (ROPE_THETA ** half)
    freq = jnp.concatenate([inv_freq, inv_freq]).reshape(1, HEAD_DIM)
    sign = jnp.concatenate([-jnp.ones((HEAD_DIM // 2,), F32), jnp.ones((HEAD_DIM // 2,), F32)]).reshape(1, HEAD_DIM)
    wide = DIFF_QK_WIDTH
    row_spec = lambda cb: pl.BlockSpec((tm, wide), lambda i: (i, cb))
    vec = pl.BlockSpec((1, HEAD_DIM), lambda i: (0, 0))
    out_spec = pl.BlockSpec((tm, wide), lambda i: (i, 0))
    shp = jax.ShapeDtypeStruct((SEQ, wide), BF16)
    return pl.pallas_call(
        _qkv_prep_kernel,
        grid=(SEQ // tm,),
        in_specs=[pl.BlockSpec((tm, 1), lambda i: (i, 0)), vec, vec, vec, vec,
                  row_spec(COL_Q // wide), row_spec(COL_K // wide), row_spec(COL_V // wide)],
        out_specs=[out_spec, out_spec, out_spec],
        out_shape=[shp, shp, shp],
        compiler_params=_params(("arbitrary",), 40),
        name="qkv_prep",
    )(positions.reshape(SEQ, 1), freq, sign, gq.reshape(1, -1), gk.reshape(1, -1), z, z, z)


KV_UNROLL = 4
def _diff_attn_kernel(lq1_ref, lk1_ref, lq2_ref, lk2_ref, sg_ref, q_ref, k_ref, v_ref, o_ref,
                      m_ref, l_ref, acc_ref, *, tq, lam_init):
    i = pl.program_id(1)
    m_ref[...] = jnp.full(m_ref.shape, -jnp.inf, F32)
    l_ref[...] = jnp.zeros(l_ref.shape, F32)
    acc_ref[...] = jnp.zeros(acc_ref.shape, F32)
    lanes = tq // 128

    def block(j, masked):
        start = pl.multiple_of(j * tq, tq)
        kb = k_ref[pl.ds(start, tq), :]
        vb = v_ref[pl.ds(start, tq), :]
        for a in range(2):
            q = q_ref[:, a * HEAD_DIM:(a + 1) * HEAD_DIM]
            k = kb[:, a * HEAD_DIM:(a + 1) * HEAD_DIM]
            s = lax.dot_general(q, k, (((1,), (1,)), ((), ())), preferred_element_type=F32)
            if masked:
                row = lax.broadcasted_iota(jnp.int32, (tq, tq), 0)
                col = lax.broadcasted_iota(jnp.int32, (tq, tq), 1)
                s = jnp.where(col <= row, s, -jnp.inf)
            m_prev = m_ref[a]
            m_new = jnp.maximum(m_prev, jnp.max(s, axis=-1, keepdims=True))
            alpha = jnp.exp2(m_prev - m_new)
            p = jnp.exp2(s - jnp.concatenate([m_new] * lanes, axis=1))
            psum = p[:, 0:128]
            for c in range(1, lanes):
                psum = psum + p[:, c * 128:(c + 1) * 128]
            l_ref[a] = alpha * l_ref[a] + psum
            acc_ref[a] = (jnp.concatenate([alpha] * (DIFF_V_DIM // 128), axis=1) * acc_ref[a]
                          + jnp.dot(p.astype(BF16), vb, preferred_element_type=F32))
            m_ref[a] = m_new

    def group(g, carry):
        for u in range(KV_UNROLL):
            block(KV_UNROLL * g + u, False)
        return carry

    def single(j, carry):
        block(j, False)
        return carry

    groups = i // KV_UNROLL
    lax.fori_loop(0, groups, group, 0)
    lax.fori_loop(groups * KV_UNROLL, i, single, 0)
    block(i, True)

    lam = (jnp.exp(jnp.sum(lq1_ref[...] * lk1_ref[...], axis=-1, keepdims=True))
           - jnp.exp(jnp.sum(lq2_ref[...] * lk2_ref[...], axis=-1, keepdims=True)) + lam_init)
    l1 = jnp.sum(l_ref[0], axis=-1, keepdims=True)
    l2 = jnp.sum(l_ref[1], axis=-1, keepdims=True)
    o = acc_ref[0] / l1 - lam * (acc_ref[1] / l2)
    o_ref[...] = (_rms(o, sg_ref[...]) * (1.0 - lam_init)).astype(o_ref.dtype)


def _diff_attn(q, k, v, lq1, lk1, lq2, lk2, subln_g, lam_init, tq=512):
    vec = pl.BlockSpec((1, HEAD_DIM), lambda h, i: (0, 0))
    kv_spec = pl.BlockSpec((SEQ, DIFF_V_DIM), lambda h, i: (0, h))
    return pl.pallas_call(
        functools.partial(_diff_attn_kernel, tq=tq, lam_init=lam_init),
        grid=(DIFF_HEADS, SEQ // tq),
        in_specs=[vec, vec, vec, vec, pl.BlockSpec((1, DIFF_V_DIM), lambda h, i: (0, 0)),
                  pl.BlockSpec((tq, DIFF_V_DIM), lambda h, i: (i, h)), kv_spec, kv_spec],
        out_specs=pl.BlockSpec((tq, DIFF_V_DIM), lambda h, i: (i, h)),
        out_shape=jax.ShapeDtypeStruct((SEQ, DIFF_WIDTH), BF16),
        scratch_shapes=[pltpu.VMEM((2, tq, 128), F32), pltpu.VMEM((2, tq, 128), F32),
                        pltpu.VMEM((2, tq, DIFF_V_DIM), F32)],
        compiler_params=_params(("arbitrary", "arbitrary"), 48),
        name="diff_attn",
    )(lq1.reshape(1, -1), lk1.reshape(1, -1), lq2.reshape(1, -1), lk2.reshape(1, -1),
      subln_g.reshape(1, -1), q, k, v)


def _mem_attn_kernel(zm_ref, kv_ref, gq_ref, gk_ref, o_ref):
    scale = MEM_HEAD_DIM ** -0.5
    for h in range(MEM_HEADS):
        lo, hi = h * MEM_HEAD_DIM, (h + 1) * MEM_HEAD_DIM
        q = (_rms(zm_ref[:, lo:hi], gq_ref[...]) * scale).astype(BF16)
        k = _rms(kv_ref[:, lo:hi], gk_ref[...]).astype(BF16)
        v = kv_ref[:, MEM_WIDTH + lo:MEM_WIDTH + hi].astype(BF16)
        s = lax.dot_general(q, k, (((1,), (1,)), ((), ())), preferred_element_type=F32)
        p = jnp.exp(s - jnp.max(s, axis=-1, keepdims=True))
        l = jnp.sum(p, axis=-1, keepdims=True)
        o = jnp.dot(p.astype(BF16), v, preferred_element_type=F32) / l
        o_ref[:, lo:hi] = o.astype(o_ref.dtype)


def _mem_attn(z, kv, gq, gk, tm=512):
    vec = pl.BlockSpec((1, MEM_HEAD_DIM), lambda i: (0, 0))
    return pl.pallas_call(
        _mem_attn_kernel,
        grid=(SEQ // tm,),
        in_specs=[pl.BlockSpec((tm, MEM_WIDTH), lambda i: (i, COL_M // MEM_WIDTH)),
                  pl.BlockSpec((MEM_LEN, 2 * MEM_WIDTH), lambda i: (0, 0)), vec, vec],
        out_specs=pl.BlockSpec((tm, MEM_WIDTH), lambda i: (i, 0)),
        out_shape=jax.ShapeDtypeStruct((SEQ, MEM_WIDTH), BF16),
        compiler_params=_params(("arbitrary",), 40),
        name="mem_attn",
    )(z, kv, gq.reshape(1, -1), gk.reshape(1, -1))


def _router_kernel(x_ref, g_ref, wr_ref, br_ref, h_ref, gate_ref, exp_ref, rank_ref, cnt_ref, carry_ref, *, tm):
    @pl.when(pl.program_id(0) == 0)
    def _():
        carry_ref[...] = jnp.zeros(carry_ref.shape, F32)

    h = _rms(x_ref[...], g_ref[...])
    _store_slabs(h_ref, h, tm)
    logits = jnp.dot(h, wr_ref[...], precision=lax.Precision.HIGHEST, preferred_element_type=F32) + br_ref[...]
    lane = lax.broadcasted_iota(jnp.int32, logits.shape, 1)
    slot = lax.broadcasted_iota(jnp.int32, (tm, TOP_K), 1)
    work = logits
    sel = jnp.zeros(logits.shape, jnp.bool_)
    hits = []
    top = None
    nums = jnp.zeros((tm, TOP_K), F32)
    experts = jnp.zeros((tm, TOP_K), jnp.int32)
    denom = jnp.zeros((tm, 1), F32)
    for k in range(TOP_K):
        m = jnp.max(work, axis=-1, keepdims=True)
        idx = jnp.min(jnp.where(work == m, lane, N_EXPERTS), axis=-1, keepdims=True)
        hit = lane == idx
        if k == 0:
            top = m
        e = jnp.exp(m - top)
        nums = jnp.where(slot == k, e, nums)
        experts = jnp.where(slot == k, idx, experts)
        denom = denom + e
        work = jnp.where(hit, -jnp.inf, work)
        sel = jnp.logical_or(sel, hit)
        hits.append(hit)
    gate_ref[...] = nums / denom
    exp_ref[...] = experts
    r = lax.broadcasted_iota(jnp.int32, (tm, tm), 0)
    c = lax.broadcasted_iota(jnp.int32, (tm, tm), 1)
    strict = jnp.where(c < r, 1.0, 0.0).astype(BF16)
    chosen = jnp.where(sel, 1.0, 0.0)
    rank = jnp.dot(strict, chosen.astype(BF16), preferred_element_type=F32) + carry_ref[...]
    ranks = jnp.zeros((tm, TOP_K), F32)
    for k in range(TOP_K):
        ranks = jnp.where(slot == k, jnp.sum(jnp.where(hits[k], rank, 0.0), axis=-1, keepdims=True), ranks)
    rank_ref[...] = ranks.astype(jnp.int32)
    carry_ref[...] = carry_ref[...] + jnp.sum(chosen, axis=0, keepdims=True)
    cnt_ref[...] = carry_ref[...].astype(jnp.int32)


def _router(x, g, w_router, b_router, tm=256):
    small = lambda: pl.BlockSpec((tm, TOP_K), lambda i: (i, 0))
    return pl.pallas_call(
        functools.partial(_router_kernel, tm=tm),
        grid=(SEQ // tm,),
        in_specs=[pl.BlockSpec((tm, D_MODEL), lambda i: (i, 0)), pl.BlockSpec((1, D_MODEL), lambda i: (0, 0)),
                  pl.BlockSpec((D_MODEL, N_EXPERTS), lambda i: (0, 0)), pl.BlockSpec((1, N_EXPERTS), lambda i: (0, 0))],
        out_specs=[pl.BlockSpec((tm * SLAB_ROWS, SLAB_LANES), lambda i: (i, 0)), small(), small(), small(),
                   pl.BlockSpec((1, N_EXPERTS), lambda i: (0, 0))],
        out_shape=[jax.ShapeDtypeStruct((SEQ * SLAB_ROWS, SLAB_LANES), jnp.uint32),
                   jax.ShapeDtypeStruct((SEQ, TOP_K), F32),
                   jax.ShapeDtypeStruct((SEQ, TOP_K), jnp.int32),
                   jax.ShapeDtypeStruct((SEQ, TOP_K), jnp.int32),
                   jax.ShapeDtypeStruct((1, N_EXPERTS), jnp.int32)],
        scratch_shapes=[pltpu.VMEM((1, N_EXPERTS), F32)],
        compiler_params=_params(("arbitrary",), 40),
        name="router",
    )(x, g.reshape(1, -1), w_router, b_router.reshape(1, -1))


def _slot_copy(src_ref, src_row, dst_ref, dst_row, sem):
    return pltpu.make_async_copy(src_ref.at[pl.ds(src_row * SLAB_ROWS, SLAB_ROWS)],
                                 dst_ref.at[pl.ds(dst_row * SLAB_ROWS, SLAB_ROWS)], sem)


def _tile_fill_copy(zero_ref, dst_ref, tile, sem):
    rows = MOE_TILE * SLAB_ROWS
    return pltpu.make_async_copy(zero_ref, dst_ref.at[pl.ds(tile * rows, rows)], sem)


def _dispatch_kernel(pstart_ref, fill_ref, exp_ref, rank_ref, h_ref, o_ref, slot_ref, zero_ref, sem, *, tc):
    @pl.when(pl.program_id(0) == 0)
    def _():
        zero_ref[...] = jnp.zeros(zero_ref.shape, zero_ref.dtype)

        def fill(t, carry):
            @pl.when(fill_ref[t] != 0)
            def _():
                _tile_fill_copy(zero_ref, o_ref, t, sem).start()
            return carry

        def fill_wait(t, carry):
            @pl.when(fill_ref[t] != 0)
            def _():
                _tile_fill_copy(zero_ref, o_ref, t, sem).wait()
            return carry

        lax.fori_loop(0, MOE_NT, fill, 0)
        lax.fori_loop(0, MOE_NT, fill_wait, 0)

    def issue(r, carry):
        for k in range(TOP_K):
            a = r * TOP_K + k
            slot = pstart_ref[exp_ref[0, 0, a]] + rank_ref[0, 0, a]
            slot_ref[0, 0, a] = slot
            _slot_copy(h_ref, r, o_ref, slot, sem).start()
        return carry

    def drain(r, carry):
        for k in range(TOP_K):
            _slot_copy(h_ref, r, o_ref, slot_ref[0, 0, r * TOP_K + k], sem).wait()
        return carry

    lax.fori_loop(0, tc, issue, 0, unroll=4)
    lax.fori_loop(0, tc, drain, 0, unroll=4)


def _dispatch(hp, experts4, rank4, pstart, fill_tiles, tc=128):
    nb = SEQ // tc
    idx_spec = pl.BlockSpec((1, 1, tc * TOP_K), lambda i, ps, ft: (i, 0, 0), memory_space=pltpu.SMEM)
    return pl.pallas_call(
        functools.partial(_dispatch_kernel, tc=tc),
        grid_spec=pltpu.PrefetchScalarGridSpec(
            num_scalar_prefetch=2,
            grid=(nb,),
            in_specs=[idx_spec, idx_spec,
                      pl.BlockSpec((tc * SLAB_ROWS, SLAB_LANES), lambda i, ps, ft: (i, 0))],
            out_specs=[pl.BlockSpec(memory_space=pl.ANY), idx_spec],
            scratch_shapes=[pltpu.VMEM((MOE_TILE * SLAB_ROWS, SLAB_LANES), jnp.uint32), pltpu.SemaphoreType.DMA(())],
        ),
        out_shape=[jax.ShapeDtypeStruct((MOE_ROWS * SLAB_ROWS, SLAB_LANES), jnp.uint32),
                   jax.ShapeDtypeStruct((nb, 1, tc * TOP_K), jnp.int32)],
        compiler_params=_params(("arbitrary",), 40),
        name="moe_dispatch",
    )(pstart, fill_tiles, experts4.reshape(nb, 1, tc * TOP_K), rank4.reshape(nb, 1, tc * TOP_K), hp)


def _group_edges(blk_ref, t):
    e = blk_ref[t]
    first = jnp.logical_or(t == 0, blk_ref[jnp.maximum(t - 1, 0)] != e)
    last = jnp.logical_or(t == MOE_NT - 1, blk_ref[jnp.minimum(t + 1, MOE_NT - 1)] != e)
    return e, first, last


def _stream_weights(blk_ref, nxt_ref, slot_ref, n_pass, copies):
    j = pl.program_id(0)
    t = pl.program_id(1)
    e, first, last = _group_edges(blk_ref, t)

    @pl.when(jnp.logical_and(j == 0, t == 0))
    def _():
        slot_ref[0] = 0
        for c in copies(e, j, 0):
            c.start()

    @pl.when(first)
    def _():
        slot = slot_ref[0]
        for c in copies(e, j, slot):
            c.wait()
        nt = nxt_ref[t]
        same_pass = nt < MOE_NT
        ne = blk_ref[jnp.where(same_pass, nt, 0)]
        nj = jnp.where(same_pass, j, j + 1)

        @pl.when(jnp.logical_or(same_pass, j + 1 < n_pass))
        def _():
            for c in copies(ne, nj, 1 - slot):
                c.start()

    slot = slot_ref[0]

    @pl.when(last)
    def _():
        slot_ref[0] = 1 - slot

    return slot


def _moe_up_kernel(blk_ref, nxt_ref, nused_ref, x_ref, w_hbm, bg_ref, bu_ref, o_ref, wbuf, sem, slot_ref, *, tn):
    t = pl.program_id(1)

    def copies(e, j, slot):
        return [pltpu.make_async_copy(w_hbm.at[e, :, pl.ds(pl.multiple_of(part * D_FF + j * tn, tn), tn)],
                                      wbuf.at[slot, part], sem.at[slot, part]) for part in range(2)]

    slot = _stream_weights(blk_ref, nxt_ref, slot_ref, D_FF // tn, copies)

    @pl.when(t < nused_ref[0])
    def _():
        x = _load_slabs(x_ref, MOE_TILE)

        def proj(part, b_ref):
            return jnp.dot(x, wbuf[slot, part].astype(BF16), preferred_element_type=F32) + b_ref[0]

        gate = jnp.minimum(proj(0, bg_ref), SWIGLU_LIMIT)
        up = jnp.clip(proj(1, bu_ref), -SWIGLU_LIMIT, SWIGLU_LIMIT)
        act = (up + 1.0) * (gate * jax.nn.sigmoid(SWIGLU_ALPHA * gate))
        o_ref[...] = act.astype(o_ref.dtype)

    @pl.when(t >= nused_ref[0])
    def _():
        o_ref[...] = jnp.zeros(o_ref.shape, o_ref.dtype)


def _moe_up(xs, blk_e, nxt, n_used, w_gate_up, b_gate_up, tn=512):
    nj = D_FF // tn
    last = lambda t, nu: jnp.minimum(t, nu[0] - 1)
    b3 = b_gate_up.reshape(N_EXPERTS, 1, 2 * D_FF)
    return pl.pallas_call(
        functools.partial(_moe_up_kernel, tn=tn),
        grid_spec=pltpu.PrefetchScalarGridSpec(
            num_scalar_prefetch=3,
            grid=(nj, MOE_NT),
            in_specs=[
                pl.BlockSpec((MOE_TILE * SLAB_ROWS, SLAB_LANES), lambda j, t, be, nx, nu: (last(t, nu), 0)),
                pl.BlockSpec(memory_space=pl.ANY),
                pl.BlockSpec((1, 1, tn), lambda j, t, be, nx, nu: (be[t], 0, j)),
                pl.BlockSpec((1, 1, tn), lambda j, t, be, nx, nu: (be[t], 0, nj + j)),
            ],
            out_specs=pl.BlockSpec((MOE_TILE, tn), lambda j, t, be, nx, nu: (t, j)),
            scratch_shapes=[pltpu.VMEM((2, 2, D_MODEL, tn), F32), pltpu.SemaphoreType.DMA((2, 2)),
                            pltpu.SMEM((1,), jnp.int32)],
        ),
        out_shape=jax.ShapeDtypeStruct((MOE_ROWS, D_FF), BF16),
        compiler_params=_params(("arbitrary", "arbitrary"), 56),
        name="moe_up",
    )(blk_e, nxt, n_used, xs, w_gate_up, b3, b3)


DOWN_TN = D_MODEL // 2
DOWN_WORDS = DOWN_TN // 2
DOWN_ROWS = DOWN_WORDS // SLAB_LANES


def _moe_down_kernel(blk_ref, nxt_ref, nused_ref, a_ref, w_hbm, b_ref, o_ref, wbuf, sem, slot_ref):
    t = pl.program_id(1)

    def copies(e, j, slot):
        return [pltpu.make_async_copy(w_hbm.at[e, :, pl.ds(pl.multiple_of(j * DOWN_TN, DOWN_TN), DOWN_TN)],
                                      wbuf.at[slot], sem.at[slot])]

    slot = _stream_weights(blk_ref, nxt_ref, slot_ref, D_MODEL // DOWN_TN, copies)

    @pl.when(t < nused_ref[0])
    def _():
        y = jnp.dot(a_ref[...], wbuf[slot].astype(BF16), preferred_element_type=F32) + b_ref[0]
        words = pltpu.pack_elementwise([y[:, :DOWN_WORDS], y[:, DOWN_WORDS:]], packed_dtype=BF16)
        for s in range(DOWN_ROWS):
            o_ref[:, s, :] = words[:, s * SLAB_LANES:(s + 1) * SLAB_LANES]

    @pl.when(t >= nused_ref[0])
    def _():
        o_ref[...] = jnp.zeros(o_ref.shape, o_ref.dtype)


def _moe_down(act, blk_e, nxt, n_used, w_down, b_down):
    nj = D_MODEL // DOWN_TN
    last = lambda t, nu: jnp.minimum(t, nu[0] - 1)
    return pl.pallas_call(
        _moe_down_kernel,
        grid_spec=pltpu.PrefetchScalarGridSpec(
            num_scalar_prefetch=3,
            grid=(nj, MOE_NT),
            in_specs=[
                pl.BlockSpec((MOE_TILE, D_FF), lambda j, t, be, nx, nu: (last(t, nu), 0)),
                pl.BlockSpec(memory_space=pl.ANY),
                pl.BlockSpec((1, 1, DOWN_TN), lambda j, t, be, nx, nu: (be[t], 0, j)),
            ],
            out_specs=pl.BlockSpec((MOE_TILE, DOWN_ROWS, SLAB_LANES), lambda j, t, be, nx, nu: (t, j, 0)),
            scratch_shapes=[pltpu.VMEM((2, D_FF, DOWN_TN), F32), pltpu.SemaphoreType.DMA((2,)),
                            pltpu.SMEM((1,), jnp.int32)],
        ),
        out_shape=jax.ShapeDtypeStruct((MOE_ROWS, SLAB_ROWS, SLAB_LANES), jnp.uint32),
        compiler_params=_params(("arbitrary", "arbitrary"), 56),
        name="moe_down",
    )(blk_e, nxt, n_used, act, w_down, b_down.reshape(N_EXPERTS, 1, D_MODEL)).reshape(MOE_ROWS * SLAB_ROWS, SLAB_LANES)


COMBINE_ROWS = 32


def _combine_kernel(cur_ref, nxt_ref, ys_ref, x_ref, g_ref, o_ref, buf_a, buf_b, sem, *, tc):
    i = pl.program_id(0)

    def fetch(idx_ref, a, buf, slot):
        r, k = divmod(a, TOP_K)
        return _slot_copy(ys_ref, idx_ref[0, 0, a], buf.at[k], r, sem.at[slot])

    def wait_tile(buf, slot):
        for k in range(TOP_K):
            pltpu.make_async_copy(ys_ref.at[pl.ds(0, tc * SLAB_ROWS)], buf.at[k], sem.at[slot]).wait()

    @pl.when(i == 0)
    def _():
        def issue(r, carry):
            for k in range(TOP_K):
                _slot_copy(ys_ref, cur_ref[0, 0, r * TOP_K + k], buf_a.at[k], r, sem.at[0]).start()
            return carry

        lax.fori_loop(0, tc, issue, 0, unroll=4)

    def step(cur_buf, cur_slot, nxt_buf, nxt_slot):
        wait_tile(cur_buf, cur_slot)
        n_chunks = (tc // COMBINE_ROWS) * SLAB_ROWS
        per_chunk = tc * TOP_K // n_chunks
        a = 0
        for t0 in range(0, tc, COMBINE_ROWS):
            rows = slice(t0, t0 + COMBINE_ROWS)
            gates = [jnp.broadcast_to(g_ref[rows, k:k + 1], (COMBINE_ROWS, SLAB_LANES)) for k in range(TOP_K)]
            for s in range(SLAB_ROWS):
                for _ in range(per_chunk):
                    fetch(nxt_ref, a, nxt_buf, nxt_slot).start()
                    a += 1
                lo_sum = None
                hi_sum = None
                for k in range(TOP_K):
                    words = cur_buf[k, pl.ds(t0 * SLAB_ROWS + s, COMBINE_ROWS, stride=SLAB_ROWS), :]
                    lo, hi = _unpack_bf16(words)
                    lo_sum = gates[k] * lo if lo_sum is None else lo_sum + gates[k] * lo
                    hi_sum = gates[k] * hi if hi_sum is None else hi_sum + gates[k] * hi
                c = (s // DOWN_ROWS) * DOWN_TN + (s % DOWN_ROWS) * SLAB_LANES
                o_ref[rows, c:c + SLAB_LANES] = x_ref[rows, c:c + SLAB_LANES] + lo_sum
                c += DOWN_WORDS
                o_ref[rows, c:c + SLAB_LANES] = x_ref[rows, c:c + SLAB_LANES] + hi_sum

        @pl.when(i == pl.num_programs(0) - 1)
        def _():
            wait_tile(nxt_buf, nxt_slot)

    @pl.when(i % 2 == 0)
    def _():
        step(buf_a, 0, buf_b, 1)

    @pl.when(i % 2 == 1)
    def _():
        step(buf_b, 1, buf_a, 0)


def _combine(ys, x2, slots, gates4, tc=128):
    nb = SEQ // tc
    idx_spec = lambda f: pl.BlockSpec((1, 1, tc * TOP_K), lambda i: (f(i), 0, 0), memory_space=pltpu.SMEM)
    return pl.pallas_call(
        functools.partial(_combine_kernel, tc=tc),
        grid=(nb,),
        in_specs=[idx_spec(lambda i: i), idx_spec(lambda i: jnp.minimum(i + 1, nb - 1)),
                  pl.BlockSpec(memory_space=pl.ANY),
                  pl.BlockSpec((tc, D_MODEL), lambda i: (i, 0)),
                  pl.BlockSpec((tc, TOP_K), lambda i: (i, 0))],
        out_specs=pl.BlockSpec((tc, D_MODEL), lambda i: (i, 0)),
        out_shape=jax.ShapeDtypeStruct((SEQ, D_MODEL), F32),
        scratch_shapes=[pltpu.VMEM((TOP_K, tc * SLAB_ROWS, SLAB_LANES), jnp.uint32),
                        pltpu.VMEM((TOP_K, tc * SLAB_ROWS, SLAB_LANES), jnp.uint32), pltpu.SemaphoreType.DMA((2,))],
        compiler_params=_params(("arbitrary",), 40),
        name="moe_combine",
    )(slots, slots, ys, x2, gates4)


def kernel(x, mem, positions, attn_norm_g, w_in, gmlp_ln_g, gmlp_ln_b, gmlp_w_s, gmlp_b_s, diff_q_norm_g, diff_k_norm_g, diff_lambda_q1, diff_lambda_k1, diff_lambda_q2, diff_lambda_k2, diff_subln_g, mem_norm_g, w_mem_kv, mem_q_norm_g, mem_k_norm_g, w_out, ffn_norm_g, w_router, b_router, w_gate_up, b_gate_up, w_down, b_down):
    depth = attn_norm_g.shape[0]
    xs = x.reshape(SEQ, D_MODEL)
    mem2 = mem.reshape(MEM_LEN, D_MODEL)
    for i in range(depth):
        lam_init = 0.8 - 0.6 * math.exp(-0.3 * i)
        h = _rmsnorm_rows(xs, attn_norm_g[i], tm=256)
        z = _matmul([h], w_in[i], None, F32, tm=1024, tn=512, name="in_proj")

        y_g = _gmlp(z, gmlp_ln_g[i], gmlp_ln_b[i], gmlp_w_s[i], gmlp_b_s[i])

        q, k, v = _qkv_prep(z, positions, diff_q_norm_g[i], diff_k_norm_g[i])
        y_d = _diff_attn(q, k, v, diff_lambda_q1[i], diff_lambda_k1[i], diff_lambda_q2[i], diff_lambda_k2[i],
                         diff_subln_g[i], lam_init)

        hm = _rmsnorm_rows(mem2, mem_norm_g[i], tm=256)
        kv = _matmul([hm], w_mem_kv[i], None, F32, tm=MEM_LEN, tn=512, name="mem_kv_proj")
        y_m = _mem_attn(z, kv, mem_q_norm_g[i], mem_k_norm_g[i])

        x2 = _matmul([y_g, y_d, y_m], w_out[i], xs, F32, tm=1024, tn=512, name="out_proj")

        hp, gates4, experts4, rank4, counts = _router(x2, ffn_norm_g[i], w_router[i], b_router[i])

        counts = counts[0]
        padded = (counts + MOE_TILE - 1) // MOE_TILE * MOE_TILE
        pend = jnp.cumsum(padded)
        pstart = pend - padded
        n_used = (pend[-1] // MOE_TILE).astype(jnp.int32).reshape(1)
        blk_start = jnp.arange(MOE_NT, dtype=jnp.int32) * MOE_TILE
        blk_e = jnp.minimum(jnp.sum(pend[None, :] <= blk_start[:, None], axis=1), N_EXPERTS - 1).astype(jnp.int32)
        blk_end = blk_start + MOE_TILE
        fill_tiles = jnp.logical_or(jnp.any(blk_end[:, None] == pend[None, :], axis=1),
                                    blk_start >= pend[-1]).astype(jnp.int32)

        tile_ids = jnp.arange(MOE_NT, dtype=jnp.int32)
        run_start = jnp.concatenate([jnp.ones((1,), jnp.bool_), blk_e[1:] != blk_e[:-1]])
        starts_from = lax.cummin(jnp.where(run_start, tile_ids, MOE_NT), axis=0, reverse=True)
        nxt = jnp.concatenate([starts_from[1:], jnp.full((1,), MOE_NT, jnp.int32)]).astype(jnp.int32)

        xg, slots = _dispatch(hp, experts4, rank4, pstart, fill_tiles)
        act = _moe_up(xg, blk_e, nxt, n_used, w_gate_up[i], b_gate_up[i])
        ys = _moe_down(act, blk_e, nxt, n_used, w_down[i], b_down[i])
        xs = _combine(ys, x2, slots, gates4)
    return xs.reshape(x.shape)
```

```python
import functools
import math

import jax
import jax.numpy as jnp
from jax import lax
from jax.experimental import pallas as pl
from jax.experimental.pallas import tpu as pltpu

F32 = jnp.float32
BF16 = jnp.bfloat16

D_MODEL = 4096
SEQ = 8192
HEAD_DIM = 128
CHUNK = 128
GMLP_WIDTH = 1024
GMLP_HEADS = 8
DIFF_HEADS = 8
DIFF_V_DIM = 256
DIFF_QK_WIDTH = 2048
DIFF_WIDTH = 2048
MEM_WIDTH = 1024
MEM_HEADS = 4
MEM_HEAD_DIM = 256
MEM_LEN = 256
IN_WIDTH = 9216
N_EXPERTS = 32
TOP_K = 4
D_FF = 1536
SWIGLU_LIMIT = 7.0
SWIGLU_ALPHA = 1.702
ROPE_THETA = 10000.0
EPS = 1e-6

COL_Q = 2 * GMLP_WIDTH
COL_K = COL_Q + DIFF_QK_WIDTH
COL_V = COL_K + DIFF_QK_WIDTH
COL_M = COL_V + DIFF_WIDTH

MOE_TILE = 256
MOE_ROWS = ((SEQ * TOP_K + N_EXPERTS * (MOE_TILE - 1) + MOE_TILE - 1) // MOE_TILE) * MOE_TILE
MOE_NT = MOE_ROWS // MOE_TILE

MIB = 1 << 20


def _params(sem, vmem_mib):
    return pltpu.CompilerParams(dimension_semantics=sem, vmem_limit_bytes=vmem_mib * MIB)


def _rms(x, g):
    ms = jnp.mean(x * x, axis=-1, keepdims=True)
    return x * lax.rsqrt(ms + EPS) * g


SLAB_LANES = 128
SLAB_ROWS = D_MODEL // 2 // SLAB_LANES


def _unpack_bf16(words):
    lo = pltpu.unpack_elementwise(words, index=0, packed_dtype=BF16, unpacked_dtype=F32)
    hi = pltpu.unpack_elementwise(words, index=1, packed_dtype=BF16, unpacked_dtype=F32)
    return lo, hi


def _store_slabs(o_ref, h, rows):
    half = D_MODEL // 2
    for s in range(SLAB_ROWS):
        lo = h[:, s * SLAB_LANES:(s + 1) * SLAB_LANES]
        hi = h[:, half + s * SLAB_LANES:half + (s + 1) * SLAB_LANES]
        o_ref[pl.ds(s, rows, stride=SLAB_ROWS), :] = pltpu.pack_elementwise([lo, hi], packed_dtype=BF16)


def _load_slabs(x_ref, rows):
    los, his = [], []
    for s in range(SLAB_ROWS):
        lo, hi = _unpack_bf16(x_ref[pl.ds(s, rows, stride=SLAB_ROWS), :])
        los.append(lo.astype(BF16))
        his.append(hi.astype(BF16))
    return jnp.concatenate(los + his, axis=1)


def _rmsnorm_kernel(x_ref, g_ref, o_ref):
    o_ref[...] = _rms(x_ref[...], g_ref[...]).astype(o_ref.dtype)


def _rmsnorm_rows(x, g, tm):
    m, d = x.shape
    return pl.pallas_call(
        _rmsnorm_kernel,
        grid=(m // tm,),
        in_specs=[pl.BlockSpec((tm, d), lambda i: (i, 0)), pl.BlockSpec((1, d), lambda i: (0, 0))],
        out_specs=pl.BlockSpec((tm, d), lambda i: (i, 0)),
        out_shape=jax.ShapeDtypeStruct((m, d), BF16),
        compiler_params=_params(("arbitrary",), 40),
        name="rmsnorm_rows",
    )(x, g.reshape(1, d))


def _matmul_kernel(*refs, widths, has_res):
    n = len(widths)
    lhs = refs[:n]
    w_ref = refs[n]
    res_ref = refs[n + 1] if has_res else None
    o_ref = refs[n + 1 + has_res]

    acc = None
    off = 0
    for r, kw in zip(lhs, widths):
        part = jnp.dot(r[...], w_ref[off:off + kw, :].astype(BF16), preferred_element_type=F32)
        acc = part if acc is None else acc + part
        off += kw
    if has_res:
        acc = acc + res_ref[...]
    o_ref[...] = acc.astype(o_ref.dtype)


def _matmul(lhs_list, w, res, out_dtype, tm, tn, name):
    m = lhs_list[0].shape[0]
    k, n = w.shape
    widths = tuple(a.shape[1] for a in lhs_list)
    assert sum(widths) == k
    in_specs = [pl.BlockSpec((tm, kw), lambda j, i: (i, 0)) for kw in widths]
    in_specs.append(pl.BlockSpec((k, tn), lambda j, i: (0, j)))
    args = list(lhs_list) + [w]
    if res is not None:
        in_specs.append(pl.BlockSpec((tm, tn), lambda j, i: (i, j)))
        args.append(res)
    return pl.pallas_call(
        functools.partial(_matmul_kernel, widths=widths, has_res=res is not None),
        grid=(n // tn, m // tm),
        in_specs=in_specs,
        out_specs=pl.BlockSpec((tm, tn), lambda j, i: (i, j)),
        out_shape=jax.ShapeDtypeStruct((m, n), out_dtype),
        compiler_params=_params(("arbitrary", "arbitrary"), 56),
        name=name,
    )(*args)


def _gmlp_kernel(zu_ref, zv_ref, lng_ref, lnb_ref, ws_ref, bst_ref, o_ref, *, tm):
    inv_sqrt2 = 1.0 / math.sqrt(2.0)

    def gelu(x):
        return 0.5 * x * (1.0 + lax.erf(x * inv_sqrt2))

    u = gelu(zu_ref[...])
    v = gelu(zv_ref[...])
    mu = jnp.mean(v, axis=-1, keepdims=True)
    vc = v - mu
    var = jnp.mean(vc * vc, axis=-1, keepdims=True)
    vn = (vc * lax.rsqrt(var + EPS) * lng_ref[...] + lnb_ref[...]).astype(BF16)
    row = lax.broadcasted_iota(jnp.int32, (CHUNK, CHUNK), 0)
    col = lax.broadcasted_iota(jnp.int32, (CHUNK, CHUNK), 1)
    causal = col <= row
    bst = bst_ref[...]
    for h in range(GMLP_HEADS):
        w = jnp.where(causal, ws_ref[h], 0.0).astype(BF16)
        b = bst[:, h:h + 1]
        for c in range(tm // CHUNK):
            vs = vn[c * CHUNK:(c + 1) * CHUNK, h * HEAD_DIM:(h + 1) * HEAD_DIM]
            mixed = jnp.dot(w, vs, preferred_element_type=F32) + b
            o_ref[c * CHUNK:(c + 1) * CHUNK, h * HEAD_DIM:(h + 1) * HEAD_DIM] = (
                u[c * CHUNK:(c + 1) * CHUNK, h * HEAD_DIM:(h + 1) * HEAD_DIM] * mixed).astype(o_ref.dtype)


def _gmlp(z, ln_g, ln_b, w_s, b_s, tm=256):
    return pl.pallas_call(
        functools.partial(_gmlp_kernel, tm=tm),
        grid=(SEQ // tm,),
        in_specs=[
            pl.BlockSpec((tm, GMLP_WIDTH), lambda i: (i, 0)),
            pl.BlockSpec((tm, GMLP_WIDTH), lambda i: (i, 1)),
            pl.BlockSpec((1, GMLP_WIDTH), lambda i: (0, 0)),
            pl.BlockSpec((1, GMLP_WIDTH), lambda i: (0, 0)),
            pl.BlockSpec((GMLP_HEADS, CHUNK, CHUNK), lambda i: (0, 0, 0)),
            pl.BlockSpec((CHUNK, GMLP_HEADS), lambda i: (0, 0)),
        ],
        out_specs=pl.BlockSpec((tm, GMLP_WIDTH), lambda i: (i, 0)),
        out_shape=jax.ShapeDtypeStruct((SEQ, GMLP_WIDTH), BF16),
        compiler_params=_params(("arbitrary",), 40),
        name="gmlp",
    )(z, z, ln_g.reshape(1, -1), ln_b.reshape(1, -1), w_s, b_s.T)


def _qkv_prep_kernel(pos_ref, freq_ref, sign_ref, gq_ref, gk_ref, zq_ref, zk_ref, zv_ref, q_ref, k_ref, v_ref):
    ang = pos_ref[...].astype(F32) * freq_ref[...]
    cos = jnp.cos(ang)
    sin = jnp.sin(ang) * sign_ref[...]
    scale = HEAD_DIM ** -0.5 * math.log2(math.e)

    def prep(z_ref, g_ref, o_ref, mult):
        for s in range(DIFF_QK_WIDTH // HEAD_DIM):
            seg = _rms(z_ref[:, s * HEAD_DIM:(s + 1) * HEAD_DIM], g_ref[...])
            rot = pltpu.roll(seg, HEAD_DIM // 2, 1)
            out = seg * cos + rot * sin
            if mult != 1.0:
                out = out * mult
            o_ref[:, s * HEAD_DIM:(s + 1) * HEAD_DIM] = out.astype(o_ref.dtype)

    prep(zq_ref, gq_ref, q_ref, scale)
    prep(zk_ref, gk_ref, k_ref, 1.0)
    v_ref[...] = zv_ref[...].astype(v_ref.dtype)


def _qkv_prep(z, positions, gq, gk, tm=256):
    half = jnp.arange(0, HEAD_DIM, 2, dtype=F32) / HEAD_DIM
    inv_freq = 1.0 / (ROPE_THETA ** half)
    freq = jnp.concatenate([inv_freq, inv_freq]).reshape(1, HEAD_DIM)
    sign = jnp.concatenate([-jnp.ones((HEAD_DIM // 2,), F32), jnp.ones((HEAD_DIM // 2,), F32)]).reshape(1, HEAD_DIM)
    wide = DIFF_QK_WIDTH
    row_spec = lambda cb: pl.BlockSpec((tm, wide), lambda i: (i, cb))
    vec = pl.BlockSpec((1, HEAD_DIM), lambda i: (0, 0))
    out_spec = pl.BlockSpec((tm, wide), lambda i: (i, 0))
    shp = jax.ShapeDtypeStruct((SEQ, wide), BF16)
    return pl.pallas_call(
        _qkv_prep_kernel,
        grid=(SEQ // tm,),
        in_specs=[pl.BlockSpec((tm, 1), lambda i: (i, 0)), vec, vec, vec, vec,
                  row_spec(COL_Q // wide), row_spec(COL_K // wide), row_spec(COL_V // wide)],
        out_specs=[out_spec, out_spec, out_spec],
        out_shape=[shp, shp, shp],
        compiler_params=_params(("arbitrary",), 40),
        name="qkv_prep",
    )(positions.reshape(SEQ, 1), freq, sign, gq.reshape(1, -1), gk.reshape(1, -1), z, z, z)


KV_UNROLL = 4
def _diff_attn_kernel(lq1_ref, lk1_ref, lq2_ref, lk2_ref, sg_ref, q_ref, k_ref, v_ref, o_ref,
                      m_ref, l_ref, acc_ref, *, tq, lam_init):
    i = pl.program_id(1)
    m_ref[...] = jnp.full(m_ref.shape, -jnp.inf, F32)
    l_ref[...] = jnp.zeros(l_ref.shape, F32)
    acc_ref[...] = jnp.zeros(acc_ref.shape, F32)
    lanes = tq // 128

    def block(j, masked):
        start = pl.multiple_of(j * tq, tq)
        kb = k_ref[pl.ds(start, tq), :]
        vb = v_ref[pl.ds(start, tq), :]
        for a in range(2):
            q = q_ref[:, a * HEAD_DIM:(a + 1) * HEAD_DIM]
            k = kb[:, a * HEAD_DIM:(a + 1) * HEAD_DIM]
            s = lax.dot_general(q, k, (((1,), (1,)), ((), ())), preferred_element_type=F32)
            if masked:
                row = lax.broadcasted_iota(jnp.int32, (tq, tq), 0)
                col = lax.broadcasted_iota(jnp.int32, (tq, tq), 1)
                s = jnp.where(col <= row, s, -jnp.inf)
            m_prev = m_ref[a]
            m_new = jnp.maximum(m_prev, jnp.max(s, axis=-1, keepdims=True))
            alpha = jnp.exp2(m_prev - m_new)
            p = jnp.exp2(s - jnp.concatenate([m_new] * lanes, axis=1))
            psum = p[:, 0:128]
            for c in range(1, lanes):
                psum = psum + p[:, c * 128:(c + 1) * 128]
            l_ref[a] = alpha * l_ref[a] + psum
            acc_ref[a] = (jnp.concatenate([alpha] * (DIFF_V_DIM // 128), axis=1) * acc_ref[a]
                          + jnp.dot(p.astype(BF16), vb, preferred_element_type=F32))
            m_ref[a] = m_new

    def group(g, carry):
        for u in range(KV_UNROLL):
            block(KV_UNROLL * g + u, False)
        return carry

    def single(j, carry):
        block(j, False)
        return carry

    groups = i // KV_UNROLL
    lax.fori_loop(0, groups, group, 0)
    lax.fori_loop(groups * KV_UNROLL, i, single, 0)
    block(i, True)

    lam = (jnp.exp(jnp.sum(lq1_ref[...] * lk1_ref[...], axis=-1, keepdims=True))
           - jnp.exp(jnp.sum(lq2_ref[...] * lk2_ref[...], axis=-1, keepdims=True)) + lam_init)
    l1 = jnp.sum(l_ref[0], axis=-1, keepdims=True)
    l2 = jnp.sum(l_ref[1], axis=-1, keepdims=True)
    o = acc_ref[0] / l1 - lam * (acc_ref[1] / l2)
    o_ref[...] = (_rms(o, sg_ref[...]) * (1.0 - lam_init)).astype(o_ref.dtype)


def _diff_attn(q, k, v, lq1, lk1, lq2, lk2, subln_g, lam_init, tq=512):
    vec = pl.BlockSpec((1, HEAD_DIM), lambda h, i: (0, 0))
    kv_spec = pl.BlockSpec((SEQ, DIFF_V_DIM), lambda h, i: (0, h))
    return pl.pallas_call(
        functools.partial(_diff_attn_kernel, tq=tq, lam_init=lam_init),
        grid=(DIFF_HEADS, SEQ // tq),
        in_specs=[vec, vec, vec, vec, pl.BlockSpec((1, DIFF_V_DIM), lambda h, i: (0, 0)),
                  pl.BlockSpec((tq, DIFF_V_DIM), lambda h, i: (i, h)), kv_spec, kv_spec],
        out_specs=pl.BlockSpec((tq, DIFF_V_DIM), lambda h, i: (i, h)),
        out_shape=jax.ShapeDtypeStruct((SEQ, DIFF_WIDTH), BF16),
        scratch_shapes=[pltpu.VMEM((2, tq, 128), F32), pltpu.VMEM((2, tq, 128), F32),
                        pltpu.VMEM((2, tq, DIFF_V_DIM), F32)],
        compiler_params=_params(("arbitrary", "arbitrary"), 48),
        name="diff_attn",
    )(lq1.reshape(1, -1), lk1.reshape(1, -1), lq2.reshape(1, -1), lk2.reshape(1, -1),
      subln_g.reshape(1, -1), q, k, v)


def _mem_attn_kernel(zm_ref, kv_ref, gq_ref, gk_ref, o_ref):
    scale = MEM_HEAD_DIM ** -0.5
    for h in range(MEM_HEADS):
        lo, hi = h * MEM_HEAD_DIM, (h + 1) * MEM_HEAD_DIM
        q = (_rms(zm_ref[:, lo:hi], gq_ref[...]) * scale).astype(BF16)
        k = _rms(kv_ref[:, lo:hi], gk_ref[...]).astype(BF16)
        v = kv_ref[:, MEM_WIDTH + lo:MEM_WIDTH + hi].astype(BF16)
        s = lax.dot_general(q, k, (((1,), (1,)), ((), ())), preferred_element_type=F32)
        p = jnp.exp(s - jnp.max(s, axis=-1, keepdims=True))
        l = jnp.sum(p, axis=-1, keepdims=True)
        o = jnp.dot(p.astype(BF16), v, preferred_element_type=F32) / l
        o_ref[:, lo:hi] = o.astype(o_ref.dtype)


def _mem_attn(z, kv, gq, gk, tm=512):
    vec = pl.BlockSpec((1, MEM_HEAD_DIM), lambda i: (0, 0))
    return pl.pallas_call(
        _mem_attn_kernel,
        grid=(SEQ // tm,),
        in_specs=[pl.BlockSpec((tm, MEM_WIDTH), lambda i: (i, COL_M // MEM_WIDTH)),
                  pl.BlockSpec((MEM_LEN, 2 * MEM_WIDTH), lambda i: (0, 0)), vec, vec],
        out_specs=pl.BlockSpec((tm, MEM_WIDTH), lambda i: (i, 0)),
        out_shape=jax.ShapeDtypeStruct((SEQ, MEM_WIDTH), BF16),
        compiler_params=_params(("arbitrary",), 40),
        name="mem_attn",
    )(z, kv, gq.reshape(1, -1), gk.reshape(1, -1))


def _router_kernel(x_ref, g_ref, wr_ref, br_ref, h_ref, gate_ref, exp_ref, rank_ref, cnt_ref, carry_ref, *, tm):
    @pl.when(pl.program_id(0) == 0)
    def _():
        carry_ref[...] = jnp.zeros(carry_ref.shape, F32)

    h = _rms(x_ref[...], g_ref[...])
    _store_slabs(h_ref, h, tm)
    w = wr_ref[...]
    h_hi = h.astype(BF16)
    h_lo = (h - h_hi.astype(F32)).astype(BF16)
    w_hi = w.astype(BF16)
    w_lo = (w - w_hi.astype(F32)).astype(BF16)
    logits = (jnp.dot(h_hi, w_hi, preferred_element_type=F32) + jnp.dot(h_hi, w_lo, preferred_element_type=F32)
              + jnp.dot(h_lo, w_hi, preferred_element_type=F32) + br_ref[...])
    lane = lax.broadcasted_iota(jnp.int32, logits.shape, 1)
    slot = lax.broadcasted_iota(jnp.int32, (tm, TOP_K), 1)
    work = logits
    sel = jnp.zeros(logits.shape, jnp.bool_)
    hits = []
    top = None
    nums = jnp.zeros((tm, TOP_K), F32)
    experts = jnp.zeros((tm, TOP_K), jnp.int32)
    denom = jnp.zeros((tm, 1), F32)
    for k in range(TOP_K):
        m = jnp.max(work, axis=-1, keepdims=True)
        idx = jnp.min(jnp.where(work == m, lane, N_EXPERTS), axis=-1, keepdims=True)
        hit = lane == idx
        if k == 0:
            top = m
        e = jnp.exp(m - top)
        nums = jnp.where(slot == k, e, nums)
        experts = jnp.where(slot == k, idx, experts)
        denom = denom + e
        work = jnp.where(hit, -jnp.inf, work)
        sel = jnp.logical_or(sel, hit)
        hits.append(hit)
    gate_ref[...] = nums / denom
    exp_ref[...] = experts
    r = lax.broadcasted_iota(jnp.int32, (tm, tm), 0)
    c = lax.broadcasted_iota(jnp.int32, (tm, tm), 1)
    strict = jnp.where(c < r, 1.0, 0.0).astype(BF16)
    chosen = jnp.where(sel, 1.0, 0.0)
    rank = jnp.dot(strict, chosen.astype(BF16), preferred_element_type=F32) + carry_ref[...]
    ranks = jnp.zeros((tm, TOP_K), F32)
    for k in range(TOP_K):
        ranks = jnp.where(slot == k, jnp.sum(jnp.where(hits[k], rank, 0.0), axis=-1, keepdims=True), ranks)
    rank_ref[...] = ranks.astype(jnp.int32)
    carry_ref[...] = carry_ref[...] + jnp.sum(chosen, axis=0, keepdims=True)
    cnt_ref[...] = carry_ref[...].astype(jnp.int32)


def _router(x, g, w_router, b_router, tm=256):
    small = lambda: pl.BlockSpec((tm, TOP_K), lambda i: (i, 0))
    return pl.pallas_call(
        functools.partial(_router_kernel, tm=tm),
        grid=(SEQ // tm,),
        in_specs=[pl.BlockSpec((tm, D_MODEL), lambda i: (i, 0)), pl.BlockSpec((1, D_MODEL), lambda i: (0, 0)),
                  pl.BlockSpec((D_MODEL, N_EXPERTS), lambda i: (0, 0)), pl.BlockSpec((1, N_EXPERTS), lambda i: (0, 0))],
        out_specs=[pl.BlockSpec((tm * SLAB_ROWS, SLAB_LANES), lambda i: (i, 0)), small(), small(), small(),
                   pl.BlockSpec((1, N_EXPERTS), lambda i: (0, 0))],
        out_shape=[jax.ShapeDtypeStruct((SEQ * SLAB_ROWS, SLAB_LANES), jnp.uint32),
                   jax.ShapeDtypeStruct((SEQ, TOP_K), F32),
                   jax.ShapeDtypeStruct((SEQ, TOP_K), jnp.int32),
                   jax.ShapeDtypeStruct((SEQ, TOP_K), jnp.int32),
                   jax.ShapeDtypeStruct((1, N_EXPERTS), jnp.int32)],
        scratch_shapes=[pltpu.VMEM((1, N_EXPERTS), F32)],
        compiler_params=_params(("arbitrary",), 40),
        name="router",
    )(x, g.reshape(1, -1), w_router, b_router.reshape(1, -1))


def _slot_copy(src_ref, src_row, dst_ref, dst_row, sem):
    return pltpu.make_async_copy(src_ref.at[pl.ds(src_row * SLAB_ROWS, SLAB_ROWS)],
                                 dst_ref.at[pl.ds(dst_row * SLAB_ROWS, SLAB_ROWS)], sem)


def _tile_fill_copy(zero_ref, dst_ref, tile, sem):
    rows = MOE_TILE * SLAB_ROWS
    return pltpu.make_async_copy(zero_ref, dst_ref.at[pl.ds(tile * rows, rows)], sem)


def _dispatch_kernel(pstart_ref, fill_ref, exp_ref, rank_ref, h_hbm, o_ref, slot_ref,
                     zero_ref, buf_a, buf_b, fill_sem, in_sem, out_sem, *, tc):
    i = pl.program_id(0)
    n = pl.num_programs(0)
    rows = tc * SLAB_ROWS

    def load(tile, buf, s):
        return pltpu.make_async_copy(h_hbm.at[pl.ds(tile * rows, rows)], buf, in_sem.at[s])

    def wait_rows(buf, s):
        for _ in range(TOP_K):
            pltpu.make_async_copy(buf, o_ref.at[pl.ds(0, rows)], out_sem.at[s]).wait()

    @pl.when(i == 0)
    def _():
        load(0, buf_a, 0).start()
        zero_ref[...] = jnp.zeros(zero_ref.shape, zero_ref.dtype)

        def fill(t, carry):
            @pl.when(fill_ref[t] != 0)
            def _():
                _tile_fill_copy(zero_ref, o_ref, t, fill_sem).start()
            return carry

        def fill_wait(t, carry):
            @pl.when(fill_ref[t] != 0)
            def _():
                _tile_fill_copy(zero_ref, o_ref, t, fill_sem).wait()
            return carry

        lax.fori_loop(0, MOE_NT, fill, 0)
        lax.fori_loop(0, MOE_NT, fill_wait, 0)

    def step(cur_buf, cs, nxt_buf, ns):
        @pl.when(i > 0)
        def _():
            wait_rows(nxt_buf, ns)

        @pl.when(i + 1 < n)
        def _():
            load(i + 1, nxt_buf, ns).start()

        load(i, cur_buf, cs).wait()

        def issue(r, carry):
            for k in range(TOP_K):
                a = r * TOP_K + k
                slot = pstart_ref[exp_ref[0, 0, a]] + rank_ref[0, 0, a]
                slot_ref[0, 0, a] = slot
                _slot_copy(cur_buf, r, o_ref, slot, out_sem.at[cs]).start()
            return carry

        lax.fori_loop(0, tc, issue, 0, unroll=4)

        @pl.when(i == n - 1)
        def _():
            wait_rows(cur_buf, cs)

    @pl.when(i % 2 == 0)
    def _():
        step(buf_a, 0, buf_b, 1)

    @pl.when(i % 2 == 1)
    def _():
        step(buf_b, 1, buf_a, 0)


def _dispatch(hp, experts4, rank4, pstart, fill_tiles, tc=128):
    nb = SEQ // tc
    idx_spec = pl.BlockSpec((1, 1, tc * TOP_K), lambda i, ps, ft: (i, 0, 0), memory_space=pltpu.SMEM)
    return pl.pallas_call(
        functools.partial(_dispatch_kernel, tc=tc),
        grid_spec=pltpu.PrefetchScalarGridSpec(
            num_scalar_prefetch=2,
            grid=(nb,),
            in_specs=[idx_spec, idx_spec, pl.BlockSpec(memory_space=pl.ANY)],
            out_specs=[pl.BlockSpec(memory_space=pl.ANY), idx_spec],
            scratch_shapes=[pltpu.VMEM((MOE_TILE * SLAB_ROWS, SLAB_LANES), jnp.uint32),
                            pltpu.VMEM((tc * SLAB_ROWS, SLAB_LANES), jnp.uint32),
                            pltpu.VMEM((tc * SLAB_ROWS, SLAB_LANES), jnp.uint32),
                            pltpu.SemaphoreType.DMA(()), pltpu.SemaphoreType.DMA((2,)), pltpu.SemaphoreType.DMA((2,))],
        ),
        out_shape=[jax.ShapeDtypeStruct((MOE_ROWS * SLAB_ROWS, SLAB_LANES), jnp.uint32),
                   jax.ShapeDtypeStruct((nb, 1, tc * TOP_K), jnp.int32)],
        compiler_params=_params(("arbitrary",), 40),
        name="moe_dispatch",
    )(pstart, fill_tiles, experts4.reshape(nb, 1, tc * TOP_K), rank4.reshape(nb, 1, tc * TOP_K), hp)


def _group_edges(blk_ref, t):
    e = blk_ref[t]
    first = jnp.logical_or(t == 0, blk_ref[jnp.maximum(t - 1, 0)] != e)
    last = jnp.logical_or(t == MOE_NT - 1, blk_ref[jnp.minimum(t + 1, MOE_NT - 1)] != e)
    return e, first, last


def _stream_weights(blk_ref, nxt_ref, slot_ref, n_pass, copies):
    j = pl.program_id(0)
    t = pl.program_id(1)
    e, first, last = _group_edges(blk_ref, t)

    @pl.when(jnp.logical_and(j == 0, t == 0))
    def _():
        slot_ref[0] = 0
        for c in copies(e, j, 0):
            c.start()

    @pl.when(first)
    def _():
        slot = slot_ref[0]
        for c in copies(e, j, slot):
            c.wait()
        nt = nxt_ref[t]
        same_pass = nt < MOE_NT
        ne = blk_ref[jnp.where(same_pass, nt, 0)]
        nj = jnp.where(same_pass, j, j + 1)

        @pl.when(jnp.logical_or(same_pass, j + 1 < n_pass))
        def _():
            for c in copies(ne, nj, 1 - slot):
                c.start()

    slot = slot_ref[0]

    @pl.when(last)
    def _():
        slot_ref[0] = 1 - slot

    return slot


def _moe_up_kernel(blk_ref, nxt_ref, nused_ref, x_ref, w_hbm, b_ref, o_ref, wbuf, sem, slot_ref, *, tn):
    j = pl.program_id(0)
    t = pl.program_id(1)

    def copies(e, j, slot):
        return [pltpu.make_async_copy(w_hbm.at[e, :, pl.ds(pl.multiple_of(part * D_FF + j * tn, tn), tn)],
                                      wbuf.at[slot, part], sem.at[slot, part]) for part in range(2)]

    slot = _stream_weights(blk_ref, nxt_ref, slot_ref, D_FF // tn, copies)

    @pl.when(t < nused_ref[0])
    def _():
        x = _load_slabs(x_ref, MOE_TILE)

        e = blk_ref[t]

        def proj(part):
            bias = b_ref[pl.ds(e, 1), pl.ds(pl.multiple_of(part * D_FF + j * tn, tn), tn)]
            return jnp.dot(x, wbuf[slot, part].astype(BF16), preferred_element_type=F32) + bias

        gate = jnp.minimum(proj(0), SWIGLU_LIMIT)
        up = jnp.clip(proj(1), -SWIGLU_LIMIT, SWIGLU_LIMIT)
        act = (up + 1.0) * (gate * jax.nn.sigmoid(SWIGLU_ALPHA * gate))
        o_ref[...] = act.astype(o_ref.dtype)

    @pl.when(t >= nused_ref[0])
    def _():
        o_ref[...] = jnp.zeros(o_ref.shape, o_ref.dtype)


def _moe_up(xs, blk_e, nxt, n_used, w_gate_up, b_gate_up, tn=512):
    nj = D_FF // tn
    last = lambda t, nu: jnp.minimum(t, nu[0] - 1)
    return pl.pallas_call(
        functools.partial(_moe_up_kernel, tn=tn),
        grid_spec=pltpu.PrefetchScalarGridSpec(
            num_scalar_prefetch=3,
            grid=(nj, MOE_NT),
            in_specs=[
                pl.BlockSpec((MOE_TILE * SLAB_ROWS, SLAB_LANES), lambda j, t, be, nx, nu: (last(t, nu), 0)),
                pl.BlockSpec(memory_space=pl.ANY),
                pl.BlockSpec((N_EXPERTS, 2 * D_FF), lambda j, t, be, nx, nu: (0, 0)),
            ],
            out_specs=pl.BlockSpec((MOE_TILE, tn), lambda j, t, be, nx, nu: (t, j)),
            scratch_shapes=[pltpu.VMEM((2, 2, D_MODEL, tn), F32), pltpu.SemaphoreType.DMA((2, 2)),
                            pltpu.SMEM((1,), jnp.int32)],
        ),
        out_shape=jax.ShapeDtypeStruct((MOE_ROWS, D_FF), BF16),
        compiler_params=_params(("arbitrary", "arbitrary"), 56),
        name="moe_up",
    )(blk_e, nxt, n_used, xs, w_gate_up, b_gate_up)


DOWN_TN = D_MODEL // 2
DOWN_WORDS = DOWN_TN // 2
DOWN_ROWS = DOWN_WORDS // SLAB_LANES


def _moe_down_kernel(blk_ref, nxt_ref, nused_ref, a_ref, w_hbm, b_ref, o_ref, wbuf, sem, slot_ref):
    t = pl.program_id(1)

    def copies(e, j, slot):
        return [pltpu.make_async_copy(w_hbm.at[e, :, pl.ds(pl.multiple_of(j * DOWN_TN, DOWN_TN), DOWN_TN)],
                                      wbuf.at[slot], sem.at[slot])]

    slot = _stream_weights(blk_ref, nxt_ref, slot_ref, D_MODEL // DOWN_TN, copies)

    @pl.when(t < nused_ref[0])
    def _():
        bias = b_ref[pl.ds(blk_ref[t], 1), pl.ds(pl.multiple_of(pl.program_id(0) * DOWN_TN, DOWN_TN), DOWN_TN)]
        y = jnp.dot(a_ref[...], wbuf[slot].astype(BF16), preferred_element_type=F32) + bias
        words = pltpu.pack_elementwise([y[:, :DOWN_WORDS], y[:, DOWN_WORDS:]], packed_dtype=BF16)
        for s in range(DOWN_ROWS):
            o_ref[:, s, :] = words[:, s * SLAB_LANES:(s + 1) * SLAB_LANES]

    @pl.when(t >= nused_ref[0])
    def _():
        o_ref[...] = jnp.zeros(o_ref.shape, o_ref.dtype)


def _moe_down(act, blk_e, nxt, n_used, w_down, b_down):
    nj = D_MODEL // DOWN_TN
    last = lambda t, nu: jnp.minimum(t, nu[0] - 1)
    return pl.pallas_call(
        _moe_down_kernel,
        grid_spec=pltpu.PrefetchScalarGridSpec(
            num_scalar_prefetch=3,
            grid=(nj, MOE_NT),
            in_specs=[
                pl.BlockSpec((MOE_TILE, D_FF), lambda j, t, be, nx, nu: (last(t, nu), 0)),
                pl.BlockSpec(memory_space=pl.ANY),
                pl.BlockSpec((N_EXPERTS, D_MODEL), lambda j, t, be, nx, nu: (0, 0)),
            ],
            out_specs=pl.BlockSpec((MOE_TILE, DOWN_ROWS, SLAB_LANES), lambda j, t, be, nx, nu: (t, j, 0)),
            scratch_shapes=[pltpu.VMEM((2, D_FF, DOWN_TN), F32), pltpu.SemaphoreType.DMA((2,)),
                            pltpu.SMEM((1,), jnp.int32)],
        ),
        out_shape=jax.ShapeDtypeStruct((MOE_ROWS, SLAB_ROWS, SLAB_LANES), jnp.uint32),
        compiler_params=_params(("arbitrary", "arbitrary"), 56),
        name="moe_down",
    )(blk_e, nxt, n_used, act, w_down, b_down).reshape(MOE_ROWS * SLAB_ROWS, SLAB_LANES)


COMBINE_ROWS = 32


def _combine_kernel(cur_ref, nxt_ref, ys_ref, x_ref, g_ref, o_ref, buf_a, buf_b, sem, *, tc):
    i = pl.program_id(0)

    def fetch(idx_ref, a, buf, slot):
        r, k = divmod(a, TOP_K)
        return _slot_copy(ys_ref, idx_ref[0, 0, a], buf.at[k], r, sem.at[slot])

    def wait_tile(buf, slot):
        for k in range(TOP_K):
            pltpu.make_async_copy(ys_ref.at[pl.ds(0, tc * SLAB_ROWS)], buf.at[k], sem.at[slot]).wait()

    @pl.when(i == 0)
    def _():
        def issue(r, carry):
            for k in range(TOP_K):
                _slot_copy(ys_ref, cur_ref[0, 0, r * TOP_K + k], buf_a.at[k], r, sem.at[0]).start()
            return carry

        lax.fori_loop(0, tc, issue, 0, unroll=4)

    def step(cur_buf, cur_slot, nxt_buf, nxt_slot):
        wait_tile(cur_buf, cur_slot)
        n_chunks = (tc // COMBINE_ROWS) * SLAB_ROWS
        per_chunk = tc * TOP_K // n_chunks
        a = 0
        for t0 in range(0, tc, COMBINE_ROWS):
            rows = slice(t0, t0 + COMBINE_ROWS)
            gates = [jnp.broadcast_to(g_ref[rows, k:k + 1], (COMBINE_ROWS, SLAB_LANES)) for k in range(TOP_K)]
            for s in range(SLAB_ROWS):
                for _ in range(per_chunk):
                    fetch(nxt_ref, a, nxt_buf, nxt_slot).start()
                    a += 1
                lo_sum = None
                hi_sum = None
                for k in range(TOP_K):
                    words = cur_buf[k, pl.ds(t0 * SLAB_ROWS + s, COMBINE_ROWS, stride=SLAB_ROWS), :]
                    lo, hi = _unpack_bf16(words)
                    lo_sum = gates[k] * lo if lo_sum is None else lo_sum + gates[k] * lo
                    hi_sum = gates[k] * hi if hi_sum is None else hi_sum + gates[k] * hi
                c = (s // DOWN_ROWS) * DOWN_TN + (s % DOWN_ROWS) * SLAB_LANES
                o_ref[rows, c:c + SLAB_LANES] = x_ref[rows, c:c + SLAB_LANES] + lo_sum
                c += DOWN_WORDS
                o_ref[rows, c:c + SLAB_LANES] = x_ref[rows, c:c + SLAB_LANES] + hi_sum

        @pl.when(i == pl.num_programs(0) - 1)
        def _():
            wait_tile(nxt_buf, nxt_slot)

    @pl.when(i % 2 == 0)
    def _():
        step(buf_a, 0, buf_b, 1)

    @pl.when(i % 2 == 1)
    def _():
        step(buf_b, 1, buf_a, 0)


def _combine(ys, x2, slots, gates4, tc=128):
    nb = SEQ // tc
    idx_spec = lambda f: pl.BlockSpec((1, 1, tc * TOP_K), lambda i: (f(i), 0, 0), memory_space=pltpu.SMEM)
    return pl.pallas_call(
        functools.partial(_combine_kernel, tc=tc),
        grid=(nb,),
        in_specs=[idx_spec(lambda i: i), idx_spec(lambda i: jnp.minimum(i + 1, nb - 1)),
                  pl.BlockSpec(memory_space=pl.ANY),
                  pl.BlockSpec((tc, D_MODEL), lambda i: (i, 0)),
                  pl.BlockSpec((tc, TOP_K), lambda i: (i, 0))],
        out_specs=pl.BlockSpec((tc, D_MODEL), lambda i: (i, 0)),
        out_shape=jax.ShapeDtypeStruct((SEQ, D_MODEL), F32),
        scratch_shapes=[pltpu.VMEM((TOP_K, tc * SLAB_ROWS, SLAB_LANES), jnp.uint32),
                        pltpu.VMEM((TOP_K, tc * SLAB_ROWS, SLAB_LANES), jnp.uint32), pltpu.SemaphoreType.DMA((2,))],
        compiler_params=_params(("arbitrary",), 40),
        name="moe_combine",
    )(slots, slots, ys, x2, gates4)


def kernel(x, mem, positions, attn_norm_g, w_in, gmlp_ln_g, gmlp_ln_b, gmlp_w_s, gmlp_b_s, diff_q_norm_g, diff_k_norm_g, diff_lambda_q1, diff_lambda_k1, diff_lambda_q2, diff_lambda_k2, diff_subln_g, mem_norm_g, w_mem_kv, mem_q_norm_g, mem_k_norm_g, w_out, ffn_norm_g, w_router, b_router, w_gate_up, b_gate_up, w_down, b_down):
    depth = attn_norm_g.shape[0]
    xs = x.reshape(SEQ, D_MODEL)
    mem2 = mem.reshape(MEM_LEN, D_MODEL)
    for i in range(depth):
        lam_init = 0.8 - 0.6 * math.exp(-0.3 * i)
        h = _rmsnorm_rows(xs, attn_norm_g[i], tm=256)
        z = _matmul([h], w_in[i], None, F32, tm=1024, tn=512, name="in_proj")

        y_g = _gmlp(z, gmlp_ln_g[i], gmlp_ln_b[i], gmlp_w_s[i], gmlp_b_s[i])

        q, k, v = _qkv_prep(z, positions, diff_q_norm_g[i], diff_k_norm_g[i])
        y_d = _diff_attn(q, k, v, diff_lambda_q1[i], diff_lambda_k1[i], diff_lambda_q2[i], diff_lambda_k2[i],
                         diff_subln_g[i], lam_init)

        hm = _rmsnorm_rows(mem2, mem_norm_g[i], tm=256)
        kv = _matmul([hm], w_mem_kv[i], None, F32, tm=MEM_LEN, tn=512, name="mem_kv_proj")
        y_m = _mem_attn(z, kv, mem_q_norm_g[i], mem_k_norm_g[i])

        x2 = _matmul([y_g, y_d, y_m], w_out[i], xs, F32, tm=1024, tn=512, name="out_proj")

        hp, gates4, experts4, rank4, counts = _router(x2, ffn_norm_g[i], w_router[i], b_router[i])

        counts = counts[0]
        padded = (counts + MOE_TILE - 1) // MOE_TILE * MOE_TILE
        pend = jnp.cumsum(padded)
        pstart = pend - padded
        n_used = (pend[-1] // MOE_TILE).astype(jnp.int32).reshape(1)
        blk_start = jnp.arange(MOE_NT, dtype=jnp.int32) * MOE_TILE
        blk_e = jnp.minimum(jnp.sum(pend[None, :] <= blk_start[:, None], axis=1), N_EXPERTS - 1).astype(jnp.int32)
        blk_end = blk_start + MOE_TILE
        fill_tiles = jnp.logical_or(jnp.any(blk_end[:, None] == pend[None, :], axis=1),
                                    blk_start >= pend[-1]).astype(jnp.int32)

        tile_ids = jnp.arange(MOE_NT, dtype=jnp.int32)
        run_start = jnp.concatenate([jnp.ones((1,), jnp.bool_), blk_e[1:] != blk_e[:-1]])
        starts_from = lax.cummin(jnp.where(run_start, tile_ids, MOE_NT), axis=0, reverse=True)
        nxt = jnp.concatenate([starts_from[1:], jnp.full((1,), MOE_NT, jnp.int32)]).astype(jnp.int32)

        xg, slots = _dispatch(hp, experts4, rank4, pstart, fill_tiles)
        act = _moe_up(xg, blk_e, nxt, n_used, w_gate_up[i], b_gate_up[i])
        ys = _moe_down(act, blk_e, nxt, n_used, w_down[i], b_down[i])
        xs = _combine(ys, x2, slots, gates4)
    return xs.reshape(x.shape)
```

```python
import functools
import math

import jax
import jax.numpy as jnp
from jax import lax
from jax.experimental import pallas as pl
from jax.experimental.pallas import tpu as pltpu

F32 = jnp.float32
BF16 = jnp.bfloat16

D_MODEL = 4096
SEQ = 8192
HEAD_DIM = 128
CHUNK = 128
GMLP_WIDTH = 1024
GMLP_HEADS = 8
DIFF_HEADS = 8
DIFF_V_DIM = 256
DIFF_QK_WIDTH = 2048
DIFF_WIDTH = 2048
MEM_WIDTH = 1024
MEM_HEADS = 4
MEM_HEAD_DIM = 256
MEM_LEN = 256
IN_WIDTH = 9216
N_EXPERTS = 32
TOP_K = 4
D_FF = 1536
SWIGLU_LIMIT = 7.0
SWIGLU_ALPHA = 1.702
ROPE_THETA = 10000.0
EPS = 1e-6

COL_Q = 2 * GMLP_WIDTH
COL_K = COL_Q + DIFF_QK_WIDTH
COL_V = COL_K + DIFF_QK_WIDTH
COL_M = COL_V + DIFF_WIDTH

MOE_TILE = 256
MOE_ROWS = ((SEQ * TOP_K + N_EXPERTS * (MOE_TILE - 1) + MOE_TILE - 1) // MOE_TILE) * MOE_TILE
MOE_NT = MOE_ROWS // MOE_TILE

MIB = 1 << 20


def _params(sem, vmem_mib):
    return pltpu.CompilerParams(dimension_semantics=sem, vmem_limit_bytes=vmem_mib * MIB)


def _rms(x, g):
    ms = jnp.mean(x * x, axis=-1, keepdims=True)
    return x * lax.rsqrt(ms + EPS) * g


SLAB_LANES = 128
SLAB_ROWS = D_MODEL // 2 // SLAB_LANES


def _unpack_bf16(words):
    lo = pltpu.unpack_elementwise(words, index=0, packed_dtype=BF16, unpacked_dtype=F32)
    hi = pltpu.unpack_elementwise(words, index=1, packed_dtype=BF16, unpacked_dtype=F32)
    return lo, hi


def _store_slabs(o_ref, h, rows):
    half = D_MODEL // 2
    for s in range(SLAB_ROWS):
        lo = h[:, s * SLAB_LANES:(s + 1) * SLAB_LANES]
        hi = h[:, half + s * SLAB_LANES:half + (s + 1) * SLAB_LANES]
        o_ref[pl.ds(s, rows, stride=SLAB_ROWS), :] = pltpu.pack_elementwise([lo, hi], packed_dtype=BF16)


def _load_slabs(x_ref, rows):
    los, his = [], []
    for s in range(SLAB_ROWS):
        lo, hi = _unpack_bf16(x_ref[pl.ds(s, rows, stride=SLAB_ROWS), :])
        los.append(lo.astype(BF16))
        his.append(hi.astype(BF16))
    return jnp.concatenate(los + his, axis=1)


def _rmsnorm_kernel(x_ref, g_ref, o_ref):
    o_ref[...] = _rms(x_ref[...], g_ref[...]).astype(o_ref.dtype)


def _rmsnorm_rows(x, g, tm):
    m, d = x.shape
    return pl.pallas_call(
        _rmsnorm_kernel,
        grid=(m // tm,),
        in_specs=[pl.BlockSpec((tm, d), lambda i: (i, 0)), pl.BlockSpec((1, d), lambda i: (0, 0))],
        out_specs=pl.BlockSpec((tm, d), lambda i: (i, 0)),
        out_shape=jax.ShapeDtypeStruct((m, d), BF16),
        compiler_params=_params(("arbitrary",), 40),
        name="rmsnorm_rows",
    )(x, g.reshape(1, d))


def _matmul_kernel(*refs, widths, has_res):
    n = len(widths)
    lhs = refs[:n]
    w_ref = refs[n]
    res_ref = refs[n + 1] if has_res else None
    o_ref = refs[n + 1 + has_res]

    acc = None
    off = 0
    for r, kw in zip(lhs, widths):
        part = jnp.dot(r[...], w_ref[off:off + kw, :].astype(BF16), preferred_element_type=F32)
        acc = part if acc is None else acc + part
        off += kw
    if has_res:
        acc = acc + res_ref[...]
    o_ref[...] = acc.astype(o_ref.dtype)


def _matmul(lhs_list, w, res, out_dtype, tm, tn, name, cols=None):
    m = lhs_list[0].shape[0]
    k = w.shape[0]
    col0, n = cols if cols is not None else (0, w.shape[1])
    assert col0 % tn == 0 and n % tn == 0
    widths = tuple(a.shape[1] for a in lhs_list)
    assert sum(widths) == k
    in_specs = [pl.BlockSpec((tm, kw), lambda j, i: (i, 0)) for kw in widths]
    in_specs.append(pl.BlockSpec((k, tn), lambda j, i: (0, j + col0 // tn)))
    args = list(lhs_list) + [w]
    if res is not None:
        in_specs.append(pl.BlockSpec((tm, tn), lambda j, i: (i, j)))
        args.append(res)
    return pl.pallas_call(
        functools.partial(_matmul_kernel, widths=widths, has_res=res is not None),
        grid=(n // tn, m // tm),
        in_specs=in_specs,
        out_specs=pl.BlockSpec((tm, tn), lambda j, i: (i, j)),
        out_shape=jax.ShapeDtypeStruct((m, n), out_dtype),
        compiler_params=_params(("arbitrary", "arbitrary"), 56),
        name=name,
    )(*args)


def _gmlp_kernel(zu_ref, zv_ref, lng_ref, lnb_ref, ws_ref, bst_ref, o_ref, *, tm):
    inv_sqrt2 = 1.0 / math.sqrt(2.0)

    def gelu(x):
        return 0.5 * x * (1.0 + lax.erf(x * inv_sqrt2))

    u = gelu(zu_ref[...])
    v = gelu(zv_ref[...])
    mu = jnp.mean(v, axis=-1, keepdims=True)
    vc = v - mu
    var = jnp.mean(vc * vc, axis=-1, keepdims=True)
    vn = (vc * lax.rsqrt(var + EPS) * lng_ref[...] + lnb_ref[...]).astype(BF16)
    row = lax.broadcasted_iota(jnp.int32, (CHUNK, CHUNK), 0)
    col = lax.broadcasted_iota(jnp.int32, (CHUNK, CHUNK), 1)
    causal = col <= row
    bst = bst_ref[...]
    for h in range(GMLP_HEADS):
        w = jnp.where(causal, ws_ref[h], 0.0).astype(BF16)
        b = bst[:, h:h + 1]
        for c in range(tm // CHUNK):
            vs = vn[c * CHUNK:(c + 1) * CHUNK, h * HEAD_DIM:(h + 1) * HEAD_DIM]
            mixed = jnp.dot(w, vs, preferred_element_type=F32) + b
            o_ref[c * CHUNK:(c + 1) * CHUNK, h * HEAD_DIM:(h + 1) * HEAD_DIM] = (
                u[c * CHUNK:(c + 1) * CHUNK, h * HEAD_DIM:(h + 1) * HEAD_DIM] * mixed).astype(o_ref.dtype)


def _gmlp(z, ln_g, ln_b, w_s, b_s, tm=256):
    return pl.pallas_call(
        functools.partial(_gmlp_kernel, tm=tm),
        grid=(SEQ // tm,),
        in_specs=[
            pl.BlockSpec((tm, GMLP_WIDTH), lambda i: (i, 0)),
            pl.BlockSpec((tm, GMLP_WIDTH), lambda i: (i, 1)),
            pl.BlockSpec((1, GMLP_WIDTH), lambda i: (0, 0)),
            pl.BlockSpec((1, GMLP_WIDTH), lambda i: (0, 0)),
            pl.BlockSpec((GMLP_HEADS, CHUNK, CHUNK), lambda i: (0, 0, 0)),
            pl.BlockSpec((CHUNK, GMLP_HEADS), lambda i: (0, 0)),
        ],
        out_specs=pl.BlockSpec((tm, GMLP_WIDTH), lambda i: (i, 0)),
        out_shape=jax.ShapeDtypeStruct((SEQ, GMLP_WIDTH), BF16),
        compiler_params=_params(("arbitrary",), 40),
        name="gmlp",
    )(z, z, ln_g.reshape(1, -1), ln_b.reshape(1, -1), w_s, b_s.T)


def _qk_prep_kernel(pos_ref, freq_ref, sign_ref, gq_ref, gk_ref, zq_ref, zk_ref, q_ref, k_ref):
    ang = pos_ref[...].astype(F32) * freq_ref[...]
    cos = jnp.cos(ang)
    sin = jnp.sin(ang) * sign_ref[...]
    scale = HEAD_DIM ** -0.5 * math.log2(math.e)

    def prep(z_ref, g_ref, o_ref, mult):
        for s in range(DIFF_QK_WIDTH // HEAD_DIM):
            seg = _rms(z_ref[:, s * HEAD_DIM:(s + 1) * HEAD_DIM], g_ref[...])
            rot = pltpu.roll(seg, HEAD_DIM // 2, 1)
            out = seg * cos + rot * sin
            if mult != 1.0:
                out = out * mult
            o_ref[:, s * HEAD_DIM:(s + 1) * HEAD_DIM] = out.astype(o_ref.dtype)

    prep(zq_ref, gq_ref, q_ref, scale)
    prep(zk_ref, gk_ref, k_ref, 1.0)


def _qk_prep(z, positions, gq, gk, tm=256):
    half = jnp.arange(0, HEAD_DIM, 2, dtype=F32) / HEAD_DIM
    inv_freq = 1.0 / (ROPE_THETA ** half)
    freq = jnp.concatenate([inv_freq, inv_freq]).reshape(1, HEAD_DIM)
    sign = jnp.concatenate([-jnp.ones((HEAD_DIM // 2,), F32), jnp.ones((HEAD_DIM // 2,), F32)]).reshape(1, HEAD_DIM)
    wide = DIFF_QK_WIDTH
    row_spec = lambda cb: pl.BlockSpec((tm, wide), lambda i: (i, cb))
    vec = pl.BlockSpec((1, HEAD_DIM), lambda i: (0, 0))
    out_spec = pl.BlockSpec((tm, wide), lambda i: (i, 0))
    shp = jax.ShapeDtypeStruct((SEQ, wide), BF16)
    return pl.pallas_call(
        _qk_prep_kernel,
        grid=(SEQ // tm,),
        in_specs=[pl.BlockSpec((tm, 1), lambda i: (i, 0)), vec, vec, vec, vec,
                  row_spec(COL_Q // wide), row_spec(COL_K // wide)],
        out_specs=[out_spec, out_spec],
        out_shape=[shp, shp],
        compiler_params=_params(("arbitrary",), 40),
        name="qk_prep",
    )(positions.reshape(SEQ, 1), freq, sign, gq.reshape(1, -1), gk.reshape(1, -1), z, z)


KV_UNROLL = 4
def _diff_attn_kernel(lq1_ref, lk1_ref, lq2_ref, lk2_ref, sg_ref, q_ref, k_ref, v_ref, o_ref,
                      m_ref, l_ref, acc_ref, *, tq, lam_init):
    i = pl.program_id(1)
    m_ref[...] = jnp.full(m_ref.shape, -jnp.inf, F32)
    l_ref[...] = jnp.zeros(l_ref.shape, F32)
    acc_ref[...] = jnp.zeros(acc_ref.shape, F32)
    lanes = tq // 128

    def block(j, masked):
        start = pl.multiple_of(j * tq, tq)
        kb = k_ref[pl.ds(start, tq), :]
        vb = v_ref[pl.ds(start, tq), :]
        for a in range(2):
            q = q_ref[:, a * HEAD_DIM:(a + 1) * HEAD_DIM]
            k = kb[:, a * HEAD_DIM:(a + 1) * HEAD_DIM]
            s = lax.dot_general(q, k, (((1,), (1,)), ((), ())), preferred_element_type=F32)
            if masked:
                row = lax.broadcasted_iota(jnp.int32, (tq, tq), 0)
                col = lax.broadcasted_iota(jnp.int32, (tq, tq), 1)
                s = jnp.where(col <= row, s, -jnp.inf)
            m_prev = m_ref[a]
            m_new = jnp.maximum(m_prev, jnp.max(s, axis=-1, keepdims=True))
            alpha = jnp.exp2(m_prev - m_new)
            p = jnp.exp2(s - jnp.concatenate([m_new] * lanes, axis=1))
            psum = p[:, 0:128]
            for c in range(1, lanes):
                psum = psum + p[:, c * 128:(c + 1) * 128]
            l_ref[a] = alpha * l_ref[a] + psum
            acc_ref[a] = (jnp.concatenate([alpha] * (DIFF_V_DIM // 128), axis=1) * acc_ref[a]
                          + jnp.dot(p.astype(BF16), vb, preferred_element_type=F32))
            m_ref[a] = m_new

    def group(g, carry):
        for u in range(KV_UNROLL):
            block(KV_UNROLL * g + u, False)
        return carry

    groups = i // KV_UNROLL
    lax.fori_loop(0, groups, group, 0)
    rem = i - groups * KV_UNROLL
    for r in range(KV_UNROLL):
        @pl.when(rem == r)
        def _(r=r):
            for u in range(r):
                block(groups * KV_UNROLL + u, False)
            block(i, True)

    lam = (jnp.exp(jnp.sum(lq1_ref[...] * lk1_ref[...], axis=-1, keepdims=True))
           - jnp.exp(jnp.sum(lq2_ref[...] * lk2_ref[...], axis=-1, keepdims=True)) + lam_init)
    l1 = jnp.sum(l_ref[0], axis=-1, keepdims=True)
    l2 = jnp.sum(l_ref[1], axis=-1, keepdims=True)
    o = acc_ref[0] / l1 - lam * (acc_ref[1] / l2)
    o_ref[...] = (_rms(o, sg_ref[...]) * (1.0 - lam_init)).astype(o_ref.dtype)


def _diff_attn(q, k, v, lq1, lk1, lq2, lk2, subln_g, lam_init, tq=512):
    vec = pl.BlockSpec((1, HEAD_DIM), lambda h, i: (0, 0))
    kv_spec = pl.BlockSpec((SEQ, DIFF_V_DIM), lambda h, i: (0, h))
    return pl.pallas_call(
        functools.partial(_diff_attn_kernel, tq=tq, lam_init=lam_init),
        grid=(DIFF_HEADS, SEQ // tq),
        in_specs=[vec, vec, vec, vec, pl.BlockSpec((1, DIFF_V_DIM), lambda h, i: (0, 0)),
                  pl.BlockSpec((tq, DIFF_V_DIM), lambda h, i: (i, h)), kv_spec, kv_spec],
        out_specs=pl.BlockSpec((tq, DIFF_V_DIM), lambda h, i: (i, h)),
        out_shape=jax.ShapeDtypeStruct((SEQ, DIFF_WIDTH), BF16),
        scratch_shapes=[pltpu.VMEM((2, tq, 128), F32), pltpu.VMEM((2, tq, 128), F32),
                        pltpu.VMEM((2, tq, DIFF_V_DIM), F32)],
        compiler_params=_params(("arbitrary", "arbitrary"), 48),
        name="diff_attn",
    )(lq1.reshape(1, -1), lk1.reshape(1, -1), lq2.reshape(1, -1), lk2.reshape(1, -1),
      subln_g.reshape(1, -1), q, k, v)


def _mem_attn_kernel(zm_ref, kv_ref, gq_ref, gk_ref, o_ref):
    scale = MEM_HEAD_DIM ** -0.5
    for h in range(MEM_HEADS):
        lo, hi = h * MEM_HEAD_DIM, (h + 1) * MEM_HEAD_DIM
        q = (_rms(zm_ref[:, lo:hi], gq_ref[...]) * scale).astype(BF16)
        k = _rms(kv_ref[:, lo:hi], gk_ref[...]).astype(BF16)
        v = kv_ref[:, MEM_WIDTH + lo:MEM_WIDTH + hi].astype(BF16)
        s = lax.dot_general(q, k, (((1,), (1,)), ((), ())), preferred_element_type=F32)
        p = jnp.exp(s - jnp.max(s, axis=-1, keepdims=True))
        l = jnp.sum(p, axis=-1, keepdims=True)
        o = jnp.dot(p.astype(BF16), v, preferred_element_type=F32) / l
        o_ref[:, lo:hi] = o.astype(o_ref.dtype)


def _mem_attn(z, kv, gq, gk, tm=512):
    vec = pl.BlockSpec((1, MEM_HEAD_DIM), lambda i: (0, 0))
    return pl.pallas_call(
        _mem_attn_kernel,
        grid=(SEQ // tm,),
        in_specs=[pl.BlockSpec((tm, MEM_WIDTH), lambda i: (i, 0)),
                  pl.BlockSpec((MEM_LEN, 2 * MEM_WIDTH), lambda i: (0, 0)), vec, vec],
        out_specs=pl.BlockSpec((tm, MEM_WIDTH), lambda i: (i, 0)),
        out_shape=jax.ShapeDtypeStruct((SEQ, MEM_WIDTH), BF16),
        compiler_params=_params(("arbitrary",), 40),
        name="mem_attn",
    )(z, kv, gq.reshape(1, -1), gk.reshape(1, -1))


def _router_kernel(x_ref, g_ref, wr_ref, br_ref, h_ref, gate_ref, exp_ref, rank_ref, cnt_ref, carry_ref, *, tm):
    @pl.when(pl.program_id(0) == 0)
    def _():
        carry_ref[...] = jnp.zeros(carry_ref.shape, F32)

    h = _rms(x_ref[...], g_ref[...])
    _store_slabs(h_ref, h, tm)
    w = wr_ref[...]
    h_hi = h.astype(BF16)
    h_lo = (h - h_hi.astype(F32)).astype(BF16)
    w_hi = w.astype(BF16)
    w_lo = (w - w_hi.astype(F32)).astype(BF16)
    logits = (jnp.dot(h_hi, w_hi, preferred_element_type=F32) + jnp.dot(h_hi, w_lo, preferred_element_type=F32)
              + jnp.dot(h_lo, w_hi, preferred_element_type=F32) + br_ref[...])
    lane = lax.broadcasted_iota(jnp.int32, logits.shape, 1)
    slot = lax.broadcasted_iota(jnp.int32, (tm, TOP_K), 1)
    work = logits
    sel = jnp.zeros(logits.shape, jnp.bool_)
    hits = []
    top = None
    nums = jnp.zeros((tm, TOP_K), F32)
    experts = jnp.zeros((tm, TOP_K), jnp.int32)
    denom = jnp.zeros((tm, 1), F32)
    for k in range(TOP_K):
        m = jnp.max(work, axis=-1, keepdims=True)
        idx = jnp.min(jnp.where(work == m, lane, N_EXPERTS), axis=-1, keepdims=True)
        hit = lane == idx
        if k == 0:
            top = m
        e = jnp.exp(m - top)
        nums = jnp.where(slot == k, e, nums)
        experts = jnp.where(slot == k, idx, experts)
        denom = denom + e
        work = jnp.where(hit, -jnp.inf, work)
        sel = jnp.logical_or(sel, hit)
        hits.append(hit)
    gate_ref[...] = nums / denom
    exp_ref[...] = experts
    r = lax.broadcasted_iota(jnp.int32, (tm, tm), 0)
    c = lax.broadcasted_iota(jnp.int32, (tm, tm), 1)
    strict = jnp.where(c < r, 1.0, 0.0).astype(BF16)
    chosen = jnp.where(sel, 1.0, 0.0)
    rank = jnp.dot(strict, chosen.astype(BF16), preferred_element_type=F32) + carry_ref[...]
    ranks = jnp.zeros((tm, TOP_K), F32)
    for k in range(TOP_K):
        ranks = jnp.where(slot == k, jnp.sum(jnp.where(hits[k], rank, 0.0), axis=-1, keepdims=True), ranks)
    rank_ref[...] = ranks.astype(jnp.int32)
    carry_ref[...] = carry_ref[...] + jnp.sum(chosen, axis=0, keepdims=True)
    cnt_ref[...] = carry_ref[...].astype(jnp.int32)


def _router(x, g, w_router, b_router, tm=256):
    small = lambda: pl.BlockSpec((tm, TOP_K), lambda i: (i, 0))
    return pl.pallas_call(
        functools.partial(_router_kernel, tm=tm),
        grid=(SEQ // tm,),
        in_specs=[pl.BlockSpec((tm, D_MODEL), lambda i: (i, 0)), pl.BlockSpec((1, D_MODEL), lambda i: (0, 0)),
                  pl.BlockSpec((D_MODEL, N_EXPERTS), lambda i: (0, 0)), pl.BlockSpec((1, N_EXPERTS), lambda i: (0, 0))],
        out_specs=[pl.BlockSpec((tm * SLAB_ROWS, SLAB_LANES), lambda i: (i, 0)), small(), small(), small(),
                   pl.BlockSpec((1, N_EXPERTS), lambda i: (0, 0))],
        out_shape=[jax.ShapeDtypeStruct((SEQ * SLAB_ROWS, SLAB_LANES), jnp.uint32),
                   jax.ShapeDtypeStruct((SEQ, TOP_K), F32),
                   jax.ShapeDtypeStruct((SEQ, TOP_K), jnp.int32),
                   jax.ShapeDtypeStruct((SEQ, TOP_K), jnp.int32),
                   jax.ShapeDtypeStruct((1, N_EXPERTS), jnp.int32)],
        scratch_shapes=[pltpu.VMEM((1, N_EXPERTS), F32)],
        compiler_params=_params(("arbitrary",), 40),
        name="router",
    )(x, g.reshape(1, -1), w_router, b_router.reshape(1, -1))


DMA_PRIORITIES = 2


def _slot_copy(src_ref, src_row, dst_ref, dst_row, sem):
    return pltpu.make_async_copy(src_ref.at[pl.ds(src_row * SLAB_ROWS, SLAB_ROWS)],
                                 dst_ref.at[pl.ds(dst_row * SLAB_ROWS, SLAB_ROWS)], sem)


def _tile_fill_copy(zero_ref, dst_ref, tile, sem):
    rows = MOE_TILE * SLAB_ROWS
    return pltpu.make_async_copy(zero_ref, dst_ref.at[pl.ds(tile * rows, rows)], sem)


def _dispatch_kernel(pstart_ref, fill_ref, exp_ref, rank_ref, h_hbm, o_ref, slot_ref,
                     zero_ref, buf_a, buf_b, fill_sem, in_sem, out_sem, *, tc):
    i = pl.program_id(0)
    n = pl.num_programs(0)
    rows = tc * SLAB_ROWS

    def load(tile, buf, s):
        return pltpu.make_async_copy(h_hbm.at[pl.ds(tile * rows, rows)], buf, in_sem.at[s])

    def wait_rows(buf, s):
        for _ in range(TOP_K):
            pltpu.make_async_copy(buf, o_ref.at[pl.ds(0, rows)], out_sem.at[s]).wait()

    @pl.when(i == 0)
    def _():
        load(0, buf_a, 0).start()
        zero_ref[...] = jnp.zeros(zero_ref.shape, zero_ref.dtype)

        def fill(t, carry):
            @pl.when(fill_ref[t] != 0)
            def _():
                _tile_fill_copy(zero_ref, o_ref, t, fill_sem).start()
            return carry

        def fill_wait(t, carry):
            @pl.when(fill_ref[t] != 0)
            def _():
                _tile_fill_copy(zero_ref, o_ref, t, fill_sem).wait()
            return carry

        lax.fori_loop(0, MOE_NT, fill, 0)
        lax.fori_loop(0, MOE_NT, fill_wait, 0)

    def step(cur_buf, cs, nxt_buf, ns):
        @pl.when(i > 0)
        def _():
            wait_rows(nxt_buf, ns)

        @pl.when(i + 1 < n)
        def _():
            load(i + 1, nxt_buf, ns).start()

        load(i, cur_buf, cs).wait()

        def issue(r, carry):
            for k in range(TOP_K):
                a = r * TOP_K + k
                slot = pstart_ref[exp_ref[0, 0, a]] + rank_ref[0, 0, a]
                slot_ref[0, 0, a] = slot
                _slot_copy(cur_buf, r, o_ref, slot, out_sem.at[cs]).start(priority=k % DMA_PRIORITIES)
            return carry

        lax.fori_loop(0, tc, issue, 0, unroll=4)

        @pl.when(i == n - 1)
        def _():
            wait_rows(cur_buf, cs)

    @pl.when(i % 2 == 0)
    def _():
        step(buf_a, 0, buf_b, 1)

    @pl.when(i % 2 == 1)
    def _():
        step(buf_b, 1, buf_a, 0)


def _dispatch(hp, experts4, rank4, pstart, fill_tiles, tc=128):
    nb = SEQ // tc
    idx_spec = pl.BlockSpec((1, 1, tc * TOP_K), lambda i, ps, ft: (i, 0, 0), memory_space=pltpu.SMEM)
    return pl.pallas_call(
        functools.partial(_dispatch_kernel, tc=tc),
        grid_spec=pltpu.PrefetchScalarGridSpec(
            num_scalar_prefetch=2,
            grid=(nb,),
            in_specs=[idx_spec, idx_spec, pl.BlockSpec(memory_space=pl.ANY)],
            out_specs=[pl.BlockSpec(memory_space=pl.ANY), idx_spec],
            scratch_shapes=[pltpu.VMEM((MOE_TILE * SLAB_ROWS, SLAB_LANES), jnp.uint32),
                            pltpu.VMEM((tc * SLAB_ROWS, SLAB_LANES), jnp.uint32),
                            pltpu.VMEM((tc * SLAB_ROWS, SLAB_LANES), jnp.uint32),
                            pltpu.SemaphoreType.DMA(()), pltpu.SemaphoreType.DMA((2,)), pltpu.SemaphoreType.DMA((2,))],
        ),
        out_shape=[jax.ShapeDtypeStruct((MOE_ROWS * SLAB_ROWS, SLAB_LANES), jnp.uint32),
                   jax.ShapeDtypeStruct((nb, 1, tc * TOP_K), jnp.int32)],
        compiler_params=_params(("arbitrary",), 40),
        name="moe_dispatch",
    )(pstart, fill_tiles, experts4.reshape(nb, 1, tc * TOP_K), rank4.reshape(nb, 1, tc * TOP_K), hp)


def _group_edges(blk_ref, t):
    e = blk_ref[t]
    first = jnp.logical_or(t == 0, blk_ref[jnp.maximum(t - 1, 0)] != e)
    last = jnp.logical_or(t == MOE_NT - 1, blk_ref[jnp.minimum(t + 1, MOE_NT - 1)] != e)
    return e, first, last


def _stream_weights(blk_ref, nxt_ref, slot_ref, n_pass, copies):
    j = pl.program_id(0)
    t = pl.program_id(1)
    e, first, last = _group_edges(blk_ref, t)

    @pl.when(jnp.logical_and(j == 0, t == 0))
    def _():
        slot_ref[0] = 0
        for c in copies(e, j, 0):
            c.start()

    @pl.when(first)
    def _():
        slot = slot_ref[0]
        for c in copies(e, j, slot):
            c.wait()
        nt = nxt_ref[t]
        same_pass = nt < MOE_NT
        ne = blk_ref[jnp.where(same_pass, nt, 0)]
        nj = jnp.where(same_pass, j, j + 1)

        @pl.when(jnp.logical_or(same_pass, j + 1 < n_pass))
        def _():
            for c in copies(ne, nj, 1 - slot):
                c.start()

    slot = slot_ref[0]

    @pl.when(last)
    def _():
        slot_ref[0] = 1 - slot

    return slot


def _moe_up_kernel(blk_ref, nxt_ref, nused_ref, x_ref, w_hbm, b_ref, o_ref, wbuf, sem, slot_ref, *, tn):
    j = pl.program_id(0)
    t = pl.program_id(1)

    def copies(e, j, slot):
        return [pltpu.make_async_copy(w_hbm.at[e, :, pl.ds(pl.multiple_of(part * D_FF + j * tn, tn), tn)],
                                      wbuf.at[slot, part], sem.at[slot, part]) for part in range(2)]

    slot = _stream_weights(blk_ref, nxt_ref, slot_ref, D_FF // tn, copies)

    @pl.when(t < nused_ref[0])
    def _():
        x = _load_slabs(x_ref, MOE_TILE)

        e = blk_ref[t]

        def proj(part):
            bias = b_ref[pl.ds(e, 1), pl.ds(pl.multiple_of(part * D_FF + j * tn, tn), tn)]
            return jnp.dot(x, wbuf[slot, part].astype(BF16), preferred_element_type=F32) + bias

        gate = jnp.minimum(proj(0), SWIGLU_LIMIT)
        up = jnp.clip(proj(1), -SWIGLU_LIMIT, SWIGLU_LIMIT)
        act = (up + 1.0) * (gate * jax.nn.sigmoid(SWIGLU_ALPHA * gate))
        o_ref[...] = act.astype(o_ref.dtype)

    @pl.when(t >= nused_ref[0])
    def _():
        o_ref[...] = jnp.zeros(o_ref.shape, o_ref.dtype)


def _moe_up(xs, blk_e, nxt, n_used, w_gate_up, b_gate_up, tn=512):
    nj = D_FF // tn
    last = lambda t, nu: jnp.minimum(t, nu[0] - 1)
    return pl.pallas_call(
        functools.partial(_moe_up_kernel, tn=tn),
        grid_spec=pltpu.PrefetchScalarGridSpec(
            num_scalar_prefetch=3,
            grid=(nj, MOE_NT),
            in_specs=[
                pl.BlockSpec((MOE_TILE * SLAB_ROWS, SLAB_LANES), lambda j, t, be, nx, nu: (last(t, nu), 0)),
                pl.BlockSpec(memory_space=pl.ANY),
                pl.BlockSpec((N_EXPERTS, 2 * D_FF), lambda j, t, be, nx, nu: (0, 0)),
            ],
            out_specs=pl.BlockSpec((MOE_TILE, tn), lambda j, t, be, nx, nu: (t, j)),
            scratch_shapes=[pltpu.VMEM((2, 2, D_MODEL, tn), F32), pltpu.SemaphoreType.DMA((2, 2)),
                            pltpu.SMEM((1,), jnp.int32)],
        ),
        out_shape=jax.ShapeDtypeStruct((MOE_ROWS, D_FF), BF16),
        compiler_params=_params(("arbitrary", "arbitrary"), 56),
        name="moe_up",
    )(blk_e, nxt, n_used, xs, w_gate_up, b_gate_up)


DOWN_TN = D_MODEL // 2
DOWN_WORDS = DOWN_TN // 2
DOWN_ROWS = DOWN_WORDS // SLAB_LANES


def _moe_down_kernel(blk_ref, nxt_ref, nused_ref, a_ref, w_hbm, b_ref, o_ref, wbuf, sem, slot_ref):
    t = pl.program_id(1)

    def copies(e, j, slot):
        return [pltpu.make_async_copy(w_hbm.at[e, :, pl.ds(pl.multiple_of(j * DOWN_TN, DOWN_TN), DOWN_TN)],
                                      wbuf.at[slot], sem.at[slot])]

    slot = _stream_weights(blk_ref, nxt_ref, slot_ref, D_MODEL // DOWN_TN, copies)

    @pl.when(t < nused_ref[0])
    def _():
        bias = b_ref[pl.ds(blk_ref[t], 1), pl.ds(pl.multiple_of(pl.program_id(0) * DOWN_TN, DOWN_TN), DOWN_TN)]
        y = jnp.dot(a_ref[...], wbuf[slot].astype(BF16), preferred_element_type=F32) + bias
        words = pltpu.pack_elementwise([y[:, :DOWN_WORDS], y[:, DOWN_WORDS:]], packed_dtype=BF16)
        for s in range(DOWN_ROWS):
            o_ref[:, s, :] = words[:, s * SLAB_LANES:(s + 1) * SLAB_LANES]

    @pl.when(t >= nused_ref[0])
    def _():
        o_ref[...] = jnp.zeros(o_ref.shape, o_ref.dtype)


def _moe_down(act, blk_e, nxt, n_used, w_down, b_down):
    nj = D_MODEL // DOWN_TN
    last = lambda t, nu: jnp.minimum(t, nu[0] - 1)
    return pl.pallas_call(
        _moe_down_kernel,
        grid_spec=pltpu.PrefetchScalarGridSpec(
            num_scalar_prefetch=3,
            grid=(nj, MOE_NT),
            in_specs=[
                pl.BlockSpec((MOE_TILE, D_FF), lambda j, t, be, nx, nu: (last(t, nu), 0)),
                pl.BlockSpec(memory_space=pl.ANY),
                pl.BlockSpec((N_EXPERTS, D_MODEL), lambda j, t, be, nx, nu: (0, 0)),
            ],
            out_specs=pl.BlockSpec((MOE_TILE, DOWN_ROWS, SLAB_LANES), lambda j, t, be, nx, nu: (t, j, 0)),
            scratch_shapes=[pltpu.VMEM((2, D_FF, DOWN_TN), F32), pltpu.SemaphoreType.DMA((2,)),
                            pltpu.SMEM((1,), jnp.int32)],
        ),
        out_shape=jax.ShapeDtypeStruct((MOE_ROWS, SLAB_ROWS, SLAB_LANES), jnp.uint32),
        compiler_params=_params(("arbitrary", "arbitrary"), 56),
        name="moe_down",
    )(blk_e, nxt, n_used, act, w_down, b_down).reshape(MOE_ROWS * SLAB_ROWS, SLAB_LANES)


COMBINE_ROWS = 32


def _combine_kernel(cur_ref, nxt_ref, ys_ref, x_ref, g_ref, o_ref, buf_a, buf_b, sem, *, tc):
    i = pl.program_id(0)

    def fetch(idx_ref, a, buf, slot):
        r, k = divmod(a, TOP_K)
        return _slot_copy(ys_ref, idx_ref[0, 0, a], buf.at[k], r, sem.at[slot])

    def wait_tile(buf, slot):
        for k in range(TOP_K):
            pltpu.make_async_copy(ys_ref.at[pl.ds(0, tc * SLAB_ROWS)], buf.at[k], sem.at[slot]).wait()

    @pl.when(i == 0)
    def _():
        def issue(r, carry):
            for k in range(TOP_K):
                _slot_copy(ys_ref, cur_ref[0, 0, r * TOP_K + k], buf_a.at[k], r, sem.at[0]).start(
                    priority=k % DMA_PRIORITIES)
            return carry

        lax.fori_loop(0, tc, issue, 0, unroll=4)

    def step(cur_buf, cur_slot, nxt_buf, nxt_slot):
        wait_tile(cur_buf, cur_slot)
        n_chunks = (tc // COMBINE_ROWS) * SLAB_ROWS
        per_chunk = tc * TOP_K // n_chunks
        a = 0
        for t0 in range(0, tc, COMBINE_ROWS):
            rows = slice(t0, t0 + COMBINE_ROWS)
            gates = [jnp.broadcast_to(g_ref[rows, k:k + 1], (COMBINE_ROWS, SLAB_LANES)) for k in range(TOP_K)]
            for s in range(SLAB_ROWS):
                for _ in range(per_chunk):
                    fetch(nxt_ref, a, nxt_buf, nxt_slot).start(priority=a % DMA_PRIORITIES)
                    a += 1
                lo_sum = None
                hi_sum = None
                for k in range(TOP_K):
                    words = cur_buf[k, pl.ds(t0 * SLAB_ROWS + s, COMBINE_ROWS, stride=SLAB_ROWS), :]
                    lo, hi = _unpack_bf16(words)
                    lo_sum = gates[k] * lo if lo_sum is None else lo_sum + gates[k] * lo
                    hi_sum = gates[k] * hi if hi_sum is None else hi_sum + gates[k] * hi
                c = (s // DOWN_ROWS) * DOWN_TN + (s % DOWN_ROWS) * SLAB_LANES
                o_ref[rows, c:c + SLAB_LANES] = x_ref[rows, c:c + SLAB_LANES] + lo_sum
                c += DOWN_WORDS
                o_ref[rows, c:c + SLAB_LANES] = x_ref[rows, c:c + SLAB_LANES] + hi_sum

        @pl.when(i == pl.num_programs(0) - 1)
        def _():
            wait_tile(nxt_buf, nxt_slot)

    @pl.when(i % 2 == 0)
    def _():
        step(buf_a, 0, buf_b, 1)

    @pl.when(i % 2 == 1)
    def _():
        step(buf_b, 1, buf_a, 0)


def _combine(ys, x2, slots, gates4, tc=128):
    nb = SEQ // tc
    idx_spec = lambda f: pl.BlockSpec((1, 1, tc * TOP_K), lambda i: (f(i), 0, 0), memory_space=pltpu.SMEM)
    return pl.pallas_call(
        functools.partial(_combine_kernel, tc=tc),
        grid=(nb,),
        in_specs=[idx_spec(lambda i: i), idx_spec(lambda i: jnp.minimum(i + 1, nb - 1)),
                  pl.BlockSpec(memory_space=pl.ANY),
                  pl.BlockSpec((tc, D_MODEL), lambda i: (i, 0)),
                  pl.BlockSpec((tc, TOP_K), lambda i: (i, 0))],
        out_specs=pl.BlockSpec((tc, D_MODEL), lambda i: (i, 0)),
        out_shape=jax.ShapeDtypeStruct((SEQ, D_MODEL), F32),
        scratch_shapes=[pltpu.VMEM((TOP_K, tc * SLAB_ROWS, SLAB_LANES), jnp.uint32),
                        pltpu.VMEM((TOP_K, tc * SLAB_ROWS, SLAB_LANES), jnp.uint32), pltpu.SemaphoreType.DMA((2,))],
        compiler_params=_params(("arbitrary",), 40),
        name="moe_combine",
    )(slots, slots, ys, x2, gates4)


def kernel(x, mem, positions, attn_norm_g, w_in, gmlp_ln_g, gmlp_ln_b, gmlp_w_s, gmlp_b_s, diff_q_norm_g, diff_k_norm_g, diff_lambda_q1, diff_lambda_k1, diff_lambda_q2, diff_lambda_k2, diff_subln_g, mem_norm_g, w_mem_kv, mem_q_norm_g, mem_k_norm_g, w_out, ffn_norm_g, w_router, b_router, w_gate_up, b_gate_up, w_down, b_down):
    depth = attn_norm_g.shape[0]
    xs = x.reshape(SEQ, D_MODEL)
    mem2 = mem.reshape(MEM_LEN, D_MODEL)
    for i in range(depth):
        lam_init = 0.8 - 0.6 * math.exp(-0.3 * i)
        h = _rmsnorm_rows(xs, attn_norm_g[i], tm=256)

        def in_proj(name, col0, width, dtype):
            return _matmul([h], w_in[i], None, dtype, tm=1024, tn=512, name=name, cols=(col0, width))

        z_gqk = in_proj("in_proj_gqk", 0, COL_V, F32)
        v = in_proj("in_proj_v", COL_V, DIFF_WIDTH, BF16)
        z_m = in_proj("in_proj_mem", COL_M, MEM_WIDTH, F32)

        y_g = _gmlp(z_gqk, gmlp_ln_g[i], gmlp_ln_b[i], gmlp_w_s[i], gmlp_b_s[i])

        q, k = _qk_prep(z_gqk, positions, diff_q_norm_g[i], diff_k_norm_g[i])
        y_d = _diff_attn(q, k, v, diff_lambda_q1[i], diff_lambda_k1[i], diff_lambda_q2[i], diff_lambda_k2[i],
                         diff_subln_g[i], lam_init)

        hm = _rmsnorm_rows(mem2, mem_norm_g[i], tm=256)
        kv = _matmul([hm], w_mem_kv[i], None, F32, tm=MEM_LEN, tn=512, name="mem_kv_proj")
        y_m = _mem_attn(z_m, kv, mem_q_norm_g[i], mem_k_norm_g[i])

        x2 = _matmul([y_g, y_d, y_m], w_out[i], xs, F32, tm=1024, tn=512, name="out_proj")

        hp, gates4, experts4, rank4, counts = _router(x2, ffn_norm_g[i], w_router[i], b_router[i])

        counts = counts[0]
        padded = (counts + MOE_TILE - 1) // MOE_TILE * MOE_TILE
        pend = jnp.cumsum(padded)
        pstart = pend - padded
        n_used = (pend[-1] // MOE_TILE).astype(jnp.int32).reshape(1)
        blk_start = jnp.arange(MOE_NT, dtype=jnp.int32) * MOE_TILE
        blk_e = jnp.minimum(jnp.sum(pend[None, :] <= blk_start[:, None], axis=1), N_EXPERTS - 1).astype(jnp.int32)
        blk_end = blk_start + MOE_TILE
        fill_tiles = jnp.logical_or(jnp.any(blk_end[:, None] == pend[None, :], axis=1),
                                    blk_start >= pend[-1]).astype(jnp.int32)

        tile_ids = jnp.arange(MOE_NT, dtype=jnp.int32)
        run_start = jnp.concatenate([jnp.ones((1,), jnp.bool_), blk_e[1:] != blk_e[:-1]])
        starts_from = lax.cummin(jnp.where(run_start, tile_ids, MOE_NT), axis=0, reverse=True)
        nxt = jnp.concatenate([starts_from[1:], jnp.full((1,), MOE_NT, jnp.int32)]).astype(jnp.int32)

        xg, slots = _dispatch(hp, experts4, rank4, pstart, fill_tiles)
        act = _moe_up(xg, blk_e, nxt, n_used, w_gate_up[i], b_gate_up[i])
        ys = _moe_down(act, blk_e, nxt, n_used, w_down[i], b_down[i])
        xs = _combine(ys, x2, slots, gates4)
    return xs.reshape(x.shape)
```

```python
import functools
import math

import jax
import jax.numpy as jnp
from jax import lax
from jax.experimental import pallas as pl
from jax.experimental.pallas import tpu as pltpu

F32 = jnp.float32
BF16 = jnp.bfloat16

D_MODEL = 4096
SEQ = 8192
HEAD_DIM = 128
CHUNK = 128
GMLP_WIDTH = 1024
GMLP_HEADS = 8
DIFF_HEADS = 8
DIFF_V_DIM = 256
DIFF_QK_WIDTH = 2048
DIFF_WIDTH = 2048
MEM_WIDTH = 1024
MEM_HEADS = 4
MEM_HEAD_DIM = 256
MEM_LEN = 256
IN_WIDTH = 9216
N_EXPERTS = 32
TOP_K = 4
D_FF = 1536
SWIGLU_LIMIT = 7.0
SWIGLU_ALPHA = 1.702
ROPE_THETA = 10000.0
EPS = 1e-6

COL_Q = 2 * GMLP_WIDTH
COL_K = COL_Q + DIFF_QK_WIDTH
COL_V = COL_K + DIFF_QK_WIDTH
COL_M = COL_V + DIFF_WIDTH

MOE_TILE = 256
MOE_ROWS = ((SEQ * TOP_K + N_EXPERTS * (MOE_TILE - 1) + MOE_TILE - 1) // MOE_TILE) * MOE_TILE
MOE_NT = MOE_ROWS // MOE_TILE

MIB = 1 << 20


def _params(sem, vmem_mib):
    return pltpu.CompilerParams(dimension_semantics=sem, vmem_limit_bytes=vmem_mib * MIB)


def _rms(x, g):
    ms = jnp.mean(x * x, axis=-1, keepdims=True)
    return x * lax.rsqrt(ms + EPS) * g


SLAB_LANES = 128
SLAB_ROWS = D_MODEL // 2 // SLAB_LANES


def _unpack_bf16(words):
    lo = pltpu.unpack_elementwise(words, index=0, packed_dtype=BF16, unpacked_dtype=F32)
    hi = pltpu.unpack_elementwise(words, index=1, packed_dtype=BF16, unpacked_dtype=F32)
    return lo, hi


def _store_slabs(o_ref, h, rows):
    half = D_MODEL // 2
    for s in range(SLAB_ROWS):
        lo = h[:, s * SLAB_LANES:(s + 1) * SLAB_LANES]
        hi = h[:, half + s * SLAB_LANES:half + (s + 1) * SLAB_LANES]
        o_ref[pl.ds(s, rows, stride=SLAB_ROWS), :] = pltpu.pack_elementwise([lo, hi], packed_dtype=BF16)


def _load_slabs(x_ref, row0, rows):
    los, his = [], []
    for s in range(SLAB_ROWS):
        lo, hi = _unpack_bf16(x_ref[pl.ds(row0 * SLAB_ROWS + s, rows, stride=SLAB_ROWS), :])
        los.append(lo.astype(BF16))
        his.append(hi.astype(BF16))
    return jnp.concatenate(los + his, axis=1)


def _rmsnorm_kernel(x_ref, g_ref, o_ref):
    o_ref[...] = _rms(x_ref[...], g_ref[...]).astype(o_ref.dtype)


def _rmsnorm_rows(x, g, tm):
    m, d = x.shape
    return pl.pallas_call(
        _rmsnorm_kernel,
        grid=(m // tm,),
        in_specs=[pl.BlockSpec((tm, d), lambda i: (i, 0)), pl.BlockSpec((1, d), lambda i: (0, 0))],
        out_specs=pl.BlockSpec((tm, d), lambda i: (i, 0)),
        out_shape=jax.ShapeDtypeStruct((m, d), BF16),
        compiler_params=_params(("arbitrary",), 40),
        name="rmsnorm_rows",
    )(x, g.reshape(1, d))


def _matmul_kernel(*refs, widths, has_res):
    n = len(widths)
    lhs = refs[:n]
    w_ref = refs[n]
    res_ref = refs[n + 1] if has_res else None
    o_ref = refs[n + 1 + has_res]

    acc = None
    off = 0
    for r, kw in zip(lhs, widths):
        part = jnp.dot(r[...], w_ref[off:off + kw, :].astype(BF16), preferred_element_type=F32)
        acc = part if acc is None else acc + part
        off += kw
    if has_res:
        acc = acc + res_ref[...]
    o_ref[...] = acc.astype(o_ref.dtype)


def _matmul(lhs_list, w, res, out_dtype, tm, tn, name, cols=None):
    m = lhs_list[0].shape[0]
    k = w.shape[0]
    col0, n = cols if cols is not None else (0, w.shape[1])
    assert col0 % tn == 0 and n % tn == 0
    widths = tuple(a.shape[1] for a in lhs_list)
    assert sum(widths) == k
    in_specs = [pl.BlockSpec((tm, kw), lambda j, i: (i, 0)) for kw in widths]
    in_specs.append(pl.BlockSpec((k, tn), lambda j, i: (0, j + col0 // tn)))
    args = list(lhs_list) + [w]
    if res is not None:
        in_specs.append(pl.BlockSpec((tm, tn), lambda j, i: (i, j)))
        args.append(res)
    return pl.pallas_call(
        functools.partial(_matmul_kernel, widths=widths, has_res=res is not None),
        grid=(n // tn, m // tm),
        in_specs=in_specs,
        out_specs=pl.BlockSpec((tm, tn), lambda j, i: (i, j)),
        out_shape=jax.ShapeDtypeStruct((m, n), out_dtype),
        compiler_params=_params(("arbitrary", "arbitrary"), 56),
        name=name,
    )(*args)


def _gmlp_kernel(zu_ref, zv_ref, lng_ref, lnb_ref, ws_ref, bst_ref, o_ref, *, tm):
    inv_sqrt2 = 1.0 / math.sqrt(2.0)

    def gelu(x):
        return 0.5 * x * (1.0 + lax.erf(x * inv_sqrt2))

    u = gelu(zu_ref[...])
    v = gelu(zv_ref[...])
    mu = jnp.mean(v, axis=-1, keepdims=True)
    vc = v - mu
    var = jnp.mean(vc * vc, axis=-1, keepdims=True)
    vn = (vc * lax.rsqrt(var + EPS) * lng_ref[...] + lnb_ref[...]).astype(BF16)
    row = lax.broadcasted_iota(jnp.int32, (CHUNK, CHUNK), 0)
    col = lax.broadcasted_iota(jnp.int32, (CHUNK, CHUNK), 1)
    causal = col <= row
    bst = bst_ref[...]
    for h in range(GMLP_HEADS):
        w = jnp.where(causal, ws_ref[h], 0.0).astype(BF16)
        b = bst[:, h:h + 1]
        for c in range(tm // CHUNK):
            vs = vn[c * CHUNK:(c + 1) * CHUNK, h * HEAD_DIM:(h + 1) * HEAD_DIM]
            mixed = jnp.dot(w, vs, preferred_element_type=F32) + b
            o_ref[c * CHUNK:(c + 1) * CHUNK, h * HEAD_DIM:(h + 1) * HEAD_DIM] = (
                u[c * CHUNK:(c + 1) * CHUNK, h * HEAD_DIM:(h + 1) * HEAD_DIM] * mixed).astype(o_ref.dtype)


def _gmlp(z, ln_g, ln_b, w_s, b_s, tm=256):
    return pl.pallas_call(
        functools.partial(_gmlp_kernel, tm=tm),
        grid=(SEQ // tm,),
        in_specs=[
            pl.BlockSpec((tm, GMLP_WIDTH), lambda i: (i, 0)),
            pl.BlockSpec((tm, GMLP_WIDTH), lambda i: (i, 1)),
            pl.BlockSpec((1, GMLP_WIDTH), lambda i: (0, 0)),
            pl.BlockSpec((1, GMLP_WIDTH), lambda i: (0, 0)),
            pl.BlockSpec((GMLP_HEADS, CHUNK, CHUNK), lambda i: (0, 0, 0)),
            pl.BlockSpec((CHUNK, GMLP_HEADS), lambda i: (0, 0)),
        ],
        out_specs=pl.BlockSpec((tm, GMLP_WIDTH), lambda i: (i, 0)),
        out_shape=jax.ShapeDtypeStruct((SEQ, GMLP_WIDTH), BF16),
        compiler_params=_params(("arbitrary",), 40),
        name="gmlp",
    )(z, z, ln_g.reshape(1, -1), ln_b.reshape(1, -1), w_s, b_s.T)


def _qk_prep_kernel(pos_ref, freq_ref, sign_ref, gq_ref, gk_ref, zq_ref, zk_ref, q_ref, k_ref):
    ang = pos_ref[...].astype(F32) * freq_ref[...]
    cos = jnp.cos(ang)
    sin = jnp.sin(ang) * sign_ref[...]
    scale = HEAD_DIM ** -0.5 * math.log2(math.e)

    def prep(z_ref, g_ref, o_ref, mult):
        for s in range(DIFF_QK_WIDTH // HEAD_DIM):
            seg = _rms(z_ref[:, s * HEAD_DIM:(s + 1) * HEAD_DIM], g_ref[...])
            rot = pltpu.roll(seg, HEAD_DIM // 2, 1)
            out = seg * cos + rot * sin
            if mult != 1.0:
                out = out * mult
            o_ref[:, s * HEAD_DIM:(s + 1) * HEAD_DIM] = out.astype(o_ref.dtype)

    prep(zq_ref, gq_ref, q_ref, scale)
    prep(zk_ref, gk_ref, k_ref, 1.0)


def _qk_prep(z, positions, gq, gk, tm=256):
    half = jnp.arange(0, HEAD_DIM, 2, dtype=F32) / HEAD_DIM
    inv_freq = 1.0 / (ROPE_THETA ** half)
    freq = jnp.concatenate([inv_freq, inv_freq]).reshape(1, HEAD_DIM)
    sign = jnp.concatenate([-jnp.ones((HEAD_DIM // 2,), F32), jnp.ones((HEAD_DIM // 2,), F32)]).reshape(1, HEAD_DIM)
    wide = DIFF_QK_WIDTH
    row_spec = lambda cb: pl.BlockSpec((tm, wide), lambda i: (i, cb))
    vec = pl.BlockSpec((1, HEAD_DIM), lambda i: (0, 0))
    out_spec = pl.BlockSpec((tm, wide), lambda i: (i, 0))
    shp = jax.ShapeDtypeStruct((SEQ, wide), BF16)
    return pl.pallas_call(
        _qk_prep_kernel,
        grid=(SEQ // tm,),
        in_specs=[pl.BlockSpec((tm, 1), lambda i: (i, 0)), vec, vec, vec, vec,
                  row_spec(COL_Q // wide), row_spec(COL_K // wide)],
        out_specs=[out_spec, out_spec],
        out_shape=[shp, shp],
        compiler_params=_params(("arbitrary",), 40),
        name="qk_prep",
    )(positions.reshape(SEQ, 1), freq, sign, gq.reshape(1, -1), gk.reshape(1, -1), z, z)


KV_UNROLL = 4
def _diff_attn_kernel(lq1_ref, lk1_ref, lq2_ref, lk2_ref, sg_ref, q_ref, k_ref, v_ref, o_ref,
                      m_ref, l_ref, acc_ref, *, tq, lam_init):
    i = pl.program_id(1)
    m_ref[...] = jnp.full(m_ref.shape, -jnp.inf, F32)
    l_ref[...] = jnp.zeros(l_ref.shape, F32)
    acc_ref[...] = jnp.zeros(acc_ref.shape, F32)
    lanes = tq // 128

    def block(j, masked):
        start = pl.multiple_of(j * tq, tq)
        kb = k_ref[pl.ds(start, tq), :]
        vb = v_ref[pl.ds(start, tq), :]
        for a in range(2):
            q = q_ref[:, a * HEAD_DIM:(a + 1) * HEAD_DIM]
            k = kb[:, a * HEAD_DIM:(a + 1) * HEAD_DIM]
            s = lax.dot_general(q, k, (((1,), (1,)), ((), ())), preferred_element_type=F32)
            if masked:
                row = lax.broadcasted_iota(jnp.int32, (tq, tq), 0)
                col = lax.broadcasted_iota(jnp.int32, (tq, tq), 1)
                s = jnp.where(col <= row, s, -jnp.inf)
            m_prev = m_ref[a]
            m_new = jnp.maximum(m_prev, jnp.max(s, axis=-1, keepdims=True))
            alpha = jnp.exp2(m_prev - m_new)
            p = jnp.exp2(s - jnp.concatenate([m_new] * lanes, axis=1))
            psum = p[:, 0:128]
            for c in range(1, lanes):
                psum = psum + p[:, c * 128:(c + 1) * 128]
            l_ref[a] = alpha * l_ref[a] + psum
            acc_ref[a] = (jnp.concatenate([alpha] * (DIFF_V_DIM // 128), axis=1) * acc_ref[a]
                          + jnp.dot(p.astype(BF16), vb, preferred_element_type=F32))
            m_ref[a] = m_new

    def group(g, carry):
        for u in range(KV_UNROLL):
            block(KV_UNROLL * g + u, False)
        return carry

    groups = i // KV_UNROLL
    lax.fori_loop(0, groups, group, 0)
    rem = i - groups * KV_UNROLL
    for r in range(KV_UNROLL):
        @pl.when(rem == r)
        def _(r=r):
            for u in range(r):
                block(groups * KV_UNROLL + u, False)
            block(i, True)

    lam = (jnp.exp(jnp.sum(lq1_ref[...] * lk1_ref[...], axis=-1, keepdims=True))
           - jnp.exp(jnp.sum(lq2_ref[...] * lk2_ref[...], axis=-1, keepdims=True)) + lam_init)
    l1 = jnp.sum(l_ref[0], axis=-1, keepdims=True)
    l2 = jnp.sum(l_ref[1], axis=-1, keepdims=True)
    o = acc_ref[0] / l1 - lam * (acc_ref[1] / l2)
    o_ref[...] = (_rms(o, sg_ref[...]) * (1.0 - lam_init)).astype(o_ref.dtype)


def _diff_attn(q, k, v, lq1, lk1, lq2, lk2, subln_g, lam_init, tq=512):
    vec = pl.BlockSpec((1, HEAD_DIM), lambda h, i: (0, 0))
    kv_spec = pl.BlockSpec((SEQ, DIFF_V_DIM), lambda h, i: (0, h))
    return pl.pallas_call(
        functools.partial(_diff_attn_kernel, tq=tq, lam_init=lam_init),
        grid=(DIFF_HEADS, SEQ // tq),
        in_specs=[vec, vec, vec, vec, pl.BlockSpec((1, DIFF_V_DIM), lambda h, i: (0, 0)),
                  pl.BlockSpec((tq, DIFF_V_DIM), lambda h, i: (i, h)), kv_spec, kv_spec],
        out_specs=pl.BlockSpec((tq, DIFF_V_DIM), lambda h, i: (i, h)),
        out_shape=jax.ShapeDtypeStruct((SEQ, DIFF_WIDTH), BF16),
        scratch_shapes=[pltpu.VMEM((2, tq, 128), F32), pltpu.VMEM((2, tq, 128), F32),
                        pltpu.VMEM((2, tq, DIFF_V_DIM), F32)],
        compiler_params=_params(("arbitrary", "arbitrary"), 48),
        name="diff_attn",
    )(lq1.reshape(1, -1), lk1.reshape(1, -1), lq2.reshape(1, -1), lk2.reshape(1, -1),
      subln_g.reshape(1, -1), q, k, v)


def _mem_attn_kernel(zm_ref, kv_ref, gq_ref, gk_ref, o_ref):
    scale = MEM_HEAD_DIM ** -0.5
    for h in range(MEM_HEADS):
        lo, hi = h * MEM_HEAD_DIM, (h + 1) * MEM_HEAD_DIM
        q = (_rms(zm_ref[:, lo:hi], gq_ref[...]) * scale).astype(BF16)
        k = _rms(kv_ref[:, lo:hi], gk_ref[...]).astype(BF16)
        v = kv_ref[:, MEM_WIDTH + lo:MEM_WIDTH + hi].astype(BF16)
        s = lax.dot_general(q, k, (((1,), (1,)), ((), ())), preferred_element_type=F32)
        p = jnp.exp(s - jnp.max(s, axis=-1, keepdims=True))
        l = jnp.sum(p, axis=-1, keepdims=True)
        o = jnp.dot(p.astype(BF16), v, preferred_element_type=F32) / l
        o_ref[:, lo:hi] = o.astype(o_ref.dtype)


def _mem_attn(z, kv, gq, gk, tm=512):
    vec = pl.BlockSpec((1, MEM_HEAD_DIM), lambda i: (0, 0))
    return pl.pallas_call(
        _mem_attn_kernel,
        grid=(SEQ // tm,),
        in_specs=[pl.BlockSpec((tm, MEM_WIDTH), lambda i: (i, 0)),
                  pl.BlockSpec((MEM_LEN, 2 * MEM_WIDTH), lambda i: (0, 0)), vec, vec],
        out_specs=pl.BlockSpec((tm, MEM_WIDTH), lambda i: (i, 0)),
        out_shape=jax.ShapeDtypeStruct((SEQ, MEM_WIDTH), BF16),
        compiler_params=_params(("arbitrary",), 40),
        name="mem_attn",
    )(z, kv, gq.reshape(1, -1), gk.reshape(1, -1))


def _router_kernel(x_ref, g_ref, wr_ref, br_ref, h_ref, gate_ref, exp_ref, rank_ref, cnt_ref, carry_ref, *, tm):
    @pl.when(pl.program_id(0) == 0)
    def _():
        carry_ref[...] = jnp.zeros(carry_ref.shape, F32)

    h = _rms(x_ref[...], g_ref[...])
    _store_slabs(h_ref, h, tm)
    w = wr_ref[...]
    h_hi = h.astype(BF16)
    h_lo = (h - h_hi.astype(F32)).astype(BF16)
    w_hi = w.astype(BF16)
    w_lo = (w - w_hi.astype(F32)).astype(BF16)
    logits = (jnp.dot(h_hi, w_hi, preferred_element_type=F32) + jnp.dot(h_hi, w_lo, preferred_element_type=F32)
              + jnp.dot(h_lo, w_hi, preferred_element_type=F32) + br_ref[...])
    lane = lax.broadcasted_iota(jnp.int32, logits.shape, 1)
    slot = lax.broadcasted_iota(jnp.int32, (tm, TOP_K), 1)
    work = logits
    sel = jnp.zeros(logits.shape, jnp.bool_)
    hits = []
    top = None
    nums = jnp.zeros((tm, TOP_K), F32)
    experts = jnp.zeros((tm, TOP_K), jnp.int32)
    denom = jnp.zeros((tm, 1), F32)
    for k in range(TOP_K):
        m = jnp.max(work, axis=-1, keepdims=True)
        idx = jnp.min(jnp.where(work == m, lane, N_EXPERTS), axis=-1, keepdims=True)
        hit = lane == idx
        if k == 0:
            top = m
        e = jnp.exp(m - top)
        nums = jnp.where(slot == k, e, nums)
        experts = jnp.where(slot == k, idx, experts)
        denom = denom + e
        work = jnp.where(hit, -jnp.inf, work)
        sel = jnp.logical_or(sel, hit)
        hits.append(hit)
    gate_ref[...] = nums / denom
    exp_ref[...] = experts
    r = lax.broadcasted_iota(jnp.int32, (tm, tm), 0)
    c = lax.broadcasted_iota(jnp.int32, (tm, tm), 1)
    strict = jnp.where(c < r, 1.0, 0.0).astype(BF16)
    chosen = jnp.where(sel, 1.0, 0.0)
    rank = jnp.dot(strict, chosen.astype(BF16), preferred_element_type=F32) + carry_ref[...]
    ranks = jnp.zeros((tm, TOP_K), F32)
    for k in range(TOP_K):
        ranks = jnp.where(slot == k, jnp.sum(jnp.where(hits[k], rank, 0.0), axis=-1, keepdims=True), ranks)
    rank_ref[...] = ranks.astype(jnp.int32)
    carry_ref[...] = carry_ref[...] + jnp.sum(chosen, axis=0, keepdims=True)
    cnt_ref[...] = carry_ref[...].astype(jnp.int32)


def _router(x, g, w_router, b_router, tm=256):
    small = lambda: pl.BlockSpec((tm, TOP_K), lambda i: (i, 0))
    return pl.pallas_call(
        functools.partial(_router_kernel, tm=tm),
        grid=(SEQ // tm,),
        in_specs=[pl.BlockSpec((tm, D_MODEL), lambda i: (i, 0)), pl.BlockSpec((1, D_MODEL), lambda i: (0, 0)),
                  pl.BlockSpec((D_MODEL, N_EXPERTS), lambda i: (0, 0)), pl.BlockSpec((1, N_EXPERTS), lambda i: (0, 0))],
        out_specs=[pl.BlockSpec((tm * SLAB_ROWS, SLAB_LANES), lambda i: (i, 0)), small(), small(), small(),
                   pl.BlockSpec((1, N_EXPERTS), lambda i: (0, 0))],
        out_shape=[jax.ShapeDtypeStruct((SEQ * SLAB_ROWS, SLAB_LANES), jnp.uint32),
                   jax.ShapeDtypeStruct((SEQ, TOP_K), F32),
                   jax.ShapeDtypeStruct((SEQ, TOP_K), jnp.int32),
                   jax.ShapeDtypeStruct((SEQ, TOP_K), jnp.int32),
                   jax.ShapeDtypeStruct((1, N_EXPERTS), jnp.int32)],
        scratch_shapes=[pltpu.VMEM((1, N_EXPERTS), F32)],
        compiler_params=_params(("arbitrary",), 40),
        name="router",
    )(x, g.reshape(1, -1), w_router, b_router.reshape(1, -1))


DMA_PRIORITIES = 2


def _slot_copy(src_ref, src_row, dst_ref, dst_row, sem):
    return pltpu.make_async_copy(src_ref.at[pl.ds(src_row * SLAB_ROWS, SLAB_ROWS)],
                                 dst_ref.at[pl.ds(dst_row * SLAB_ROWS, SLAB_ROWS)], sem)


def _tile_fill_copy(zero_ref, dst_ref, tile, sem):
    rows = MOE_TILE * SLAB_ROWS
    return pltpu.make_async_copy(zero_ref, dst_ref.at[pl.ds(tile * rows, rows)], sem)


def _dispatch_kernel(pstart_ref, fill_ref, exp_ref, rank_ref, h_hbm, o_ref, slot_ref,
                     zero_ref, buf_a, buf_b, fill_sem, in_sem, out_sem, *, tc):
    i = pl.program_id(0)
    n = pl.num_programs(0)
    rows = tc * SLAB_ROWS

    def load(tile, buf, s):
        return pltpu.make_async_copy(h_hbm.at[pl.ds(tile * rows, rows)], buf, in_sem.at[s])

    def wait_rows(buf, s):
        for _ in range(TOP_K):
            pltpu.make_async_copy(buf, o_ref.at[pl.ds(0, rows)], out_sem.at[s]).wait()

    @pl.when(i == 0)
    def _():
        load(0, buf_a, 0).start()
        zero_ref[...] = jnp.zeros(zero_ref.shape, zero_ref.dtype)

        def fill(t, carry):
            @pl.when(fill_ref[t] != 0)
            def _():
                _tile_fill_copy(zero_ref, o_ref, t, fill_sem).start()
            return carry

        def fill_wait(t, carry):
            @pl.when(fill_ref[t] != 0)
            def _():
                _tile_fill_copy(zero_ref, o_ref, t, fill_sem).wait()
            return carry

        lax.fori_loop(0, MOE_NT, fill, 0)
        lax.fori_loop(0, MOE_NT, fill_wait, 0)

    def step(cur_buf, cs, nxt_buf, ns):
        @pl.when(i > 0)
        def _():
            wait_rows(nxt_buf, ns)

        @pl.when(i + 1 < n)
        def _():
            load(i + 1, nxt_buf, ns).start()

        load(i, cur_buf, cs).wait()

        def issue(r, carry):
            for k in range(TOP_K):
                a = r * TOP_K + k
                slot = pstart_ref[exp_ref[0, 0, a]] + rank_ref[0, 0, a]
                slot_ref[0, 0, a] = slot
                _slot_copy(cur_buf, r, o_ref, slot, out_sem.at[cs]).start(priority=k % DMA_PRIORITIES)
            return carry

        lax.fori_loop(0, tc, issue, 0, unroll=4)

        @pl.when(i == n - 1)
        def _():
            wait_rows(cur_buf, cs)

    @pl.when(i % 2 == 0)
    def _():
        step(buf_a, 0, buf_b, 1)

    @pl.when(i % 2 == 1)
    def _():
        step(buf_b, 1, buf_a, 0)


def _dispatch(hp, experts4, rank4, pstart, fill_tiles, tc=128):
    nb = SEQ // tc
    idx_spec = pl.BlockSpec((1, 1, tc * TOP_K), lambda i, ps, ft: (i, 0, 0), memory_space=pltpu.SMEM)
    return pl.pallas_call(
        functools.partial(_dispatch_kernel, tc=tc),
        grid_spec=pltpu.PrefetchScalarGridSpec(
            num_scalar_prefetch=2,
            grid=(nb,),
            in_specs=[idx_spec, idx_spec, pl.BlockSpec(memory_space=pl.ANY)],
            out_specs=[pl.BlockSpec(memory_space=pl.ANY), idx_spec],
            scratch_shapes=[pltpu.VMEM((MOE_TILE * SLAB_ROWS, SLAB_LANES), jnp.uint32),
                            pltpu.VMEM((tc * SLAB_ROWS, SLAB_LANES), jnp.uint32),
                            pltpu.VMEM((tc * SLAB_ROWS, SLAB_LANES), jnp.uint32),
                            pltpu.SemaphoreType.DMA(()), pltpu.SemaphoreType.DMA((2,)), pltpu.SemaphoreType.DMA((2,))],
        ),
        out_shape=[jax.ShapeDtypeStruct((MOE_ROWS * SLAB_ROWS, SLAB_LANES), jnp.uint32),
                   jax.ShapeDtypeStruct((nb, 1, tc * TOP_K), jnp.int32)],
        compiler_params=_params(("arbitrary",), 40),
        name="moe_dispatch",
    )(pstart, fill_tiles, experts4.reshape(nb, 1, tc * TOP_K), rank4.reshape(nb, 1, tc * TOP_K), hp)


def _group_edges(blk_ref, t):
    e = blk_ref[t]
    first = jnp.logical_or(t == 0, blk_ref[jnp.maximum(t - 1, 0)] != e)
    last = jnp.logical_or(t == MOE_NT - 1, blk_ref[jnp.minimum(t + 1, MOE_NT - 1)] != e)
    return e, first, last


def _stream_weights(blk_ref, nxt_ref, slot_ref, n_pass, copies, t):
    j = pl.program_id(0)
    e, first, last = _group_edges(blk_ref, t)

    @pl.when(jnp.logical_and(j == 0, t == 0))
    def _():
        slot_ref[0] = 0
        for c in copies(e, j, 0):
            c.start()

    @pl.when(first)
    def _():
        slot = slot_ref[0]
        for c in copies(e, j, slot):
            c.wait()
        nt = nxt_ref[t]
        same_pass = nt < MOE_NT
        ne = blk_ref[jnp.where(same_pass, nt, 0)]
        nj = jnp.where(same_pass, j, j + 1)

        @pl.when(jnp.logical_or(same_pass, j + 1 < n_pass))
        def _():
            for c in copies(ne, nj, 1 - slot):
                c.start()

    slot = slot_ref[0]

    @pl.when(last)
    def _():
        slot_ref[0] = 1 - slot

    return slot


def _pair_step(blk_ref, nxt_ref, nused_ref, slot_ref, n_pass, copies, compute, clear):
    ta = 2 * pl.program_id(1)
    tb = ta + 1
    ea = blk_ref[ta]
    eb = blk_ref[tb]
    va = ta < nused_ref[0]
    vb = tb < nused_ref[0]
    merged = jnp.logical_and(vb, ea == eb)

    slot_a = _stream_weights(blk_ref, nxt_ref, slot_ref, n_pass, copies, ta)

    @pl.when(merged)
    def _():
        compute(0, 2 * MOE_TILE, ea, slot_a)

    @pl.when(jnp.logical_and(va, jnp.logical_not(merged)))
    def _():
        compute(0, MOE_TILE, ea, slot_a)

    slot_b = _stream_weights(blk_ref, nxt_ref, slot_ref, n_pass, copies, tb)

    @pl.when(jnp.logical_and(vb, jnp.logical_not(merged)))
    def _():
        compute(MOE_TILE, MOE_TILE, eb, slot_b)

    @pl.when(jnp.logical_not(va))
    def _():
        clear(0, MOE_TILE)

    @pl.when(jnp.logical_not(vb))
    def _():
        clear(MOE_TILE, MOE_TILE)


def _moe_up_kernel(blk_ref, nxt_ref, nused_ref, x_ref, w_hbm, b_ref, o_ref, wbuf, sem, slot_ref, *, tn):
    j = pl.program_id(0)

    def copies(e, j, slot):
        return [pltpu.make_async_copy(w_hbm.at[e, :, pl.ds(pl.multiple_of(part * D_FF + j * tn, tn), tn)],
                                      wbuf.at[slot, part], sem.at[slot, part]) for part in range(2)]

    def compute(row0, rows, e, slot):
        x = _load_slabs(x_ref, row0, rows)

        def proj(part):
            bias = b_ref[pl.ds(e, 1), pl.ds(pl.multiple_of(part * D_FF + j * tn, tn), tn)]
            return jnp.dot(x, wbuf[slot, part].astype(BF16), preferred_element_type=F32) + bias

        gate = jnp.minimum(proj(0), SWIGLU_LIMIT)
        up = jnp.clip(proj(1), -SWIGLU_LIMIT, SWIGLU_LIMIT)
        act = (up + 1.0) * (gate * jax.nn.sigmoid(SWIGLU_ALPHA * gate))
        o_ref[row0:row0 + rows, :] = act.astype(o_ref.dtype)

    def clear(row0, rows):
        o_ref[row0:row0 + rows, :] = jnp.zeros((rows, tn), o_ref.dtype)

    _pair_step(blk_ref, nxt_ref, nused_ref, slot_ref, D_FF // tn, copies, compute, clear)


def _moe_up(xs, blk_e, nxt, n_used, w_gate_up, b_gate_up, tn=512):
    nj = D_FF // tn
    last = lambda u, nu: jnp.minimum(u, (nu[0] - 1) // 2)
    return pl.pallas_call(
        functools.partial(_moe_up_kernel, tn=tn),
        grid_spec=pltpu.PrefetchScalarGridSpec(
            num_scalar_prefetch=3,
            grid=(nj, MOE_NT // 2),
            in_specs=[
                pl.BlockSpec((2 * MOE_TILE * SLAB_ROWS, SLAB_LANES), lambda j, u, be, nx, nu: (last(u, nu), 0)),
                pl.BlockSpec(memory_space=pl.ANY),
                pl.BlockSpec((N_EXPERTS, 2 * D_FF), lambda j, u, be, nx, nu: (0, 0)),
            ],
            out_specs=pl.BlockSpec((2 * MOE_TILE, tn), lambda j, u, be, nx, nu: (u, j)),
            scratch_shapes=[pltpu.VMEM((2, 2, D_MODEL, tn), F32), pltpu.SemaphoreType.DMA((2, 2)),
                            pltpu.SMEM((1,), jnp.int32)],
        ),
        out_shape=jax.ShapeDtypeStruct((MOE_ROWS, D_FF), BF16),
        compiler_params=_params(("arbitrary", "arbitrary"), 56),
        name="moe_up",
    )(blk_e, nxt, n_used, xs, w_gate_up, b_gate_up)


DOWN_TN = D_MODEL // 2
DOWN_WORDS = DOWN_TN // 2
DOWN_ROWS = DOWN_WORDS // SLAB_LANES


def _moe_down_kernel(blk_ref, nxt_ref, nused_ref, a_ref, w_hbm, b_ref, o_ref, wbuf, sem, slot_ref):
    def copies(e, j, slot):
        return [pltpu.make_async_copy(w_hbm.at[e, :, pl.ds(pl.multiple_of(j * DOWN_TN, DOWN_TN), DOWN_TN)],
                                      wbuf.at[slot], sem.at[slot])]

    def compute(row0, rows, e, slot):
        bias = b_ref[pl.ds(e, 1), pl.ds(pl.multiple_of(pl.program_id(0) * DOWN_TN, DOWN_TN), DOWN_TN)]
        y = jnp.dot(a_ref[row0:row0 + rows, :], wbuf[slot].astype(BF16), preferred_element_type=F32) + bias
        words = pltpu.pack_elementwise([y[:, :DOWN_WORDS], y[:, DOWN_WORDS:]], packed_dtype=BF16)
        for s in range(DOWN_ROWS):
            o_ref[row0:row0 + rows, s, :] = words[:, s * SLAB_LANES:(s + 1) * SLAB_LANES]

    def clear(row0, rows):
        o_ref[row0:row0 + rows] = jnp.zeros((rows, DOWN_ROWS, SLAB_LANES), o_ref.dtype)

    _pair_step(blk_ref, nxt_ref, nused_ref, slot_ref, D_MODEL // DOWN_TN, copies, compute, clear)


def _moe_down(act, blk_e, nxt, n_used, w_down, b_down):
    nj = D_MODEL // DOWN_TN
    last = lambda u, nu: jnp.minimum(u, (nu[0] - 1) // 2)
    return pl.pallas_call(
        _moe_down_kernel,
        grid_spec=pltpu.PrefetchScalarGridSpec(
            num_scalar_prefetch=3,
            grid=(nj, MOE_NT // 2),
            in_specs=[
                pl.BlockSpec((2 * MOE_TILE, D_FF), lambda j, u, be, nx, nu: (last(u, nu), 0)),
                pl.BlockSpec(memory_space=pl.ANY),
                pl.BlockSpec((N_EXPERTS, D_MODEL), lambda j, u, be, nx, nu: (0, 0)),
            ],
            out_specs=pl.BlockSpec((2 * MOE_TILE, DOWN_ROWS, SLAB_LANES), lambda j, u, be, nx, nu: (u, j, 0)),
            scratch_shapes=[pltpu.VMEM((2, D_FF, DOWN_TN), F32), pltpu.SemaphoreType.DMA((2,)),
                            pltpu.SMEM((1,), jnp.int32)],
        ),
        out_shape=jax.ShapeDtypeStruct((MOE_ROWS, SLAB_ROWS, SLAB_LANES), jnp.uint32),
        compiler_params=_params(("arbitrary", "arbitrary"), 56),
        name="moe_down",
    )(blk_e, nxt, n_used, act, w_down, b_down).reshape(MOE_ROWS * SLAB_ROWS, SLAB_LANES)


COMBINE_ROWS = 32


def _combine_kernel(cur_ref, nxt_ref, ys_ref, x_ref, g_ref, o_ref, buf_a, buf_b, sem, *, tc):
    i = pl.program_id(0)

    def fetch(idx_ref, a, buf, slot):
        r, k = divmod(a, TOP_K)
        return _slot_copy(ys_ref, idx_ref[0, 0, a], buf.at[k], r, sem.at[slot])

    def wait_tile(buf, slot):
        for k in range(TOP_K):
            pltpu.make_async_copy(ys_ref.at[pl.ds(0, tc * SLAB_ROWS)], buf.at[k], sem.at[slot]).wait()

    @pl.when(i == 0)
    def _():
        def issue(r, carry):
            for k in range(TOP_K):
                _slot_copy(ys_ref, cur_ref[0, 0, r * TOP_K + k], buf_a.at[k], r, sem.at[0]).start(
                    priority=k % DMA_PRIORITIES)
            return carry

        lax.fori_loop(0, tc, issue, 0, unroll=4)

    def step(cur_buf, cur_slot, nxt_buf, nxt_slot):
        wait_tile(cur_buf, cur_slot)
        n_chunks = (tc // COMBINE_ROWS) * SLAB_ROWS
        per_chunk = tc * TOP_K // n_chunks
        a = 0
        for t0 in range(0, tc, COMBINE_ROWS):
            rows = slice(t0, t0 + COMBINE_ROWS)
            gates = [jnp.broadcast_to(g_ref[rows, k:k + 1], (COMBINE_ROWS, SLAB_LANES)) for k in range(TOP_K)]
            for s in range(SLAB_ROWS):
                for _ in range(per_chunk):
                    fetch(nxt_ref, a, nxt_buf, nxt_slot).start(priority=a % DMA_PRIORITIES)
                    a += 1
                lo_sum = None
                hi_sum = None
                for k in range(TOP_K):
                    words = cur_buf[k, pl.ds(t0 * SLAB_ROWS + s, COMBINE_ROWS, stride=SLAB_ROWS), :]
                    lo, hi = _unpack_bf16(words)
                    lo_sum = gates[k] * lo if lo_sum is None else lo_sum + gates[k] * lo
                    hi_sum = gates[k] * hi if hi_sum is None else hi_sum + gates[k] * hi
                c = (s // DOWN_ROWS) * DOWN_TN + (s % DOWN_ROWS) * SLAB_LANES
                o_ref[rows, c:c + SLAB_LANES] = x_ref[rows, c:c + SLAB_LANES] + lo_sum
                c += DOWN_WORDS
                o_ref[rows, c:c + SLAB_LANES] = x_ref[rows, c:c + SLAB_LANES] + hi_sum

        @pl.when(i == pl.num_programs(0) - 1)
        def _():
            wait_tile(nxt_buf, nxt_slot)

    @pl.when(i % 2 == 0)
    def _():
        step(buf_a, 0, buf_b, 1)

    @pl.when(i % 2 == 1)
    def _():
        step(buf_b, 1, buf_a, 0)


def _combine(ys, x2, slots, gates4, tc=128):
    nb = SEQ // tc
    idx_spec = lambda f: pl.BlockSpec((1, 1, tc * TOP_K), lambda i: (f(i), 0, 0), memory_space=pltpu.SMEM)
    return pl.pallas_call(
        functools.partial(_combine_kernel, tc=tc),
        grid=(nb,),
        in_specs=[idx_spec(lambda i: i), idx_spec(lambda i: jnp.minimum(i + 1, nb - 1)),
                  pl.BlockSpec(memory_space=pl.ANY),
                  pl.BlockSpec((tc, D_MODEL), lambda i: (i, 0)),
                  pl.BlockSpec((tc, TOP_K), lambda i: (i, 0))],
        out_specs=pl.BlockSpec((tc, D_MODEL), lambda i: (i, 0)),
        out_shape=jax.ShapeDtypeStruct((SEQ, D_MODEL), F32),
        scratch_shapes=[pltpu.VMEM((TOP_K, tc * SLAB_ROWS, SLAB_LANES), jnp.uint32),
                        pltpu.VMEM((TOP_K, tc * SLAB_ROWS, SLAB_LANES), jnp.uint32), pltpu.SemaphoreType.DMA((2,))],
        compiler_params=_params(("arbitrary",), 40),
        name="moe_combine",
    )(slots, slots, ys, x2, gates4)


def kernel(x, mem, positions, attn_norm_g, w_in, gmlp_ln_g, gmlp_ln_b, gmlp_w_s, gmlp_b_s, diff_q_norm_g, diff_k_norm_g, diff_lambda_q1, diff_lambda_k1, diff_lambda_q2, diff_lambda_k2, diff_subln_g, mem_norm_g, w_mem_kv, mem_q_norm_g, mem_k_norm_g, w_out, ffn_norm_g, w_router, b_router, w_gate_up, b_gate_up, w_down, b_down):
    depth = attn_norm_g.shape[0]
    xs = x.reshape(SEQ, D_MODEL)
    mem2 = mem.reshape(MEM_LEN, D_MODEL)
    for i in range(depth):
        lam_init = 0.8 - 0.6 * math.exp(-0.3 * i)
        h = _rmsnorm_rows(xs, attn_norm_g[i], tm=256)

        def in_proj(name, col0, width, dtype):
            return _matmul([h], w_in[i], None, dtype, tm=1024, tn=512, name=name, cols=(col0, width))

        z_gqk = in_proj("in_proj_gqk", 0, COL_V, F32)
        v = in_proj("in_proj_v", COL_V, DIFF_WIDTH, BF16)
        z_m = in_proj("in_proj_mem", COL_M, MEM_WIDTH, F32)

        y_g = _gmlp(z_gqk, gmlp_ln_g[i], gmlp_ln_b[i], gmlp_w_s[i], gmlp_b_s[i])

        q, k = _qk_prep(z_gqk, positions, diff_q_norm_g[i], diff_k_norm_g[i])
        y_d = _diff_attn(q, k, v, diff_lambda_q1[i], diff_lambda_k1[i], diff_lambda_q2[i], diff_lambda_k2[i],
                         diff_subln_g[i], lam_init)

        hm = _rmsnorm_rows(mem2, mem_norm_g[i], tm=256)
        kv = _matmul([hm], w_mem_kv[i], None, F32, tm=MEM_LEN, tn=512, name="mem_kv_proj")
        y_m = _mem_attn(z_m, kv, mem_q_norm_g[i], mem_k_norm_g[i])

        x2 = _matmul([y_g, y_d, y_m], w_out[i], xs, F32, tm=1024, tn=512, name="out_proj")

        hp, gates4, experts4, rank4, counts = _router(x2, ffn_norm_g[i], w_router[i], b_router[i])

        counts = counts[0]
        padded = (counts + MOE_TILE - 1) // MOE_TILE * MOE_TILE
        pend = jnp.cumsum(padded)
        pstart = pend - padded
        n_used = (pend[-1] // MOE_TILE).astype(jnp.int32).reshape(1)
        blk_start = jnp.arange(MOE_NT, dtype=jnp.int32) * MOE_TILE
        blk_e = jnp.minimum(jnp.sum(pend[None, :] <= blk_start[:, None], axis=1), N_EXPERTS - 1).astype(jnp.int32)
        blk_end = blk_start + MOE_TILE
        fill_tiles = jnp.logical_or(jnp.any(blk_end[:, None] == pend[None, :], axis=1),
                                    blk_start >= pend[-1]).astype(jnp.int32)

        tile_ids = jnp.arange(MOE_NT, dtype=jnp.int32)
        run_start = jnp.concatenate([jnp.ones((1,), jnp.bool_), blk_e[1:] != blk_e[:-1]])
        starts_from = lax.cummin(jnp.where(run_start, tile_ids, MOE_NT), axis=0, reverse=True)
        nxt = jnp.concatenate([starts_from[1:], jnp.full((1,), MOE_NT, jnp.int32)]).astype(jnp.int32)

        xg, slots = _dispatch(hp, experts4, rank4, pstart, fill_tiles)
        act = _moe_up(xg, blk_e, nxt, n_used, w_gate_up[i], b_gate_up[i])
        ys = _moe_down(act, blk_e, nxt, n_used, w_down[i], b_down[i])
        xs = _combine(ys, x2, slots, gates4)
    return xs.reshape(x.shape)
```

```python
import functools
import math

import jax
import jax.numpy as jnp
from jax import lax
from jax.experimental import pallas as pl
from jax.experimental.pallas import tpu as pltpu

F32 = jnp.float32
BF16 = jnp.bfloat16

D_MODEL = 4096
SEQ = 8192
HEAD_DIM = 128
CHUNK = 128
GMLP_WIDTH = 1024
GMLP_HEADS = 8
DIFF_HEADS = 8
DIFF_V_DIM = 256
DIFF_QK_WIDTH = 2048
DIFF_WIDTH = 2048
MEM_WIDTH = 1024
MEM_HEADS = 4
MEM_HEAD_DIM = 256
MEM_LEN = 256
IN_WIDTH = 9216
N_EXPERTS = 32
TOP_K = 4
D_FF = 1536
SWIGLU_LIMIT = 7.0
SWIGLU_ALPHA = 1.702
ROPE_THETA = 10000.0
EPS = 1e-6

COL_Q = 2 * GMLP_WIDTH
COL_K = COL_Q + DIFF_QK_WIDTH
COL_V = COL_K + DIFF_QK_WIDTH
COL_M = COL_V + DIFF_WIDTH

MOE_TILE = 256
MOE_ROWS = ((SEQ * TOP_K + N_EXPERTS * (MOE_TILE - 1) + MOE_TILE - 1) // MOE_TILE) * MOE_TILE
MOE_NT = MOE_ROWS // MOE_TILE

MIB = 1 << 20


def _params(sem, vmem_mib):
    return pltpu.CompilerParams(dimension_semantics=sem, vmem_limit_bytes=vmem_mib * MIB)


def _rms(x, g):
    ms = jnp.mean(x * x, axis=-1, keepdims=True)
    return x * lax.rsqrt(ms + EPS) * g


SLAB_LANES = 128
SLAB_ROWS = D_MODEL // 2 // SLAB_LANES


def _unpack_bf16(words):
    lo = pltpu.unpack_elementwise(words, index=0, packed_dtype=BF16, unpacked_dtype=F32)
    hi = pltpu.unpack_elementwise(words, index=1, packed_dtype=BF16, unpacked_dtype=F32)
    return lo, hi


def _store_slabs(o_ref, h, rows):
    half = D_MODEL // 2
    for s in range(SLAB_ROWS):
        lo = h[:, s * SLAB_LANES:(s + 1) * SLAB_LANES]
        hi = h[:, half + s * SLAB_LANES:half + (s + 1) * SLAB_LANES]
        o_ref[pl.ds(s, rows, stride=SLAB_ROWS), :] = pltpu.pack_elementwise([lo, hi], packed_dtype=BF16)


def _load_slabs(x_ref, row0, rows):
    los, his = [], []
    for s in range(SLAB_ROWS):
        lo, hi = _unpack_bf16(x_ref[pl.ds(row0 * SLAB_ROWS + s, rows, stride=SLAB_ROWS), :])
        los.append(lo.astype(BF16))
        his.append(hi.astype(BF16))
    return jnp.concatenate(los + his, axis=1)


def _rmsnorm_kernel(x_ref, g_ref, o_ref):
    o_ref[...] = _rms(x_ref[...], g_ref[...]).astype(o_ref.dtype)


def _rmsnorm_rows(x, g, tm):
    m, d = x.shape
    return pl.pallas_call(
        _rmsnorm_kernel,
        grid=(m // tm,),
        in_specs=[pl.BlockSpec((tm, d), lambda i: (i, 0)), pl.BlockSpec((1, d), lambda i: (0, 0))],
        out_specs=pl.BlockSpec((tm, d), lambda i: (i, 0)),
        out_shape=jax.ShapeDtypeStruct((m, d), BF16),
        compiler_params=_params(("arbitrary",), 40),
        name="rmsnorm_rows",
    )(x, g.reshape(1, d))


def _matmul_kernel(*refs, widths, has_res):
    n = len(widths)
    lhs = refs[:n]
    w_ref = refs[n]
    res_ref = refs[n + 1] if has_res else None
    o_ref = refs[n + 1 + has_res]

    acc = None
    off = 0
    for r, kw in zip(lhs, widths):
        part = jnp.dot(r[...], w_ref[off:off + kw, :].astype(BF16), preferred_element_type=F32)
        acc = part if acc is None else acc + part
        off += kw
    if has_res:
        acc = acc + res_ref[...]
    o_ref[...] = acc.astype(o_ref.dtype)


def _matmul(lhs_list, w, res, out_dtype, tm, tn, name, cols=None):
    m = lhs_list[0].shape[0]
    k = w.shape[0]
    col0, n = cols if cols is not None else (0, w.shape[1])
    assert col0 % tn == 0 and n % tn == 0
    widths = tuple(a.shape[1] for a in lhs_list)
    assert sum(widths) == k
    in_specs = [pl.BlockSpec((tm, kw), lambda j, i: (i, 0)) for kw in widths]
    in_specs.append(pl.BlockSpec((k, tn), lambda j, i: (0, j + col0 // tn)))
    args = list(lhs_list) + [w]
    if res is not None:
        in_specs.append(pl.BlockSpec((tm, tn), lambda j, i: (i, j)))
        args.append(res)
    return pl.pallas_call(
        functools.partial(_matmul_kernel, widths=widths, has_res=res is not None),
        grid=(n // tn, m // tm),
        in_specs=in_specs,
        out_specs=pl.BlockSpec((tm, tn), lambda j, i: (i, j)),
        out_shape=jax.ShapeDtypeStruct((m, n), out_dtype),
        compiler_params=_params(("arbitrary", "arbitrary"), 56),
        name=name,
    )(*args)


def _gmlp_kernel(zu_ref, zv_ref, lng_ref, lnb_ref, ws_ref, bst_ref, o_ref, *, tm):
    inv_sqrt2 = 1.0 / math.sqrt(2.0)

    def gelu(x):
        return 0.5 * x * (1.0 + lax.erf(x * inv_sqrt2))

    u = gelu(zu_ref[...])
    v = gelu(zv_ref[...])
    mu = jnp.mean(v, axis=-1, keepdims=True)
    vc = v - mu
    var = jnp.mean(vc * vc, axis=-1, keepdims=True)
    vn = (vc * lax.rsqrt(var + EPS) * lng_ref[...] + lnb_ref[...]).astype(BF16)
    row = lax.broadcasted_iota(jnp.int32, (CHUNK, CHUNK), 0)
    col = lax.broadcasted_iota(jnp.int32, (CHUNK, CHUNK), 1)
    causal = col <= row
    bst = bst_ref[...]
    for h in range(GMLP_HEADS):
        w = jnp.where(causal, ws_ref[h], 0.0).astype(BF16)
        b = bst[:, h:h + 1]
        for c in range(tm // CHUNK):
            vs = vn[c * CHUNK:(c + 1) * CHUNK, h * HEAD_DIM:(h + 1) * HEAD_DIM]
            mixed = jnp.dot(w, vs, preferred_element_type=F32) + b
            o_ref[c * CHUNK:(c + 1) * CHUNK, h * HEAD_DIM:(h + 1) * HEAD_DIM] = (
                u[c * CHUNK:(c + 1) * CHUNK, h * HEAD_DIM:(h + 1) * HEAD_DIM] * mixed).astype(o_ref.dtype)


def _gmlp(z, ln_g, ln_b, w_s, b_s, tm=256):
    return pl.pallas_call(
        functools.partial(_gmlp_kernel, tm=tm),
        grid=(SEQ // tm,),
        in_specs=[
            pl.BlockSpec((tm, GMLP_WIDTH), lambda i: (i, 0)),
            pl.BlockSpec((tm, GMLP_WIDTH), lambda i: (i, 1)),
            pl.BlockSpec((1, GMLP_WIDTH), lambda i: (0, 0)),
            pl.BlockSpec((1, GMLP_WIDTH), lambda i: (0, 0)),
            pl.BlockSpec((GMLP_HEADS, CHUNK, CHUNK), lambda i: (0, 0, 0)),
            pl.BlockSpec((CHUNK, GMLP_HEADS), lambda i: (0, 0)),
        ],
        out_specs=pl.BlockSpec((tm, GMLP_WIDTH), lambda i: (i, 0)),
        out_shape=jax.ShapeDtypeStruct((SEQ, GMLP_WIDTH), BF16),
        compiler_params=_params(("arbitrary",), 40),
        name="gmlp",
    )(z, z, ln_g.reshape(1, -1), ln_b.reshape(1, -1), w_s, b_s.T)


def _qk_prep_kernel(pos_ref, freq_ref, sign_ref, gq_ref, gk_ref, zq_ref, zk_ref, q_ref, k_ref):
    ang = pos_ref[...].astype(F32) * freq_ref[...]
    cos = jnp.cos(ang)
    sin = jnp.sin(ang) * sign_ref[...]
    scale = HEAD_DIM ** -0.5 * math.log2(math.e)

    def prep(z_ref, g_ref, o_ref, mult):
        for s in range(DIFF_QK_WIDTH // HEAD_DIM):
            seg = _rms(z_ref[:, s * HEAD_DIM:(s + 1) * HEAD_DIM], g_ref[...])
            rot = pltpu.roll(seg, HEAD_DIM // 2, 1)
            out = seg * cos + rot * sin
            if mult != 1.0:
                out = out * mult
            o_ref[:, s * HEAD_DIM:(s + 1) * HEAD_DIM] = out.astype(o_ref.dtype)

    prep(zq_ref, gq_ref, q_ref, scale)
    prep(zk_ref, gk_ref, k_ref, 1.0)


def _qk_prep(z, positions, gq, gk, tm=256):
    half = jnp.arange(0, HEAD_DIM, 2, dtype=F32) / HEAD_DIM
    inv_freq = 1.0 / (ROPE_THETA ** half)
    freq = jnp.concatenate([inv_freq, inv_freq]).reshape(1, HEAD_DIM)
    sign = jnp.concatenate([-jnp.ones((HEAD_DIM // 2,), F32), jnp.ones((HEAD_DIM // 2,), F32)]).reshape(1, HEAD_DIM)
    wide = DIFF_QK_WIDTH
    row_spec = lambda cb: pl.BlockSpec((tm, wide), lambda i: (i, cb))
    vec = pl.BlockSpec((1, HEAD_DIM), lambda i: (0, 0))
    out_spec = pl.BlockSpec((tm, wide), lambda i: (i, 0))
    shp = jax.ShapeDtypeStruct((SEQ, wide), BF16)
    return pl.pallas_call(
        _qk_prep_kernel,
        grid=(SEQ // tm,),
        in_specs=[pl.BlockSpec((tm, 1), lambda i: (i, 0)), vec, vec, vec, vec,
                  row_spec(COL_Q // wide), row_spec(COL_K // wide)],
        out_specs=[out_spec, out_spec],
        out_shape=[shp, shp],
        compiler_params=_params(("arbitrary",), 40),
        name="qk_prep",
    )(positions.reshape(SEQ, 1), freq, sign, gq.reshape(1, -1), gk.reshape(1, -1), z, z)


KV_UNROLL = 4
def _diff_attn_kernel(lq1_ref, lk1_ref, lq2_ref, lk2_ref, sg_ref, q_ref, k_ref, v_ref, o_ref,
                      m_ref, l_ref, acc_ref, *, tq, lam_init):
    i = pl.program_id(1)
    m_ref[...] = jnp.full(m_ref.shape, -jnp.inf, F32)
    l_ref[...] = jnp.zeros(l_ref.shape, F32)
    acc_ref[...] = jnp.zeros(acc_ref.shape, F32)
    lanes = tq // 128

    def block(j, masked):
        start = pl.multiple_of(j * tq, tq)
        kb = k_ref[pl.ds(start, tq), :]
        vb = v_ref[pl.ds(start, tq), :]
        for a in range(2):
            q = q_ref[:, a * HEAD_DIM:(a + 1) * HEAD_DIM]
            k = kb[:, a * HEAD_DIM:(a + 1) * HEAD_DIM]
            s = lax.dot_general(q, k, (((1,), (1,)), ((), ())), preferred_element_type=F32)
            if masked:
                row = lax.broadcasted_iota(jnp.int32, (tq, tq), 0)
                col = lax.broadcasted_iota(jnp.int32, (tq, tq), 1)
                s = jnp.where(col <= row, s, -jnp.inf)
            m_prev = m_ref[a]
            m_new = jnp.maximum(m_prev, jnp.max(s, axis=-1, keepdims=True))
            alpha = jnp.exp2(m_prev - m_new)
            p = jnp.exp2(s - jnp.concatenate([m_new] * lanes, axis=1))
            psum = p[:, 0:128]
            for c in range(1, lanes):
                psum = psum + p[:, c * 128:(c + 1) * 128]
            l_ref[a] = alpha * l_ref[a] + psum
            acc_ref[a] = (jnp.concatenate([alpha] * (DIFF_V_DIM // 128), axis=1) * acc_ref[a]
                          + jnp.dot(p.astype(BF16), vb, preferred_element_type=F32))
            m_ref[a] = m_new

    def group(g, carry):
        for u in range(KV_UNROLL):
            block(KV_UNROLL * g + u, False)
        return carry

    groups = i // KV_UNROLL
    lax.fori_loop(0, groups, group, 0)
    rem = i - groups * KV_UNROLL
    for r in range(KV_UNROLL):
        @pl.when(rem == r)
        def _(r=r):
            for u in range(r):
                block(groups * KV_UNROLL + u, False)
            block(i, True)

    lam = (jnp.exp(jnp.sum(lq1_ref[...] * lk1_ref[...], axis=-1, keepdims=True))
           - jnp.exp(jnp.sum(lq2_ref[...] * lk2_ref[...], axis=-1, keepdims=True)) + lam_init)
    l1 = jnp.sum(l_ref[0], axis=-1, keepdims=True)
    l2 = jnp.sum(l_ref[1], axis=-1, keepdims=True)
    o = acc_ref[0] / l1 - lam * (acc_ref[1] / l2)
    o_ref[...] = (_rms(o, sg_ref[...]) * (1.0 - lam_init)).astype(o_ref.dtype)


def _diff_attn(q, k, v, lq1, lk1, lq2, lk2, subln_g, lam_init, tq=512):
    vec = pl.BlockSpec((1, HEAD_DIM), lambda h, i: (0, 0))
    kv_spec = pl.BlockSpec((SEQ, DIFF_V_DIM), lambda h, i: (0, h))
    return pl.pallas_call(
        functools.partial(_diff_attn_kernel, tq=tq, lam_init=lam_init),
        grid=(DIFF_HEADS, SEQ // tq),
        in_specs=[vec, vec, vec, vec, pl.BlockSpec((1, DIFF_V_DIM), lambda h, i: (0, 0)),
                  pl.BlockSpec((tq, DIFF_V_DIM), lambda h, i: (i, h)), kv_spec, kv_spec],
        out_specs=pl.BlockSpec((tq, DIFF_V_DIM), lambda h, i: (i, h)),
        out_shape=jax.ShapeDtypeStruct((SEQ, DIFF_WIDTH), BF16),
        scratch_shapes=[pltpu.VMEM((2, tq, 128), F32), pltpu.VMEM((2, tq, 128), F32),
                        pltpu.VMEM((2, tq, DIFF_V_DIM), F32)],
        compiler_params=_params(("arbitrary", "arbitrary"), 48),
        name="diff_attn",
    )(lq1.reshape(1, -1), lk1.reshape(1, -1), lq2.reshape(1, -1), lk2.reshape(1, -1),
      subln_g.reshape(1, -1), q, k, v)


def _mem_attn_kernel(zm_ref, kv_ref, gq_ref, gk_ref, o_ref):
    scale = MEM_HEAD_DIM ** -0.5
    for h in range(MEM_HEADS):
        lo, hi = h * MEM_HEAD_DIM, (h + 1) * MEM_HEAD_DIM
        q = (_rms(zm_ref[:, lo:hi], gq_ref[...]) * scale).astype(BF16)
        k = _rms(kv_ref[:, lo:hi], gk_ref[...]).astype(BF16)
        v = kv_ref[:, MEM_WIDTH + lo:MEM_WIDTH + hi].astype(BF16)
        s = lax.dot_general(q, k, (((1,), (1,)), ((), ())), preferred_element_type=F32)
        p = jnp.exp(s - jnp.max(s, axis=-1, keepdims=True))
        l = jnp.sum(p, axis=-1, keepdims=True)
        o = jnp.dot(p.astype(BF16), v, preferred_element_type=F32) / l
        o_ref[:, lo:hi] = o.astype(o_ref.dtype)


def _mem_attn(z, kv, gq, gk, tm=512):
    vec = pl.BlockSpec((1, MEM_HEAD_DIM), lambda i: (0, 0))
    return pl.pallas_call(
        _mem_attn_kernel,
        grid=(SEQ // tm,),
        in_specs=[pl.BlockSpec((tm, MEM_WIDTH), lambda i: (i, 0)),
                  pl.BlockSpec((MEM_LEN, 2 * MEM_WIDTH), lambda i: (0, 0)), vec, vec],
        out_specs=pl.BlockSpec((tm, MEM_WIDTH), lambda i: (i, 0)),
        out_shape=jax.ShapeDtypeStruct((SEQ, MEM_WIDTH), BF16),
        compiler_params=_params(("arbitrary",), 40),
        name="mem_attn",
    )(z, kv, gq.reshape(1, -1), gk.reshape(1, -1))


def _router_kernel(x_ref, g_ref, wr_ref, br_ref, h_ref, gate_ref, exp_ref, rank_ref, cnt_ref, carry_ref, *, tm):
    @pl.when(pl.program_id(0) == 0)
    def _():
        carry_ref[...] = jnp.zeros(carry_ref.shape, F32)

    h = _rms(x_ref[...], g_ref[...])
    _store_slabs(h_ref, h, tm)
    w = wr_ref[...]
    h_hi = h.astype(BF16)
    h_lo = (h - h_hi.astype(F32)).astype(BF16)
    w_hi = w.astype(BF16)
    w_lo = (w - w_hi.astype(F32)).astype(BF16)
    logits = (jnp.dot(h_hi, w_hi, preferred_element_type=F32) + jnp.dot(h_hi, w_lo, preferred_element_type=F32)
              + jnp.dot(h_lo, w_hi, preferred_element_type=F32) + br_ref[...])
    lane = lax.broadcasted_iota(jnp.int32, logits.shape, 1)
    slot = lax.broadcasted_iota(jnp.int32, (tm, TOP_K), 1)
    work = logits
    sel = jnp.zeros(logits.shape, jnp.bool_)
    hits = []
    top = None
    nums = jnp.zeros((tm, TOP_K), F32)
    experts = jnp.zeros((tm, TOP_K), jnp.int32)
    denom = jnp.zeros((tm, 1), F32)
    for k in range(TOP_K):
        m = jnp.max(work, axis=-1, keepdims=True)
        idx = jnp.min(jnp.where(work == m, lane, N_EXPERTS), axis=-1, keepdims=True)
        hit = lane == idx
        if k == 0:
            top = m
        e = jnp.exp(m - top)
        nums = jnp.where(slot == k, e, nums)
        experts = jnp.where(slot == k, idx, experts)
        denom = denom + e
        work = jnp.where(hit, -jnp.inf, work)
        sel = jnp.logical_or(sel, hit)
        hits.append(hit)
    gate_ref[...] = nums / denom
    exp_ref[...] = experts
    r = lax.broadcasted_iota(jnp.int32, (tm, tm), 0)
    c = lax.broadcasted_iota(jnp.int32, (tm, tm), 1)
    strict = jnp.where(c < r, 1.0, 0.0).astype(BF16)
    chosen = jnp.where(sel, 1.0, 0.0)
    rank = jnp.dot(strict, chosen.astype(BF16), preferred_element_type=F32) + carry_ref[...]
    ranks = jnp.zeros((tm, TOP_K), F32)
    for k in range(TOP_K):
        ranks = jnp.where(slot == k, jnp.sum(jnp.where(hits[k], rank, 0.0), axis=-1, keepdims=True), ranks)
    rank_ref[...] = ranks.astype(jnp.int32)
    carry_ref[...] = carry_ref[...] + jnp.sum(chosen, axis=0, keepdims=True)
    cnt_ref[...] = carry_ref[...].astype(jnp.int32)


def _router(x, g, w_router, b_router, tm=256):
    small = lambda: pl.BlockSpec((tm, TOP_K), lambda i: (i, 0))
    return pl.pallas_call(
        functools.partial(_router_kernel, tm=tm),
        grid=(SEQ // tm,),
        in_specs=[pl.BlockSpec((tm, D_MODEL), lambda i: (i, 0)), pl.BlockSpec((1, D_MODEL), lambda i: (0, 0)),
                  pl.BlockSpec((D_MODEL, N_EXPERTS), lambda i: (0, 0)), pl.BlockSpec((1, N_EXPERTS), lambda i: (0, 0))],
        out_specs=[pl.BlockSpec((tm * SLAB_ROWS, SLAB_LANES), lambda i: (i, 0)), small(), small(), small(),
                   pl.BlockSpec((1, N_EXPERTS), lambda i: (0, 0))],
        out_shape=[jax.ShapeDtypeStruct((SEQ * SLAB_ROWS, SLAB_LANES), jnp.uint32),
                   jax.ShapeDtypeStruct((SEQ, TOP_K), F32),
                   jax.ShapeDtypeStruct((SEQ, TOP_K), jnp.int32),
                   jax.ShapeDtypeStruct((SEQ, TOP_K), jnp.int32),
                   jax.ShapeDtypeStruct((1, N_EXPERTS), jnp.int32)],
        scratch_shapes=[pltpu.VMEM((1, N_EXPERTS), F32)],
        compiler_params=_params(("arbitrary",), 40),
        name="router",
    )(x, g.reshape(1, -1), w_router, b_router.reshape(1, -1))


DMA_PRIORITIES = 2


def _slot_copy(src_ref, src_row, dst_ref, dst_row, sem):
    return pltpu.make_async_copy(src_ref.at[pl.ds(src_row * SLAB_ROWS, SLAB_ROWS)],
                                 dst_ref.at[pl.ds(dst_row * SLAB_ROWS, SLAB_ROWS)], sem)


def _tile_fill_copy(zero_ref, dst_ref, tile, sem):
    rows = MOE_TILE * SLAB_ROWS
    return pltpu.make_async_copy(zero_ref, dst_ref.at[pl.ds(tile * rows, rows)], sem)


def _dispatch_kernel(pstart_ref, fill_ref, exp_ref, rank_ref, h_hbm, o_ref, slot_ref,
                     zero_ref, buf_a, buf_b, fill_sem, in_sem, out_sem, *, tc):
    i = pl.program_id(0)
    n = pl.num_programs(0)
    rows = tc * SLAB_ROWS

    def load(tile, buf, s):
        return pltpu.make_async_copy(h_hbm.at[pl.ds(tile * rows, rows)], buf, in_sem.at[s])

    def wait_rows(buf, s):
        for _ in range(TOP_K):
            pltpu.make_async_copy(buf, o_ref.at[pl.ds(0, rows)], out_sem.at[s]).wait()

    @pl.when(i == 0)
    def _():
        load(0, buf_a, 0).start()
        zero_ref[...] = jnp.zeros(zero_ref.shape, zero_ref.dtype)

        def fill(t, carry):
            @pl.when(fill_ref[t] != 0)
            def _():
                _tile_fill_copy(zero_ref, o_ref, t, fill_sem).start()
            return carry

        def fill_wait(t, carry):
            @pl.when(fill_ref[t] != 0)
            def _():
                _tile_fill_copy(zero_ref, o_ref, t, fill_sem).wait()
            return carry

        lax.fori_loop(0, MOE_NT, fill, 0)
        lax.fori_loop(0, MOE_NT, fill_wait, 0)

    def step(cur_buf, cs, nxt_buf, ns):
        @pl.when(i > 0)
        def _():
            wait_rows(nxt_buf, ns)

        @pl.when(i + 1 < n)
        def _():
            load(i + 1, nxt_buf, ns).start()

        load(i, cur_buf, cs).wait()

        def issue(r, carry):
            for k in range(TOP_K):
                a = r * TOP_K + k
                slot = pstart_ref[exp_ref[0, 0, a]] + rank_ref[0, 0, a]
                slot_ref[0, 0, a] = slot
                _slot_copy(cur_buf, r, o_ref, slot, out_sem.at[cs]).start(priority=k % DMA_PRIORITIES)
            return carry

        lax.fori_loop(0, tc, issue, 0, unroll=4)

        @pl.when(i == n - 1)
        def _():
            wait_rows(cur_buf, cs)

    @pl.when(i % 2 == 0)
    def _():
        step(buf_a, 0, buf_b, 1)

    @pl.when(i % 2 == 1)
    def _():
        step(buf_b, 1, buf_a, 0)


def _dispatch(hp, experts4, rank4, pstart, fill_tiles, tc=128):
    nb = SEQ // tc
    idx_spec = pl.BlockSpec((1, 1, tc * TOP_K), lambda i, ps, ft: (i, 0, 0), memory_space=pltpu.SMEM)
    return pl.pallas_call(
        functools.partial(_dispatch_kernel, tc=tc),
        grid_spec=pltpu.PrefetchScalarGridSpec(
            num_scalar_prefetch=2,
            grid=(nb,),
            in_specs=[idx_spec, idx_spec, pl.BlockSpec(memory_space=pl.ANY)],
            out_specs=[pl.BlockSpec(memory_space=pl.ANY), idx_spec],
            scratch_shapes=[pltpu.VMEM((MOE_TILE * SLAB_ROWS, SLAB_LANES), jnp.uint32),
                            pltpu.VMEM((tc * SLAB_ROWS, SLAB_LANES), jnp.uint32),
                            pltpu.VMEM((tc * SLAB_ROWS, SLAB_LANES), jnp.uint32),
                            pltpu.SemaphoreType.DMA(()), pltpu.SemaphoreType.DMA((2,)), pltpu.SemaphoreType.DMA((2,))],
        ),
        out_shape=[jax.ShapeDtypeStruct((MOE_ROWS * SLAB_ROWS, SLAB_LANES), jnp.uint32),
                   jax.ShapeDtypeStruct((nb, 1, tc * TOP_K), jnp.int32)],
        compiler_params=_params(("arbitrary",), 40),
        name="moe_dispatch",
    )(pstart, fill_tiles, experts4.reshape(nb, 1, tc * TOP_K), rank4.reshape(nb, 1, tc * TOP_K), hp)


def _group_edges(blk_ref, t):
    e = blk_ref[t]
    first = jnp.logical_or(t == 0, blk_ref[jnp.maximum(t - 1, 0)] != e)
    last = jnp.logical_or(t == MOE_NT - 1, blk_ref[jnp.minimum(t + 1, MOE_NT - 1)] != e)
    return e, first, last


def _stream_weights(blk_ref, nxt_ref, slot_ref, n_pass, copies, t):
    j = pl.program_id(0)
    e, first, last = _group_edges(blk_ref, t)

    @pl.when(jnp.logical_and(j == 0, t == 0))
    def _():
        slot_ref[0] = 0
        for c in copies(e, j, 0):
            c.start()

    @pl.when(first)
    def _():
        slot = slot_ref[0]
        for c in copies(e, j, slot):
            c.wait()
        nt = nxt_ref[t]
        same_pass = nt < MOE_NT
        ne = blk_ref[jnp.where(same_pass, nt, 0)]
        nj = jnp.where(same_pass, j, j + 1)

        @pl.when(jnp.logical_or(same_pass, j + 1 < n_pass))
        def _():
            for c in copies(ne, nj, 1 - slot):
                c.start()

    slot = slot_ref[0]

    @pl.when(last)
    def _():
        slot_ref[0] = 1 - slot

    return slot


def _pair_step(blk_ref, nxt_ref, nused_ref, slot_ref, n_pass, copies, compute, clear):
    ta = 2 * pl.program_id(1)
    tb = ta + 1
    ea = blk_ref[ta]
    eb = blk_ref[tb]
    va = ta < nused_ref[0]
    vb = tb < nused_ref[0]
    merged = jnp.logical_and(vb, ea == eb)

    slot_a = _stream_weights(blk_ref, nxt_ref, slot_ref, n_pass, copies, ta)

    @pl.when(merged)
    def _():
        compute(0, 2 * MOE_TILE, ea, slot_a)

    @pl.when(jnp.logical_and(va, jnp.logical_not(merged)))
    def _():
        compute(0, MOE_TILE, ea, slot_a)

    slot_b = _stream_weights(blk_ref, nxt_ref, slot_ref, n_pass, copies, tb)

    @pl.when(jnp.logical_and(vb, jnp.logical_not(merged)))
    def _():
        compute(MOE_TILE, MOE_TILE, eb, slot_b)

    @pl.when(jnp.logical_not(va))
    def _():
        clear(0, MOE_TILE)

    @pl.when(jnp.logical_not(vb))
    def _():
        clear(MOE_TILE, MOE_TILE)


def _moe_up_kernel(blk_ref, nxt_ref, nused_ref, x_ref, w_hbm, b_ref, o_ref, wbuf, sem, slot_ref, *, tn):
    j = pl.program_id(0)

    def copies(e, j, slot):
        return [pltpu.make_async_copy(w_hbm.at[e, :, pl.ds(pl.multiple_of(part * D_FF + j * tn, tn), tn)],
                                      wbuf.at[slot, part], sem.at[slot, part]) for part in range(2)]

    def compute(row0, rows, e, slot):
        x = _load_slabs(x_ref, row0, rows)

        def proj(part):
            bias = b_ref[pl.ds(e, 1), pl.ds(pl.multiple_of(part * D_FF + j * tn, tn), tn)]
            return jnp.dot(x, wbuf[slot, part].astype(BF16), preferred_element_type=F32) + bias

        gate = jnp.minimum(proj(0), SWIGLU_LIMIT)
        up = jnp.clip(proj(1), -SWIGLU_LIMIT, SWIGLU_LIMIT)
        act = (up + 1.0) * (gate * jax.nn.sigmoid(SWIGLU_ALPHA * gate))
        o_ref[row0:row0 + rows, :] = act.astype(o_ref.dtype)

    def clear(row0, rows):
        o_ref[row0:row0 + rows, :] = jnp.zeros((rows, tn), o_ref.dtype)

    _pair_step(blk_ref, nxt_ref, nused_ref, slot_ref, D_FF // tn, copies, compute, clear)


def _moe_up(xs, blk_e, nxt, n_used, w_gate_up, b_gate_up, tn=512):
    nj = D_FF // tn
    last = lambda u, nu: jnp.minimum(u, (nu[0] - 1) // 2)
    return pl.pallas_call(
        functools.partial(_moe_up_kernel, tn=tn),
        grid_spec=pltpu.PrefetchScalarGridSpec(
            num_scalar_prefetch=3,
            grid=(nj, MOE_NT // 2),
            in_specs=[
                pl.BlockSpec((2 * MOE_TILE * SLAB_ROWS, SLAB_LANES), lambda j, u, be, nx, nu: (last(u, nu), 0)),
                pl.BlockSpec(memory_space=pl.ANY),
                pl.BlockSpec((N_EXPERTS, 2 * D_FF), lambda j, u, be, nx, nu: (0, 0)),
            ],
            out_specs=pl.BlockSpec((2 * MOE_TILE, tn), lambda j, u, be, nx, nu: (u, j)),
            scratch_shapes=[pltpu.VMEM((2, 2, D_MODEL, tn), F32), pltpu.SemaphoreType.DMA((2, 2)),
                            pltpu.SMEM((1,), jnp.int32)],
        ),
        out_shape=jax.ShapeDtypeStruct((MOE_ROWS, D_FF), BF16),
        compiler_params=_params(("arbitrary", "arbitrary"), 56),
        name="moe_up",
    )(blk_e, nxt, n_used, xs, w_gate_up, b_gate_up)


DOWN_TN = D_MODEL // 2
DOWN_WORDS = DOWN_TN // 2
DOWN_ROWS = DOWN_WORDS // SLAB_LANES


def _moe_down_kernel(blk_ref, nxt_ref, nused_ref, a_ref, w_hbm, b_ref, o_ref, wbuf, sem, slot_ref):
    def copies(e, j, slot):
        return [pltpu.make_async_copy(w_hbm.at[e, :, pl.ds(pl.multiple_of(j * DOWN_TN, DOWN_TN), DOWN_TN)],
                                      wbuf.at[slot], sem.at[slot])]

    def compute(row0, rows, e, slot):
        bias = b_ref[pl.ds(e, 1), pl.ds(pl.multiple_of(pl.program_id(0) * DOWN_TN, DOWN_TN), DOWN_TN)]
        y = jnp.dot(a_ref[row0:row0 + rows, :], wbuf[slot].astype(BF16), preferred_element_type=F32) + bias
        words = pltpu.pack_elementwise([y[:, :DOWN_WORDS], y[:, DOWN_WORDS:]], packed_dtype=BF16)
        flat = o_ref.reshape(2 * MOE_TILE * DOWN_ROWS, SLAB_LANES)
        for s in range(DOWN_ROWS):
            flat[pl.ds(row0 * DOWN_ROWS + s, rows, stride=DOWN_ROWS), :] = words[:, s * SLAB_LANES:(s + 1) * SLAB_LANES]

    def clear(row0, rows):
        o_ref[row0:row0 + rows] = jnp.zeros((rows, DOWN_ROWS, SLAB_LANES), o_ref.dtype)

    _pair_step(blk_ref, nxt_ref, nused_ref, slot_ref, D_MODEL // DOWN_TN, copies, compute, clear)


def _moe_down(act, blk_e, nxt, n_used, w_down, b_down):
    nj = D_MODEL // DOWN_TN
    last = lambda u, nu: jnp.minimum(u, (nu[0] - 1) // 2)
    return pl.pallas_call(
        _moe_down_kernel,
        grid_spec=pltpu.PrefetchScalarGridSpec(
            num_scalar_prefetch=3,
            grid=(nj, MOE_NT // 2),
            in_specs=[
                pl.BlockSpec((2 * MOE_TILE, D_FF), lambda j, u, be, nx, nu: (last(u, nu), 0)),
                pl.BlockSpec(memory_space=pl.ANY),
                pl.BlockSpec((N_EXPERTS, D_MODEL), lambda j, u, be, nx, nu: (0, 0)),
            ],
            out_specs=pl.BlockSpec((2 * MOE_TILE, DOWN_ROWS, SLAB_LANES), lambda j, u, be, nx, nu: (u, j, 0)),
            scratch_shapes=[pltpu.VMEM((2, D_FF, DOWN_TN), F32), pltpu.SemaphoreType.DMA((2,)),
                            pltpu.SMEM((1,), jnp.int32)],
        ),
        out_shape=jax.ShapeDtypeStruct((MOE_ROWS, SLAB_ROWS, SLAB_LANES), jnp.uint32),
        compiler_params=_params(("arbitrary", "arbitrary"), 56),
        name="moe_down",
    )(blk_e, nxt, n_used, act, w_down, b_down).reshape(MOE_ROWS * SLAB_ROWS, SLAB_LANES)


COMBINE_ROWS = 32


def _combine_kernel(cur_ref, nxt_ref, ys_ref, x_ref, g_ref, o_ref, buf_a, buf_b, sem, *, tc):
    i = pl.program_id(0)

    def fetch(idx_ref, a, buf, slot):
        r, k = divmod(a, TOP_K)
        return _slot_copy(ys_ref, idx_ref[0, 0, a], buf.at[k], r, sem.at[slot])

    def wait_tile(buf, slot):
        for k in range(TOP_K):
            pltpu.make_async_copy(ys_ref.at[pl.ds(0, tc * SLAB_ROWS)], buf.at[k], sem.at[slot]).wait()

    @pl.when(i == 0)
    def _():
        def issue(r, carry):
            for k in range(TOP_K):
                _slot_copy(ys_ref, cur_ref[0, 0, r * TOP_K + k], buf_a.at[k], r, sem.at[0]).start(
                    priority=k % DMA_PRIORITIES)
            return carry

        lax.fori_loop(0, tc, issue, 0, unroll=4)

    def step(cur_buf, cur_slot, nxt_buf, nxt_slot):
        wait_tile(cur_buf, cur_slot)
        n_chunks = (tc // COMBINE_ROWS) * SLAB_ROWS
        per_chunk = tc * TOP_K // n_chunks
        a = 0
        for t0 in range(0, tc, COMBINE_ROWS):
            rows = slice(t0, t0 + COMBINE_ROWS)
            gates = [jnp.broadcast_to(g_ref[rows, k:k + 1], (COMBINE_ROWS, SLAB_LANES)) for k in range(TOP_K)]
            for s in range(SLAB_ROWS):
                for _ in range(per_chunk):
                    fetch(nxt_ref, a, nxt_buf, nxt_slot).start(priority=a % DMA_PRIORITIES)
                    a += 1
                lo_sum = None
                hi_sum = None
                for k in range(TOP_K):
                    words = cur_buf[k, pl.ds(t0 * SLAB_ROWS + s, COMBINE_ROWS, stride=SLAB_ROWS), :]
                    lo, hi = _unpack_bf16(words)
                    lo_sum = gates[k] * lo if lo_sum is None else lo_sum + gates[k] * lo
                    hi_sum = gates[k] * hi if hi_sum is None else hi_sum + gates[k] * hi
                c = (s // DOWN_ROWS) * DOWN_TN + (s % DOWN_ROWS) * SLAB_LANES
                o_ref[rows, c:c + SLAB_LANES] = x_ref[rows, c:c + SLAB_LANES] + lo_sum
                c += DOWN_WORDS
                o_ref[rows, c:c + SLAB_LANES] = x_ref[rows, c:c + SLAB_LANES] + hi_sum

        @pl.when(i == pl.num_programs(0) - 1)
        def _():
            wait_tile(nxt_buf, nxt_slot)

    @pl.when(i % 2 == 0)
    def _():
        step(buf_a, 0, buf_b, 1)

    @pl.when(i % 2 == 1)
    def _():
        step(buf_b, 1, buf_a, 0)


def _combine(ys, x2, slots, gates4, tc=128):
    nb = SEQ // tc
    idx_spec = lambda f: pl.BlockSpec((1, 1, tc * TOP_K), lambda i: (f(i), 0, 0), memory_space=pltpu.SMEM)
    return pl.pallas_call(
        functools.partial(_combine_kernel, tc=tc),
        grid=(nb,),
        in_specs=[idx_spec(lambda i: i), idx_spec(lambda i: jnp.minimum(i + 1, nb - 1)),
                  pl.BlockSpec(memory_space=pl.ANY),
                  pl.BlockSpec((tc, D_MODEL), lambda i: (i, 0)),
                  pl.BlockSpec((tc, TOP_K), lambda i: (i, 0))],
        out_specs=pl.BlockSpec((tc, D_MODEL), lambda i: (i, 0)),
        out_shape=jax.ShapeDtypeStruct((SEQ, D_MODEL), F32),
        scratch_shapes=[pltpu.VMEM((TOP_K, tc * SLAB_ROWS, SLAB_LANES), jnp.uint32),
                        pltpu.VMEM((TOP_K, tc * SLAB_ROWS, SLAB_LANES), jnp.uint32), pltpu.SemaphoreType.DMA((2,))],
        compiler_params=_params(("arbitrary",), 40),
        name="moe_combine",
    )(slots, slots, ys, x2, gates4)


def kernel(x, mem, positions, attn_norm_g, w_in, gmlp_ln_g, gmlp_ln_b, gmlp_w_s, gmlp_b_s, diff_q_norm_g, diff_k_norm_g, diff_lambda_q1, diff_lambda_k1, diff_lambda_q2, diff_lambda_k2, diff_subln_g, mem_norm_g, w_mem_kv, mem_q_norm_g, mem_k_norm_g, w_out, ffn_norm_g, w_router, b_router, w_gate_up, b_gate_up, w_down, b_down):
    depth = attn_norm_g.shape[0]
    xs = x.reshape(SEQ, D_MODEL)
    mem2 = mem.reshape(MEM_LEN, D_MODEL)
    for i in range(depth):
        lam_init = 0.8 - 0.6 * math.exp(-0.3 * i)
        h = _rmsnorm_rows(xs, attn_norm_g[i], tm=256)

        def in_proj(name, col0, width, dtype):
            return _matmul([h], w_in[i], None, dtype, tm=1024, tn=512, name=name, cols=(col0, width))

        z_gqk = in_proj("in_proj_gqk", 0, COL_V, F32)
        v = in_proj("in_proj_v", COL_V, DIFF_WIDTH, BF16)
        z_m = in_proj("in_proj_mem", COL_M, MEM_WIDTH, F32)

        y_g = _gmlp(z_gqk, gmlp_ln_g[i], gmlp_ln_b[i], gmlp_w_s[i], gmlp_b_s[i])

        q, k = _qk_prep(z_gqk, positions, diff_q_norm_g[i], diff_k_norm_g[i])
        y_d = _diff_attn(q, k, v, diff_lambda_q1[i], diff_lambda_k1[i], diff_lambda_q2[i], diff_lambda_k2[i],
                         diff_subln_g[i], lam_init)

        hm = _rmsnorm_rows(mem2, mem_norm_g[i], tm=256)
        kv = _matmul([hm], w_mem_kv[i], None, F32, tm=MEM_LEN, tn=512, name="mem_kv_proj")
        y_m = _mem_attn(z_m, kv, mem_q_norm_g[i], mem_k_norm_g[i])

        x2 = _matmul([y_g, y_d, y_m], w_out[i], xs, F32, tm=1024, tn=512, name="out_proj")

        hp, gates4, experts4, rank4, counts = _router(x2, ffn_norm_g[i], w_router[i], b_router[i])

        counts = counts[0]
        padded = (counts + MOE_TILE - 1) // MOE_TILE * MOE_TILE
        pend = jnp.cumsum(padded)
        pstart = pend - padded
        n_used = (pend[-1] // MOE_TILE).astype(jnp.int32).reshape(1)
        blk_start = jnp.arange(MOE_NT, dtype=jnp.int32) * MOE_TILE
        blk_e = jnp.minimum(jnp.sum(pend[None, :] <= blk_start[:, None], axis=1), N_EXPERTS - 1).astype(jnp.int32)
        blk_end = blk_start + MOE_TILE
        fill_tiles = jnp.logical_or(jnp.any(blk_end[:, None] == pend[None, :], axis=1),
                                    blk_start >= pend[-1]).astype(jnp.int32)

        tile_ids = jnp.arange(MOE_NT, dtype=jnp.int32)
        run_start = jnp.concatenate([jnp.ones((1,), jnp.bool_), blk_e[1:] != blk_e[:-1]])
        starts_from = lax.cummin(jnp.where(run_start, tile_ids, MOE_NT), axis=0, reverse=True)
        nxt = jnp.concatenate([starts_from[1:], jnp.full((1,), MOE_NT, jnp.int32)]).astype(jnp.int32)

        xg, slots = _dispatch(hp, experts4, rank4, pstart, fill_tiles)
        act = _moe_up(xg, blk_e, nxt, n_used, w_gate_up[i], b_gate_up[i])
        ys = _moe_down(act, blk_e, nxt, n_used, w_down[i], b_down[i])
        xs = _combine(ys, x2, slots, gates4)
    return xs.reshape(x.shape)
```

```python
import functools
import math

import jax
import jax.numpy as jnp
from jax import lax
from jax.experimental import pallas as pl
from jax.experimental.pallas import tpu as pltpu

F32 = jnp.float32
BF16 = jnp.bfloat16

D_MODEL = 4096
SEQ = 8192
HEAD_DIM = 128
CHUNK = 128
GMLP_WIDTH = 1024
GMLP_HEADS = 8
DIFF_HEADS = 8
DIFF_V_DIM = 256
DIFF_QK_WIDTH = 2048
DIFF_WIDTH = 2048
MEM_WIDTH = 1024
MEM_HEADS = 4
MEM_HEAD_DIM = 256
MEM_LEN = 256
IN_WIDTH = 9216
N_EXPERTS = 32
TOP_K = 4
D_FF = 1536
SWIGLU_LIMIT = 7.0
SWIGLU_ALPHA = 1.702
ROPE_THETA = 10000.0
EPS = 1e-6

COL_Q = 2 * GMLP_WIDTH
COL_K = COL_Q + DIFF_QK_WIDTH
COL_V = COL_K + DIFF_QK_WIDTH
COL_M = COL_V + DIFF_WIDTH

MOE_TILE = 256
MOE_ROWS = ((SEQ * TOP_K + N_EXPERTS * (MOE_TILE - 1) + MOE_TILE - 1) // MOE_TILE) * MOE_TILE
MOE_NT = MOE_ROWS // MOE_TILE

MIB = 1 << 20


def _params(sem, vmem_mib):
    return pltpu.CompilerParams(dimension_semantics=sem, vmem_limit_bytes=vmem_mib * MIB)


def _rms(x, g):
    ms = jnp.mean(x * x, axis=-1, keepdims=True)
    return x * lax.rsqrt(ms + EPS) * g


SLAB_LANES = 128
SLAB_ROWS = D_MODEL // 2 // SLAB_LANES


def _unpack_bf16(words):
    lo = pltpu.unpack_elementwise(words, index=0, packed_dtype=BF16, unpacked_dtype=F32)
    hi = pltpu.unpack_elementwise(words, index=1, packed_dtype=BF16, unpacked_dtype=F32)
    return lo, hi


def _store_slabs(o_ref, h, rows):
    half = D_MODEL // 2
    for s in range(SLAB_ROWS):
        lo = h[:, s * SLAB_LANES:(s + 1) * SLAB_LANES]
        hi = h[:, half + s * SLAB_LANES:half + (s + 1) * SLAB_LANES]
        o_ref[pl.ds(s, rows, stride=SLAB_ROWS), :] = pltpu.pack_elementwise([lo, hi], packed_dtype=BF16)


def _load_slabs(x_ref, row0, rows):
    los, his = [], []
    for s in range(SLAB_ROWS):
        lo, hi = _unpack_bf16(x_ref[pl.ds(row0 * SLAB_ROWS + s, rows, stride=SLAB_ROWS), :])
        los.append(lo.astype(BF16))
        his.append(hi.astype(BF16))
    return jnp.concatenate(los + his, axis=1)


def _rmsnorm_kernel(x_ref, g_ref, o_ref):
    o_ref[...] = _rms(x_ref[...], g_ref[...]).astype(o_ref.dtype)


def _rmsnorm_rows(x, g, tm):
    m, d = x.shape
    return pl.pallas_call(
        _rmsnorm_kernel,
        grid=(m // tm,),
        in_specs=[pl.BlockSpec((tm, d), lambda i: (i, 0)), pl.BlockSpec((1, d), lambda i: (0, 0))],
        out_specs=pl.BlockSpec((tm, d), lambda i: (i, 0)),
        out_shape=jax.ShapeDtypeStruct((m, d), BF16),
        compiler_params=_params(("arbitrary",), 40),
        name="rmsnorm_rows",
    )(x, g.reshape(1, d))


def _matmul_kernel(*refs, widths, has_res):
    n = len(widths)
    lhs = refs[:n]
    w_ref = refs[n]
    res_ref = refs[n + 1] if has_res else None
    o_ref = refs[n + 1 + has_res]

    acc = None
    off = 0
    for r, kw in zip(lhs, widths):
        part = jnp.dot(r[...], w_ref[off:off + kw, :].astype(BF16), preferred_element_type=F32)
        acc = part if acc is None else acc + part
        off += kw
    if has_res:
        acc = acc + res_ref[...]
    o_ref[...] = acc.astype(o_ref.dtype)


def _matmul(lhs_list, w, res, out_dtype, tm, tn, name, cols=None):
    m = lhs_list[0].shape[0]
    k = w.shape[0]
    col0, n = cols if cols is not None else (0, w.shape[1])
    assert col0 % tn == 0 and n % tn == 0
    widths = tuple(a.shape[1] for a in lhs_list)
    assert sum(widths) == k
    in_specs = [pl.BlockSpec((tm, kw), lambda j, i: (i, 0)) for kw in widths]
    in_specs.append(pl.BlockSpec((k, tn), lambda j, i: (0, j + col0 // tn)))
    args = list(lhs_list) + [w]
    if res is not None:
        in_specs.append(pl.BlockSpec((tm, tn), lambda j, i: (i, j)))
        args.append(res)
    return pl.pallas_call(
        functools.partial(_matmul_kernel, widths=widths, has_res=res is not None),
        grid=(n // tn, m // tm),
        in_specs=in_specs,
        out_specs=pl.BlockSpec((tm, tn), lambda j, i: (i, j)),
        out_shape=jax.ShapeDtypeStruct((m, n), out_dtype),
        compiler_params=_params(("arbitrary", "arbitrary"), 56),
        name=name,
    )(*args)


def _gmlp_kernel(zu_ref, zv_ref, lng_ref, lnb_ref, ws_ref, bst_ref, o_ref, *, tm):
    inv_sqrt2 = 1.0 / math.sqrt(2.0)

    def gelu(x):
        return 0.5 * x * (1.0 + lax.erf(x * inv_sqrt2))

    u = gelu(zu_ref[...])
    v = gelu(zv_ref[...])
    mu = jnp.mean(v, axis=-1, keepdims=True)
    vc = v - mu
    var = jnp.mean(vc * vc, axis=-1, keepdims=True)
    vn = (vc * lax.rsqrt(var + EPS) * lng_ref[...] + lnb_ref[...]).astype(BF16)
    row = lax.broadcasted_iota(jnp.int32, (CHUNK, CHUNK), 0)
    col = lax.broadcasted_iota(jnp.int32, (CHUNK, CHUNK), 1)
    causal = col <= row
    bst = bst_ref[...]
    for h in range(GMLP_HEADS):
        w = jnp.where(causal, ws_ref[h], 0.0).astype(BF16)
        b = bst[:, h:h + 1]
        for c in range(tm // CHUNK):
            vs = vn[c * CHUNK:(c + 1) * CHUNK, h * HEAD_DIM:(h + 1) * HEAD_DIM]
            mixed = jnp.dot(w, vs, preferred_element_type=F32) + b
            o_ref[c * CHUNK:(c + 1) * CHUNK, h * HEAD_DIM:(h + 1) * HEAD_DIM] = (
                u[c * CHUNK:(c + 1) * CHUNK, h * HEAD_DIM:(h + 1) * HEAD_DIM] * mixed).astype(o_ref.dtype)


def _gmlp(z, ln_g, ln_b, w_s, b_s, tm=256):
    return pl.pallas_call(
        functools.partial(_gmlp_kernel, tm=tm),
        grid=(SEQ // tm,),
        in_specs=[
            pl.BlockSpec((tm, GMLP_WIDTH), lambda i: (i, 0)),
            pl.BlockSpec((tm, GMLP_WIDTH), lambda i: (i, 1)),
            pl.BlockSpec((1, GMLP_WIDTH), lambda i: (0, 0)),
            pl.BlockSpec((1, GMLP_WIDTH), lambda i: (0, 0)),
            pl.BlockSpec((GMLP_HEADS, CHUNK, CHUNK), lambda i: (0, 0, 0)),
            pl.BlockSpec((CHUNK, GMLP_HEADS), lambda i: (0, 0)),
        ],
        out_specs=pl.BlockSpec((tm, GMLP_WIDTH), lambda i: (i, 0)),
        out_shape=jax.ShapeDtypeStruct((SEQ, GMLP_WIDTH), BF16),
        compiler_params=_params(("arbitrary",), 40),
        name="gmlp",
    )(z, z, ln_g.reshape(1, -1), ln_b.reshape(1, -1), w_s, b_s.T)


def _qk_prep_kernel(pos_ref, freq_ref, sign_ref, gq_ref, gk_ref, zq_ref, zk_ref, q_ref, k_ref):
    ang = pos_ref[...].astype(F32) * freq_ref[...]
    cos = jnp.cos(ang)
    sin = jnp.sin(ang) * sign_ref[...]
    scale = HEAD_DIM ** -0.5 * math.log2(math.e)

    pair = 2 * HEAD_DIM
    r2 = lax.broadcasted_iota(jnp.int32, (pair, pair), 0) // HEAD_DIM
    c2 = lax.broadcasted_iota(jnp.int32, (pair, pair), 1) // HEAD_DIM
    ones = jnp.where(r2 == c2, 1.0, 0.0).astype(BF16)

    def prep(z_ref, g_ref, o_ref, mult):
        g = g_ref[...]
        a = cos * (g * mult)
        b = sin * (pltpu.roll(g, HEAD_DIM // 2, 1) * mult)
        for s in range(DIFF_QK_WIDTH // pair):
            x = z_ref[:, s * pair:(s + 1) * pair]
            sq = x * x
            hi = sq.astype(BF16)
            lo = (sq - hi.astype(F32)).astype(BF16)
            ssum = jnp.dot(hi, ones, preferred_element_type=F32) + jnp.dot(lo, ones, preferred_element_type=F32)
            for t in range(2):
                cols = slice(t * HEAD_DIM, (t + 1) * HEAD_DIM)
                seg = x[:, cols]
                r = lax.rsqrt(ssum[:, cols] * (1.0 / HEAD_DIM) + EPS)
                out = (seg * a + pltpu.roll(seg, HEAD_DIM // 2, 1) * b) * r
                o_ref[:, s * pair + t * HEAD_DIM:s * pair + (t + 1) * HEAD_DIM] = out.astype(o_ref.dtype)

    prep(zq_ref, gq_ref, q_ref, scale)
    prep(zk_ref, gk_ref, k_ref, 1.0)


def _qk_prep(z, positions, gq, gk, tm=256):
    half = jnp.arange(0, HEAD_DIM, 2, dtype=F32) / HEAD_DIM
    inv_freq = 1.0 / (ROPE_THETA ** half)
    freq = jnp.concatenate([inv_freq, inv_freq]).reshape(1, HEAD_DIM)
    sign = jnp.concatenate([-jnp.ones((HEAD_DIM // 2,), F32), jnp.ones((HEAD_DIM // 2,), F32)]).reshape(1, HEAD_DIM)
    wide = DIFF_QK_WIDTH
    row_spec = lambda cb: pl.BlockSpec((tm, wide), lambda i: (i, cb))
    vec = pl.BlockSpec((1, HEAD_DIM), lambda i: (0, 0))
    out_spec = pl.BlockSpec((tm, wide), lambda i: (i, 0))
    shp = jax.ShapeDtypeStruct((SEQ, wide), BF16)
    return pl.pallas_call(
        _qk_prep_kernel,
        grid=(SEQ // tm,),
        in_specs=[pl.BlockSpec((tm, 1), lambda i: (i, 0)), vec, vec, vec, vec,
                  row_spec(COL_Q // wide), row_spec(COL_K // wide)],
        out_specs=[out_spec, out_spec],
        out_shape=[shp, shp],
        compiler_params=_params(("arbitrary",), 40),
        name="qk_prep",
    )(positions.reshape(SEQ, 1), freq, sign, gq.reshape(1, -1), gk.reshape(1, -1), z, z)


KV_UNROLL = 4
def _diff_attn_kernel(lq1_ref, lk1_ref, lq2_ref, lk2_ref, sg_ref, q_ref, k_ref, v_ref, o_ref,
                      m_ref, l_ref, acc_ref, *, tq, lam_init):
    i = pl.program_id(1)
    m_ref[...] = jnp.full(m_ref.shape, -jnp.inf, F32)
    l_ref[...] = jnp.zeros(l_ref.shape, F32)
    acc_ref[...] = jnp.zeros(acc_ref.shape, F32)
    lanes = tq // 128

    def block(j, masked):
        start = pl.multiple_of(j * tq, tq)
        kb = k_ref[pl.ds(start, tq), :]
        vb = v_ref[pl.ds(start, tq), :]
        for a in range(2):
            q = q_ref[:, a * HEAD_DIM:(a + 1) * HEAD_DIM]
            k = kb[:, a * HEAD_DIM:(a + 1) * HEAD_DIM]
            s = lax.dot_general(q, k, (((1,), (1,)), ((), ())), preferred_element_type=F32)
            if masked:
                row = lax.broadcasted_iota(jnp.int32, (tq, tq), 0)
                col = lax.broadcasted_iota(jnp.int32, (tq, tq), 1)
                s = jnp.where(col <= row, s, -jnp.inf)
            m_prev = m_ref[a]
            m_new = jnp.maximum(m_prev, jnp.max(s, axis=-1, keepdims=True))
            alpha = jnp.exp2(m_prev - m_new)
            p = jnp.exp2(s - jnp.concatenate([m_new] * lanes, axis=1))
            psum = p[:, 0:128]
            for c in range(1, lanes):
                psum = psum + p[:, c * 128:(c + 1) * 128]
            l_ref[a] = alpha * l_ref[a] + psum
            acc_ref[a] = (jnp.concatenate([alpha] * (DIFF_V_DIM // 128), axis=1) * acc_ref[a]
                          + jnp.dot(p.astype(BF16), vb, preferred_element_type=F32))
            m_ref[a] = m_new

    def group(g, carry):
        for u in range(KV_UNROLL):
            block(KV_UNROLL * g + u, False)
        return carry

    groups = i // KV_UNROLL
    lax.fori_loop(0, groups, group, 0)
    rem = i - groups * KV_UNROLL
    for r in range(KV_UNROLL):
        @pl.when(rem == r)
        def _(r=r):
            for u in range(r):
                block(groups * KV_UNROLL + u, False)
            block(i, True)

    lam = (jnp.exp(jnp.sum(lq1_ref[...] * lk1_ref[...], axis=-1, keepdims=True))
           - jnp.exp(jnp.sum(lq2_ref[...] * lk2_ref[...], axis=-1, keepdims=True)) + lam_init)
    l1 = jnp.sum(l_ref[0], axis=-1, keepdims=True)
    l2 = jnp.sum(l_ref[1], axis=-1, keepdims=True)
    o = acc_ref[0] / l1 - lam * (acc_ref[1] / l2)
    o_ref[...] = (_rms(o, sg_ref[...]) * (1.0 - lam_init)).astype(o_ref.dtype)


def _diff_attn(q, k, v, lq1, lk1, lq2, lk2, subln_g, lam_init, tq=512):
    vec = pl.BlockSpec((1, HEAD_DIM), lambda h, i: (0, 0))
    kv_spec = pl.BlockSpec((SEQ, DIFF_V_DIM), lambda h, i: (0, h))
    return pl.pallas_call(
        functools.partial(_diff_attn_kernel, tq=tq, lam_init=lam_init),
        grid=(DIFF_HEADS, SEQ // tq),
        in_specs=[vec, vec, vec, vec, pl.BlockSpec((1, DIFF_V_DIM), lambda h, i: (0, 0)),
                  pl.BlockSpec((tq, DIFF_V_DIM), lambda h, i: (i, h)), kv_spec, kv_spec],
        out_specs=pl.BlockSpec((tq, DIFF_V_DIM), lambda h, i: (i, h)),
        out_shape=jax.ShapeDtypeStruct((SEQ, DIFF_WIDTH), BF16),
        scratch_shapes=[pltpu.VMEM((2, tq, 128), F32), pltpu.VMEM((2, tq, 128), F32),
                        pltpu.VMEM((2, tq, DIFF_V_DIM), F32)],
        compiler_params=_params(("arbitrary", "arbitrary"), 48),
        name="diff_attn",
    )(lq1.reshape(1, -1), lk1.reshape(1, -1), lq2.reshape(1, -1), lk2.reshape(1, -1),
      subln_g.reshape(1, -1), q, k, v)


def _mem_attn_kernel(zm_ref, kv_ref, gq_ref, gk_ref, o_ref):
    scale = MEM_HEAD_DIM ** -0.5
    for h in range(MEM_HEADS):
        lo, hi = h * MEM_HEAD_DIM, (h + 1) * MEM_HEAD_DIM
        q = (_rms(zm_ref[:, lo:hi], gq_ref[...]) * scale).astype(BF16)
        k = _rms(kv_ref[:, lo:hi], gk_ref[...]).astype(BF16)
        v = kv_ref[:, MEM_WIDTH + lo:MEM_WIDTH + hi].astype(BF16)
        s = lax.dot_general(q, k, (((1,), (1,)), ((), ())), preferred_element_type=F32)
        p = jnp.exp(s - jnp.max(s, axis=-1, keepdims=True))
        l = jnp.sum(p, axis=-1, keepdims=True)
        o = jnp.dot(p.astype(BF16), v, preferred_element_type=F32) / l
        o_ref[:, lo:hi] = o.astype(o_ref.dtype)


def _mem_attn(z, kv, gq, gk, tm=512):
    vec = pl.BlockSpec((1, MEM_HEAD_DIM), lambda i: (0, 0))
    return pl.pallas_call(
        _mem_attn_kernel,
        grid=(SEQ // tm,),
        in_specs=[pl.BlockSpec((tm, MEM_WIDTH), lambda i: (i, 0)),
                  pl.BlockSpec((MEM_LEN, 2 * MEM_WIDTH), lambda i: (0, 0)), vec, vec],
        out_specs=pl.BlockSpec((tm, MEM_WIDTH), lambda i: (i, 0)),
        out_shape=jax.ShapeDtypeStruct((SEQ, MEM_WIDTH), BF16),
        compiler_params=_params(("arbitrary",), 40),
        name="mem_attn",
    )(z, kv, gq.reshape(1, -1), gk.reshape(1, -1))


def _router_kernel(x_ref, g_ref, wr_ref, br_ref, h_ref, gate_ref, exp_ref, rank_ref, cnt_ref, carry_ref, *, tm):
    @pl.when(pl.program_id(0) == 0)
    def _():
        carry_ref[...] = jnp.zeros(carry_ref.shape, F32)

    h = _rms(x_ref[...], g_ref[...])
    _store_slabs(h_ref, h, tm)
    w = wr_ref[...]
    h_hi = h.astype(BF16)
    h_lo = (h - h_hi.astype(F32)).astype(BF16)
    w_hi = w.astype(BF16)
    w_lo = (w - w_hi.astype(F32)).astype(BF16)
    parts = jnp.dot(jnp.concatenate([h_hi, h_lo], axis=0), jnp.concatenate([w_hi, w_lo], axis=1),
                    preferred_element_type=F32)
    logits = ((parts[:tm, :N_EXPERTS] + parts[:tm, N_EXPERTS:]) + (parts[tm:, :N_EXPERTS] + parts[tm:, N_EXPERTS:])
              + br_ref[...])
    lane = lax.broadcasted_iota(jnp.int32, logits.shape, 1)
    slot = lax.broadcasted_iota(jnp.int32, (tm, TOP_K), 1)
    work = logits
    sel = jnp.zeros(logits.shape, jnp.bool_)
    hits = []
    top = None
    nums = jnp.zeros((tm, TOP_K), F32)
    experts = jnp.zeros((tm, TOP_K), jnp.int32)
    denom = jnp.zeros((tm, 1), F32)
    for k in range(TOP_K):
        m = jnp.max(work, axis=-1, keepdims=True)
        idx = jnp.min(jnp.where(work == m, lane, N_EXPERTS), axis=-1, keepdims=True)
        hit = lane == idx
        if k == 0:
            top = m
        e = jnp.exp(m - top)
        nums = jnp.where(slot == k, e, nums)
        experts = jnp.where(slot == k, idx, experts)
        denom = denom + e
        work = jnp.where(hit, -jnp.inf, work)
        sel = jnp.logical_or(sel, hit)
        hits.append(hit)
    gate_ref[...] = nums / denom
    exp_ref[...] = experts
    r = lax.broadcasted_iota(jnp.int32, (tm, tm), 0)
    c = lax.broadcasted_iota(jnp.int32, (tm, tm), 1)
    strict = jnp.where(c < r, 1.0, 0.0).astype(BF16)
    chosen = jnp.where(sel, 1.0, 0.0)
    rank = jnp.dot(strict, chosen.astype(BF16), preferred_element_type=F32) + carry_ref[...]
    ranks = jnp.zeros((tm, TOP_K), F32)
    for k in range(TOP_K):
        ranks = jnp.where(slot == k, jnp.sum(jnp.where(hits[k], rank, 0.0), axis=-1, keepdims=True), ranks)
    rank_ref[...] = ranks.astype(jnp.int32)
    carry_ref[...] = carry_ref[...] + jnp.sum(chosen, axis=0, keepdims=True)
    cnt_ref[...] = carry_ref[...].astype(jnp.int32)


def _router(x, g, w_router, b_router, tm=256):
    small = lambda: pl.BlockSpec((tm, TOP_K), lambda i: (i, 0))
    return pl.pallas_call(
        functools.partial(_router_kernel, tm=tm),
        grid=(SEQ // tm,),
        in_specs=[pl.BlockSpec((tm, D_MODEL), lambda i: (i, 0)), pl.BlockSpec((1, D_MODEL), lambda i: (0, 0)),
                  pl.BlockSpec((D_MODEL, N_EXPERTS), lambda i: (0, 0)), pl.BlockSpec((1, N_EXPERTS), lambda i: (0, 0))],
        out_specs=[pl.BlockSpec((tm * SLAB_ROWS, SLAB_LANES), lambda i: (i, 0)), small(), small(), small(),
                   pl.BlockSpec((1, N_EXPERTS), lambda i: (0, 0))],
        out_shape=[jax.ShapeDtypeStruct((SEQ * SLAB_ROWS, SLAB_LANES), jnp.uint32),
                   jax.ShapeDtypeStruct((SEQ, TOP_K), F32),
                   jax.ShapeDtypeStruct((SEQ, TOP_K), jnp.int32),
                   jax.ShapeDtypeStruct((SEQ, TOP_K), jnp.int32),
                   jax.ShapeDtypeStruct((1, N_EXPERTS), jnp.int32)],
        scratch_shapes=[pltpu.VMEM((1, N_EXPERTS), F32)],
        compiler_params=_params(("arbitrary",), 40),
        name="router",
    )(x, g.reshape(1, -1), w_router, b_router.reshape(1, -1))


DMA_PRIORITIES = 2


def _slot_copy(src_ref, src_row, dst_ref, dst_row, sem):
    return pltpu.make_async_copy(src_ref.at[pl.ds(src_row * SLAB_ROWS, SLAB_ROWS)],
                                 dst_ref.at[pl.ds(dst_row * SLAB_ROWS, SLAB_ROWS)], sem)


def _tile_fill_copy(zero_ref, dst_ref, tile, sem):
    rows = MOE_TILE * SLAB_ROWS
    return pltpu.make_async_copy(zero_ref, dst_ref.at[pl.ds(tile * rows, rows)], sem)


def _dispatch_kernel(pstart_ref, fill_ref, exp_ref, rank_ref, h_hbm, o_ref, slot_ref,
                     zero_ref, buf_a, buf_b, fill_sem, in_sem, out_sem, *, tc):
    i = pl.program_id(0)
    n = pl.num_programs(0)
    rows = tc * SLAB_ROWS

    def load(tile, buf, s):
        return pltpu.make_async_copy(h_hbm.at[pl.ds(tile * rows, rows)], buf, in_sem.at[s])

    def wait_rows(buf, s):
        for _ in range(TOP_K):
            pltpu.make_async_copy(buf, o_ref.at[pl.ds(0, rows)], out_sem.at[s]).wait()

    @pl.when(i == 0)
    def _():
        load(0, buf_a, 0).start()
        zero_ref[...] = jnp.zeros(zero_ref.shape, zero_ref.dtype)

        def fill(t, carry):
            @pl.when(fill_ref[t] != 0)
            def _():
                _tile_fill_copy(zero_ref, o_ref, t, fill_sem).start()
            return carry

        def fill_wait(t, carry):
            @pl.when(fill_ref[t] != 0)
            def _():
                _tile_fill_copy(zero_ref, o_ref, t, fill_sem).wait()
            return carry

        lax.fori_loop(0, MOE_NT, fill, 0)
        lax.fori_loop(0, MOE_NT, fill_wait, 0)

    def step(cur_buf, cs, nxt_buf, ns):
        @pl.when(i > 0)
        def _():
            wait_rows(nxt_buf, ns)

        @pl.when(i + 1 < n)
        def _():
            load(i + 1, nxt_buf, ns).start()

        load(i, cur_buf, cs).wait()

        def issue(r, carry):
            for k in range(TOP_K):
                a = r * TOP_K + k
                slot = pstart_ref[exp_ref[0, 0, a]] + rank_ref[0, 0, a]
                slot_ref[0, 0, a] = slot
                _slot_copy(cur_buf, r, o_ref, slot, out_sem.at[cs]).start(priority=k % DMA_PRIORITIES)
            return carry

        lax.fori_loop(0, tc, issue, 0, unroll=4)

        @pl.when(i == n - 1)
        def _():
            wait_rows(cur_buf, cs)

    @pl.when(i % 2 == 0)
    def _():
        step(buf_a, 0, buf_b, 1)

    @pl.when(i % 2 == 1)
    def _():
        step(buf_b, 1, buf_a, 0)


def _dispatch(hp, experts4, rank4, pstart, fill_tiles, tc=128):
    nb = SEQ // tc
    idx_spec = pl.BlockSpec((1, 1, tc * TOP_K), lambda i, ps, ft: (i, 0, 0), memory_space=pltpu.SMEM)
    return pl.pallas_call(
        functools.partial(_dispatch_kernel, tc=tc),
        grid_spec=pltpu.PrefetchScalarGridSpec(
            num_scalar_prefetch=2,
            grid=(nb,),
            in_specs=[idx_spec, idx_spec, pl.BlockSpec(memory_space=pl.ANY)],
            out_specs=[pl.BlockSpec(memory_space=pl.ANY), idx_spec],
            scratch_shapes=[pltpu.VMEM((MOE_TILE * SLAB_ROWS, SLAB_LANES), jnp.uint32),
                            pltpu.VMEM((tc * SLAB_ROWS, SLAB_LANES), jnp.uint32),
                            pltpu.VMEM((tc * SLAB_ROWS, SLAB_LANES), jnp.uint32),
                            pltpu.SemaphoreType.DMA(()), pltpu.SemaphoreType.DMA((2,)), pltpu.SemaphoreType.DMA((2,))],
        ),
        out_shape=[jax.ShapeDtypeStruct((MOE_ROWS * SLAB_ROWS, SLAB_LANES), jnp.uint32),
                   jax.ShapeDtypeStruct((nb, 1, tc * TOP_K), jnp.int32)],
        compiler_params=_params(("arbitrary",), 40),
        name="moe_dispatch",
    )(pstart, fill_tiles, experts4.reshape(nb, 1, tc * TOP_K), rank4.reshape(nb, 1, tc * TOP_K), hp)


def _group_edges(blk_ref, t):
    e = blk_ref[t]
    first = jnp.logical_or(t == 0, blk_ref[jnp.maximum(t - 1, 0)] != e)
    last = jnp.logical_or(t == MOE_NT - 1, blk_ref[jnp.minimum(t + 1, MOE_NT - 1)] != e)
    return e, first, last


def _stream_weights(blk_ref, nxt_ref, slot_ref, n_pass, copies, t):
    j = pl.program_id(0)
    e, first, last = _group_edges(blk_ref, t)

    @pl.when(jnp.logical_and(j == 0, t == 0))
    def _():
        slot_ref[0] = 0
        for c in copies(e, j, 0):
            c.start()

    @pl.when(first)
    def _():
        slot = slot_ref[0]
        for c in copies(e, j, slot):
            c.wait()
        nt = nxt_ref[t]
        same_pass = nt < MOE_NT
        ne = blk_ref[jnp.where(same_pass, nt, 0)]
        nj = jnp.where(same_pass, j, j + 1)

        @pl.when(jnp.logical_or(same_pass, j + 1 < n_pass))
        def _():
            for c in copies(ne, nj, 1 - slot):
                c.start()

    slot = slot_ref[0]

    @pl.when(last)
    def _():
        slot_ref[0] = 1 - slot

    return slot


def _pair_step(blk_ref, nxt_ref, nused_ref, slot_ref, n_pass, copies, compute, clear):
    ta = 2 * pl.program_id(1)
    tb = ta + 1
    ea = blk_ref[ta]
    eb = blk_ref[tb]
    va = ta < nused_ref[0]
    vb = tb < nused_ref[0]
    merged = jnp.logical_and(vb, ea == eb)

    slot_a = _stream_weights(blk_ref, nxt_ref, slot_ref, n_pass, copies, ta)

    @pl.when(merged)
    def _():
        compute(0, 2 * MOE_TILE, ea, slot_a)

    @pl.when(jnp.logical_and(va, jnp.logical_not(merged)))
    def _():
        compute(0, MOE_TILE, ea, slot_a)

    slot_b = _stream_weights(blk_ref, nxt_ref, slot_ref, n_pass, copies, tb)

    @pl.when(jnp.logical_and(vb, jnp.logical_not(merged)))
    def _():
        compute(MOE_TILE, MOE_TILE, eb, slot_b)

    @pl.when(jnp.logical_not(va))
    def _():
        clear(0, MOE_TILE)

    @pl.when(jnp.logical_not(vb))
    def _():
        clear(MOE_TILE, MOE_TILE)


def _moe_up_kernel(blk_ref, nxt_ref, nused_ref, x_ref, w_hbm, b_ref, o_ref, wbuf, sem, slot_ref, *, tn):
    j = pl.program_id(0)

    def copies(e, j, slot):
        return [pltpu.make_async_copy(w_hbm.at[e, :, pl.ds(pl.multiple_of(part * D_FF + j * tn, tn), tn)],
                                      wbuf.at[slot, part], sem.at[slot, part]) for part in range(2)]

    def compute(row0, rows, e, slot):
        x = _load_slabs(x_ref, row0, rows)

        def proj(part):
            bias = b_ref[pl.ds(e, 1), pl.ds(pl.multiple_of(part * D_FF + j * tn, tn), tn)]
            return jnp.dot(x, wbuf[slot, part].astype(BF16), preferred_element_type=F32) + bias

        gate = jnp.minimum(proj(0), SWIGLU_LIMIT)
        up = jnp.clip(proj(1), -SWIGLU_LIMIT, SWIGLU_LIMIT)
        act = (up + 1.0) * (gate * jax.nn.sigmoid(SWIGLU_ALPHA * gate))
        o_ref[row0:row0 + rows, :] = act.astype(o_ref.dtype)

    def clear(row0, rows):
        o_ref[row0:row0 + rows, :] = jnp.zeros((rows, tn), o_ref.dtype)

    _pair_step(blk_ref, nxt_ref, nused_ref, slot_ref, D_FF // tn, copies, compute, clear)


def _moe_up(xs, blk_e, nxt, n_used, w_gate_up, b_gate_up, tn=512):
    nj = D_FF // tn
    last = lambda u, nu: jnp.minimum(u, (nu[0] - 1) // 2)
    return pl.pallas_call(
        functools.partial(_moe_up_kernel, tn=tn),
        grid_spec=pltpu.PrefetchScalarGridSpec(
            num_scalar_prefetch=3,
            grid=(nj, MOE_NT // 2),
            in_specs=[
                pl.BlockSpec((2 * MOE_TILE * SLAB_ROWS, SLAB_LANES), lambda j, u, be, nx, nu: (last(u, nu), 0)),
                pl.BlockSpec(memory_space=pl.ANY),
                pl.BlockSpec((N_EXPERTS, 2 * D_FF), lambda j, u, be, nx, nu: (0, 0)),
            ],
            out_specs=pl.BlockSpec((2 * MOE_TILE, tn), lambda j, u, be, nx, nu: (u, j)),
            scratch_shapes=[pltpu.VMEM((2, 2, D_MODEL, tn), F32), pltpu.SemaphoreType.DMA((2, 2)),
                            pltpu.SMEM((1,), jnp.int32)],
        ),
        out_shape=jax.ShapeDtypeStruct((MOE_ROWS, D_FF), BF16),
        compiler_params=_params(("arbitrary", "arbitrary"), 56),
        name="moe_up",
    )(blk_e, nxt, n_used, xs, w_gate_up, b_gate_up)


DOWN_TN = D_MODEL // 2
DOWN_WORDS = DOWN_TN // 2
DOWN_ROWS = DOWN_WORDS // SLAB_LANES


def _moe_down_kernel(blk_ref, nxt_ref, nused_ref, a_ref, w_hbm, b_ref, o_ref, wbuf, sem, slot_ref):
    def copies(e, j, slot):
        return [pltpu.make_async_copy(w_hbm.at[e, :, pl.ds(pl.multiple_of(j * DOWN_TN, DOWN_TN), DOWN_TN)],
                                      wbuf.at[slot], sem.at[slot])]

    def compute(row0, rows, e, slot):
        bias = b_ref[pl.ds(e, 1), pl.ds(pl.multiple_of(pl.program_id(0) * DOWN_TN, DOWN_TN), DOWN_TN)]
        y = jnp.dot(a_ref[row0:row0 + rows, :], wbuf[slot].astype(BF16), preferred_element_type=F32) + bias
        words = pltpu.pack_elementwise([y[:, :DOWN_WORDS], y[:, DOWN_WORDS:]], packed_dtype=BF16)
        flat = o_ref.reshape(2 * MOE_TILE * DOWN_ROWS, SLAB_LANES)
        for s in range(DOWN_ROWS):
            flat[pl.ds(row0 * DOWN_ROWS + s, rows, stride=DOWN_ROWS), :] = words[:, s * SLAB_LANES:(s + 1) * SLAB_LANES]

    def clear(row0, rows):
        o_ref[row0:row0 + rows] = jnp.zeros((rows, DOWN_ROWS, SLAB_LANES), o_ref.dtype)

    _pair_step(blk_ref, nxt_ref, nused_ref, slot_ref, D_MODEL // DOWN_TN, copies, compute, clear)


def _moe_down(act, blk_e, nxt, n_used, w_down, b_down):
    nj = D_MODEL // DOWN_TN
    last = lambda u, nu: jnp.minimum(u, (nu[0] - 1) // 2)
    return pl.pallas_call(
        _moe_down_kernel,
        grid_spec=pltpu.PrefetchScalarGridSpec(
            num_scalar_prefetch=3,
            grid=(nj, MOE_NT // 2),
            in_specs=[
                pl.BlockSpec((2 * MOE_TILE, D_FF), lambda j, u, be, nx, nu: (last(u, nu), 0)),
                pl.BlockSpec(memory_space=pl.ANY),
                pl.BlockSpec((N_EXPERTS, D_MODEL), lambda j, u, be, nx, nu: (0, 0)),
            ],
            out_specs=pl.BlockSpec((2 * MOE_TILE, DOWN_ROWS, SLAB_LANES), lambda j, u, be, nx, nu: (u, j, 0)),
            scratch_shapes=[pltpu.VMEM((2, D_FF, DOWN_TN), F32), pltpu.SemaphoreType.DMA((2,)),
                            pltpu.SMEM((1,), jnp.int32)],
        ),
        out_shape=jax.ShapeDtypeStruct((MOE_ROWS, SLAB_ROWS, SLAB_LANES), jnp.uint32),
        compiler_params=_params(("arbitrary", "arbitrary"), 56),
        name="moe_down",
    )(blk_e, nxt, n_used, act, w_down, b_down).reshape(MOE_ROWS * SLAB_ROWS, SLAB_LANES)


COMBINE_ROWS = 32


def _combine_kernel(cur_ref, nxt_ref, ys_ref, x_ref, g_ref, o_ref, buf_a, buf_b, sem, *, tc):
    i = pl.program_id(0)

    def fetch(idx_ref, a, buf, slot):
        r, k = divmod(a, TOP_K)
        return _slot_copy(ys_ref, idx_ref[0, 0, a], buf.at[k], r, sem.at[slot])

    def wait_tile(buf, slot):
        for k in range(TOP_K):
            pltpu.make_async_copy(ys_ref.at[pl.ds(0, tc * SLAB_ROWS)], buf.at[k], sem.at[slot]).wait()

    @pl.when(i == 0)
    def _():
        def issue(r, carry):
            for k in range(TOP_K):
                _slot_copy(ys_ref, cur_ref[0, 0, r * TOP_K + k], buf_a.at[k], r, sem.at[0]).start(
                    priority=k % DMA_PRIORITIES)
            return carry

        lax.fori_loop(0, tc, issue, 0, unroll=4)

    def step(cur_buf, cur_slot, nxt_buf, nxt_slot):
        wait_tile(cur_buf, cur_slot)
        n_chunks = (tc // COMBINE_ROWS) * SLAB_ROWS
        per_chunk = tc * TOP_K // n_chunks
        a = 0
        for t0 in range(0, tc, COMBINE_ROWS):
            rows = slice(t0, t0 + COMBINE_ROWS)
            gates = [jnp.broadcast_to(g_ref[rows, k:k + 1], (COMBINE_ROWS, SLAB_LANES)) for k in range(TOP_K)]
            for s in range(SLAB_ROWS):
                for _ in range(per_chunk):
                    fetch(nxt_ref, a, nxt_buf, nxt_slot).start(priority=a % DMA_PRIORITIES)
                    a += 1
                lo_sum = None
                hi_sum = None
                for k in range(TOP_K):
                    words = cur_buf[k, pl.ds(t0 * SLAB_ROWS + s, COMBINE_ROWS, stride=SLAB_ROWS), :]
                    lo, hi = _unpack_bf16(words)
                    lo_sum = gates[k] * lo if lo_sum is None else lo_sum + gates[k] * lo
                    hi_sum = gates[k] * hi if hi_sum is None else hi_sum + gates[k] * hi
                c = (s // DOWN_ROWS) * DOWN_TN + (s % DOWN_ROWS) * SLAB_LANES
                o_ref[rows, c:c + SLAB_LANES] = x_ref[rows, c:c + SLAB_LANES] + lo_sum
                c += DOWN_WORDS
                o_ref[rows, c:c + SLAB_LANES] = x_ref[rows, c:c + SLAB_LANES] + hi_sum

        @pl.when(i == pl.num_programs(0) - 1)
        def _():
            wait_tile(nxt_buf, nxt_slot)

    @pl.when(i % 2 == 0)
    def _():
        step(buf_a, 0, buf_b, 1)

    @pl.when(i % 2 == 1)
    def _():
        step(buf_b, 1, buf_a, 0)


def _combine(ys, x2, slots, gates4, tc=128):
    nb = SEQ // tc
    idx_spec = lambda f: pl.BlockSpec((1, 1, tc * TOP_K), lambda i: (f(i), 0, 0), memory_space=pltpu.SMEM)
    return pl.pallas_call(
        functools.partial(_combine_kernel, tc=tc),
        grid=(nb,),
        in_specs=[idx_spec(lambda i: i), idx_spec(lambda i: jnp.minimum(i + 1, nb - 1)),
                  pl.BlockSpec(memory_space=pl.ANY),
                  pl.BlockSpec((tc, D_MODEL), lambda i: (i, 0)),
                  pl.BlockSpec((tc, TOP_K), lambda i: (i, 0))],
        out_specs=pl.BlockSpec((tc, D_MODEL), lambda i: (i, 0)),
        out_shape=jax.ShapeDtypeStruct((SEQ, D_MODEL), F32),
        scratch_shapes=[pltpu.VMEM((TOP_K, tc * SLAB_ROWS, SLAB_LANES), jnp.uint32),
                        pltpu.VMEM((TOP_K, tc * SLAB_ROWS, SLAB_LANES), jnp.uint32), pltpu.SemaphoreType.DMA((2,))],
        compiler_params=_params(("arbitrary",), 40),
        name="moe_combine",
    )(slots, slots, ys, x2, gates4)


def kernel(x, mem, positions, attn_norm_g, w_in, gmlp_ln_g, gmlp_ln_b, gmlp_w_s, gmlp_b_s, diff_q_norm_g, diff_k_norm_g, diff_lambda_q1, diff_lambda_k1, diff_lambda_q2, diff_lambda_k2, diff_subln_g, mem_norm_g, w_mem_kv, mem_q_norm_g, mem_k_norm_g, w_out, ffn_norm_g, w_router, b_router, w_gate_up, b_gate_up, w_down, b_down):
    depth = attn_norm_g.shape[0]
    xs = x.reshape(SEQ, D_MODEL)
    mem2 = mem.reshape(MEM_LEN, D_MODEL)
    for i in range(depth):
        lam_init = 0.8 - 0.6 * math.exp(-0.3 * i)
        h = _rmsnorm_rows(xs, attn_norm_g[i], tm=256)

        def in_proj(name, col0, width, dtype):
            return _matmul([h], w_in[i], None, dtype, tm=1024, tn=512, name=name, cols=(col0, width))

        z_gqk = in_proj("in_proj_gqk", 0, COL_V, F32)
        v = in_proj("in_proj_v", COL_V, DIFF_WIDTH, BF16)
        z_m = in_proj("in_proj_mem", COL_M, MEM_WIDTH, F32)

        y_g = _gmlp(z_gqk, gmlp_ln_g[i], gmlp_ln_b[i], gmlp_w_s[i], gmlp_b_s[i])

        q, k = _qk_prep(z_gqk, positions, diff_q_norm_g[i], diff_k_norm_g[i])
        y_d = _diff_attn(q, k, v, diff_lambda_q1[i], diff_lambda_k1[i], diff_lambda_q2[i], diff_lambda_k2[i],
                         diff_subln_g[i], lam_init)

        hm = _rmsnorm_rows(mem2, mem_norm_g[i], tm=256)
        kv = _matmul([hm], w_mem_kv[i], None, F32, tm=MEM_LEN, tn=512, name="mem_kv_proj")
        y_m = _mem_attn(z_m, kv, mem_q_norm_g[i], mem_k_norm_g[i])

        x2 = _matmul([y_g, y_d, y_m], w_out[i], xs, F32, tm=1024, tn=512, name="out_proj")

        hp, gates4, experts4, rank4, counts = _router(x2, ffn_norm_g[i], w_router[i], b_router[i])

        counts = counts[0]
        padded = (counts + MOE_TILE - 1) // MOE_TILE * MOE_TILE
        pend = jnp.cumsum(padded)
        pstart = pend - padded
        n_used = (pend[-1] // MOE_TILE).astype(jnp.int32).reshape(1)
        blk_start = jnp.arange(MOE_NT, dtype=jnp.int32) * MOE_TILE
        blk_e = jnp.minimum(jnp.sum(pend[None, :] <= blk_start[:, None], axis=1), N_EXPERTS - 1).astype(jnp.int32)
        blk_end = blk_start + MOE_TILE
        fill_tiles = jnp.logical_or(jnp.any(blk_end[:, None] == pend[None, :], axis=1),
                                    blk_start >= pend[-1]).astype(jnp.int32)

        tile_ids = jnp.arange(MOE_NT, dtype=jnp.int32)
        run_start = jnp.concatenate([jnp.ones((1,), jnp.bool_), blk_e[1:] != blk_e[:-1]])
        starts_from = lax.cummin(jnp.where(run_start, tile_ids, MOE_NT), axis=0, reverse=True)
        nxt = jnp.concatenate([starts_from[1:], jnp.full((1,), MOE_NT, jnp.int32)]).astype(jnp.int32)

        xg, slots = _dispatch(hp, experts4, rank4, pstart, fill_tiles)
        act = _moe_up(xg, blk_e, nxt, n_used, w_gate_up[i], b_gate_up[i])
        ys = _moe_down(act, blk_e, nxt, n_used, w_down[i], b_down[i])
        xs = _combine(ys, x2, slots, gates4)
    return xs.reshape(x.shape)
```

```python
import functools
import math

import jax
import jax.numpy as jnp
from jax import lax
from jax.experimental import pallas as pl
from jax.experimental.pallas import tpu as pltpu

F32 = jnp.float32
BF16 = jnp.bfloat16

D_MODEL = 4096
SEQ = 8192
HEAD_DIM = 128
CHUNK = 128
GMLP_WIDTH = 1024
GMLP_HEADS = 8
DIFF_HEADS = 8
DIFF_V_DIM = 256
DIFF_QK_WIDTH = 2048
DIFF_WIDTH = 2048
MEM_WIDTH = 1024
MEM_HEADS = 4
MEM_HEAD_DIM = 256
MEM_LEN = 256
IN_WIDTH = 9216
N_EXPERTS = 32
TOP_K = 4
D_FF = 1536
SWIGLU_LIMIT = 7.0
SWIGLU_ALPHA = 1.702
ROPE_THETA = 10000.0
EPS = 1e-6

COL_Q = 2 * GMLP_WIDTH
COL_K = COL_Q + DIFF_QK_WIDTH
COL_V = COL_K + DIFF_QK_WIDTH
COL_M = COL_V + DIFF_WIDTH

MOE_TILE = 256
MOE_ROWS = ((SEQ * TOP_K + N_EXPERTS * (MOE_TILE - 1) + MOE_TILE - 1) // MOE_TILE) * MOE_TILE
MOE_NT = MOE_ROWS // MOE_TILE
assert MOE_NT % 2 == 0

LANES = 128
MIB = 1 << 20
VMEM_LIMIT_MIB = 56
VMEM_SMALL_MIB = 40

PROJ_TM = 512
PROJ_TN = 1024


def _params(sem, vmem_mib):
    return pltpu.CompilerParams(dimension_semantics=sem, vmem_limit_bytes=vmem_mib * MIB)


def _rms(x, g):
    ms = jnp.mean(x * x, axis=-1, keepdims=True)
    return x * lax.rsqrt(ms + EPS) * g


SLAB_LANES = 128
SLAB_ROWS = D_MODEL // 2 // SLAB_LANES


def _unpack_bf16(words):
    lo = pltpu.unpack_elementwise(words, index=0, packed_dtype=BF16, unpacked_dtype=F32)
    hi = pltpu.unpack_elementwise(words, index=1, packed_dtype=BF16, unpacked_dtype=F32)
    return lo, hi


def _store_slabs(o_ref, h, rows):
    half = D_MODEL // 2
    for s in range(SLAB_ROWS):
        lo = h[:, s * SLAB_LANES:(s + 1) * SLAB_LANES]
        hi = h[:, half + s * SLAB_LANES:half + (s + 1) * SLAB_LANES]
        o_ref[pl.ds(s, rows, stride=SLAB_ROWS), :] = pltpu.pack_elementwise([lo, hi], packed_dtype=BF16)


def _load_slabs(x_ref, row0, rows):
    los, his = [], []
    for s in range(SLAB_ROWS):
        lo, hi = _unpack_bf16(x_ref[pl.ds(row0 * SLAB_ROWS + s, rows, stride=SLAB_ROWS), :])
        los.append(lo.astype(BF16))
        his.append(hi.astype(BF16))
    return jnp.concatenate(los + his, axis=1)


def _rmsnorm_kernel(x_ref, g_ref, o_ref):
    o_ref[...] = _rms(x_ref[...], g_ref[...]).astype(o_ref.dtype)


def _rmsnorm_rows(x, g, tm):
    m, d = x.shape
    return pl.pallas_call(
        _rmsnorm_kernel,
        grid=(m // tm,),
        in_specs=[pl.BlockSpec((tm, d), lambda i: (i, 0)), pl.BlockSpec((1, d), lambda i: (0, 0))],
        out_specs=pl.BlockSpec((tm, d), lambda i: (i, 0)),
        out_shape=jax.ShapeDtypeStruct((m, d), BF16),
        compiler_params=_params(("arbitrary",), VMEM_SMALL_MIB),
        name="rmsnorm_rows",
    )(x, g.reshape(1, d))


def _matmul_kernel(*refs, widths, has_res):
    n = len(widths)
    lhs = refs[:n]
    w_ref = refs[n]
    res_ref = refs[n + 1] if has_res else None
    o_ref = refs[n + 1 + has_res]

    acc = None
    off = 0
    for r, kw in zip(lhs, widths):
        part = jnp.dot(r[...], w_ref[off:off + kw, :].astype(BF16), preferred_element_type=F32)
        acc = part if acc is None else acc + part
        off += kw
    if has_res:
        acc = acc + res_ref[...]
    o_ref[...] = acc.astype(o_ref.dtype)


def _matmul(lhs_list, w, res, out_dtype, tm, tn, name, cols=None):
    m = lhs_list[0].shape[0]
    k = w.shape[0]
    col0, n = cols if cols is not None else (0, w.shape[1])
    assert col0 % tn == 0 and n % tn == 0
    widths = tuple(a.shape[1] for a in lhs_list)
    assert sum(widths) == k
    in_specs = [pl.BlockSpec((tm, kw), lambda j, i: (i, 0)) for kw in widths]
    in_specs.append(pl.BlockSpec((k, tn), lambda j, i: (0, j + col0 // tn)))
    args = list(lhs_list) + [w]
    if res is not None:
        in_specs.append(pl.BlockSpec((tm, tn), lambda j, i: (i, j)))
        args.append(res)
    return pl.pallas_call(
        functools.partial(_matmul_kernel, widths=widths, has_res=res is not None),
        grid=(n // tn, m // tm),
        in_specs=in_specs,
        out_specs=pl.BlockSpec((tm, tn), lambda j, i: (i, j)),
        out_shape=jax.ShapeDtypeStruct((m, n), out_dtype),
        compiler_params=_params(("arbitrary", "arbitrary"), VMEM_LIMIT_MIB),
        name=name,
    )(*args)


def _gmlp_kernel(zu_ref, zv_ref, lng_ref, lnb_ref, ws_ref, bst_ref, o_ref, *, tm):
    inv_sqrt2 = 1.0 / math.sqrt(2.0)

    def gelu(x):
        return 0.5 * x * (1.0 + lax.erf(x * inv_sqrt2))

    u = gelu(zu_ref[...])
    v = gelu(zv_ref[...])
    mu = jnp.mean(v, axis=-1, keepdims=True)
    vc = v - mu
    var = jnp.mean(vc * vc, axis=-1, keepdims=True)
    vn = (vc * lax.rsqrt(var + EPS) * lng_ref[...] + lnb_ref[...]).astype(BF16)
    row = lax.broadcasted_iota(jnp.int32, (CHUNK, CHUNK), 0)
    col = lax.broadcasted_iota(jnp.int32, (CHUNK, CHUNK), 1)
    causal = col <= row
    bst = bst_ref[...]
    for h in range(GMLP_HEADS):
        w = jnp.where(causal, ws_ref[h], 0.0).astype(BF16)
        b = bst[:, h:h + 1]
        for c in range(tm // CHUNK):
            vs = vn[c * CHUNK:(c + 1) * CHUNK, h * HEAD_DIM:(h + 1) * HEAD_DIM]
            mixed = jnp.dot(w, vs, preferred_element_type=F32) + b
            o_ref[c * CHUNK:(c + 1) * CHUNK, h * HEAD_DIM:(h + 1) * HEAD_DIM] = (
                u[c * CHUNK:(c + 1) * CHUNK, h * HEAD_DIM:(h + 1) * HEAD_DIM] * mixed).astype(o_ref.dtype)


def _gmlp(z, ln_g, ln_b, w_s, b_s, tm=256):
    return pl.pallas_call(
        functools.partial(_gmlp_kernel, tm=tm),
        grid=(SEQ // tm,),
        in_specs=[
            pl.BlockSpec((tm, GMLP_WIDTH), lambda i: (i, 0)),
            pl.BlockSpec((tm, GMLP_WIDTH), lambda i: (i, 1)),
            pl.BlockSpec((1, GMLP_WIDTH), lambda i: (0, 0)),
            pl.BlockSpec((1, GMLP_WIDTH), lambda i: (0, 0)),
            pl.BlockSpec((GMLP_HEADS, CHUNK, CHUNK), lambda i: (0, 0, 0)),
            pl.BlockSpec((CHUNK, GMLP_HEADS), lambda i: (0, 0)),
        ],
        out_specs=pl.BlockSpec((tm, GMLP_WIDTH), lambda i: (i, 0)),
        out_shape=jax.ShapeDtypeStruct((SEQ, GMLP_WIDTH), BF16),
        compiler_params=_params(("arbitrary",), VMEM_SMALL_MIB),
        name="gmlp",
    )(z, z, ln_g.reshape(1, -1), ln_b.reshape(1, -1), w_s, b_s.T)


def _qk_prep_kernel(pos_ref, freq_ref, sign_ref, gq_ref, gk_ref, zq_ref, zk_ref, q_ref, k_ref):
    ang = pos_ref[...].astype(F32) * freq_ref[...]
    cos = jnp.cos(ang)
    sin = jnp.sin(ang) * sign_ref[...]
    scale = HEAD_DIM ** -0.5 * math.log2(math.e)

    pair = 2 * HEAD_DIM
    r2 = lax.broadcasted_iota(jnp.int32, (pair, pair), 0) // HEAD_DIM
    c2 = lax.broadcasted_iota(jnp.int32, (pair, pair), 1) // HEAD_DIM
    ones = jnp.where(r2 == c2, 1.0, 0.0).astype(BF16)

    def prep(z_ref, g_ref, o_ref, mult):
        g = g_ref[...]
        a = cos * (g * mult)
        b = sin * (pltpu.roll(g, HEAD_DIM // 2, 1) * mult)
        for s in range(DIFF_QK_WIDTH // pair):
            x = z_ref[:, s * pair:(s + 1) * pair]
            sq = x * x
            hi = sq.astype(BF16)
            lo = (sq - hi.astype(F32)).astype(BF16)
            ssum = jnp.dot(hi, ones, preferred_element_type=F32) + jnp.dot(lo, ones, preferred_element_type=F32)
            for t in range(2):
                cols = slice(t * HEAD_DIM, (t + 1) * HEAD_DIM)
                seg = x[:, cols]
                r = lax.rsqrt(ssum[:, cols] * (1.0 / HEAD_DIM) + EPS)
                out = (seg * a + pltpu.roll(seg, HEAD_DIM // 2, 1) * b) * r
                o_ref[:, s * pair + t * HEAD_DIM:s * pair + (t + 1) * HEAD_DIM] = out.astype(o_ref.dtype)

    prep(zq_ref, gq_ref, q_ref, scale)
    prep(zk_ref, gk_ref, k_ref, 1.0)


def _qk_prep(z, positions, gq, gk, tm=256):
    half = jnp.arange(0, HEAD_DIM, 2, dtype=F32) / HEAD_DIM
    inv_freq = 1.0 / (ROPE_THETA ** half)
    freq = jnp.concatenate([inv_freq, inv_freq]).reshape(1, HEAD_DIM)
    sign = jnp.concatenate([-jnp.ones((HEAD_DIM // 2,), F32), jnp.ones((HEAD_DIM // 2,), F32)]).reshape(1, HEAD_DIM)
    wide = DIFF_QK_WIDTH
    row_spec = lambda cb: pl.BlockSpec((tm, wide), lambda i: (i, cb))
    vec = pl.BlockSpec((1, HEAD_DIM), lambda i: (0, 0))
    out_spec = pl.BlockSpec((tm, wide), lambda i: (i, 0))
    shp = jax.ShapeDtypeStruct((SEQ, wide), BF16)
    return pl.pallas_call(
        _qk_prep_kernel,
        grid=(SEQ // tm,),
        in_specs=[pl.BlockSpec((tm, 1), lambda i: (i, 0)), vec, vec, vec, vec,
                  row_spec(COL_Q // wide), row_spec(COL_K // wide)],
        out_specs=[out_spec, out_spec],
        out_shape=[shp, shp],
        compiler_params=_params(("arbitrary",), VMEM_SMALL_MIB),
        name="qk_prep",
    )(positions.reshape(SEQ, 1), freq, sign, gq.reshape(1, -1), gk.reshape(1, -1), z, z)


KV_UNROLL = 4
def _diff_attn_kernel(lq1_ref, lk1_ref, lq2_ref, lk2_ref, sg_ref, q_ref, k_ref, v_ref, o_ref,
                      m_ref, l_ref, acc_ref, *, tq, lam_init):
    i = pl.program_id(1)
    m_ref[...] = jnp.full(m_ref.shape, -jnp.inf, F32)
    l_ref[...] = jnp.zeros(l_ref.shape, F32)
    acc_ref[...] = jnp.zeros(acc_ref.shape, F32)
    lanes = tq // LANES

    def block(j, masked):
        start = pl.multiple_of(j * tq, tq)
        kb = k_ref[pl.ds(start, tq), :]
        vb = v_ref[pl.ds(start, tq), :]
        for a in range(2):
            q = q_ref[:, a * HEAD_DIM:(a + 1) * HEAD_DIM]
            k = kb[:, a * HEAD_DIM:(a + 1) * HEAD_DIM]
            s = lax.dot_general(q, k, (((1,), (1,)), ((), ())), preferred_element_type=F32)
            if masked:
                row = lax.broadcasted_iota(jnp.int32, (tq, tq), 0)
                col = lax.broadcasted_iota(jnp.int32, (tq, tq), 1)
                s = jnp.where(col <= row, s, -jnp.inf)
            m_prev = m_ref[a]
            m_new = jnp.maximum(m_prev, jnp.max(s, axis=-1, keepdims=True))
            alpha = jnp.exp2(m_prev - m_new)
            p = jnp.exp2(s - jnp.concatenate([m_new] * lanes, axis=1))
            psum = p[:, 0:LANES]
            for c in range(1, lanes):
                psum = psum + p[:, c * LANES:(c + 1) * LANES]
            l_ref[a] = alpha * l_ref[a] + psum
            acc_ref[a] = (jnp.concatenate([alpha] * (DIFF_V_DIM // LANES), axis=1) * acc_ref[a]
                          + jnp.dot(p.astype(BF16), vb, preferred_element_type=F32))
            m_ref[a] = m_new

    def group(g, carry):
        for u in range(KV_UNROLL):
            block(KV_UNROLL * g + u, False)
        return carry

    groups = i // KV_UNROLL
    lax.fori_loop(0, groups, group, 0)
    rem = i - groups * KV_UNROLL
    for r in range(KV_UNROLL):
        @pl.when(rem == r)
        def _(r=r):
            for u in range(r):
                block(groups * KV_UNROLL + u, False)
            block(i, True)

    lam = (jnp.exp(jnp.sum(lq1_ref[...] * lk1_ref[...], axis=-1, keepdims=True))
           - jnp.exp(jnp.sum(lq2_ref[...] * lk2_ref[...], axis=-1, keepdims=True)) + lam_init)
    l1 = jnp.sum(l_ref[0], axis=-1, keepdims=True)
    l2 = jnp.sum(l_ref[1], axis=-1, keepdims=True)
    o = acc_ref[0] / l1 - lam * (acc_ref[1] / l2)
    o_ref[...] = (_rms(o, sg_ref[...]) * (1.0 - lam_init)).astype(o_ref.dtype)


def _diff_attn(q, k, v, lq1, lk1, lq2, lk2, subln_g, lam_init, tq=512):
    vec = pl.BlockSpec((1, HEAD_DIM), lambda h, i: (0, 0))
    kv_spec = pl.BlockSpec((SEQ, DIFF_V_DIM), lambda h, i: (0, h))
    return pl.pallas_call(
        functools.partial(_diff_attn_kernel, tq=tq, lam_init=lam_init),
        grid=(DIFF_HEADS, SEQ // tq),
        in_specs=[vec, vec, vec, vec, pl.BlockSpec((1, DIFF_V_DIM), lambda h, i: (0, 0)),
                  pl.BlockSpec((tq, DIFF_V_DIM), lambda h, i: (i, h)), kv_spec, kv_spec],
        out_specs=pl.BlockSpec((tq, DIFF_V_DIM), lambda h, i: (i, h)),
        out_shape=jax.ShapeDtypeStruct((SEQ, DIFF_WIDTH), BF16),
        scratch_shapes=[pltpu.VMEM((2, tq, LANES), F32), pltpu.VMEM((2, tq, LANES), F32),
                        pltpu.VMEM((2, tq, DIFF_V_DIM), F32)],
        compiler_params=_params(("arbitrary", "arbitrary"), VMEM_SMALL_MIB),
        name="diff_attn",
    )(lq1.reshape(1, -1), lk1.reshape(1, -1), lq2.reshape(1, -1), lk2.reshape(1, -1),
      subln_g.reshape(1, -1), q, k, v)


def _mem_attn_kernel(zm_ref, kv_ref, gq_ref, gk_ref, o_ref):
    scale = MEM_HEAD_DIM ** -0.5
    for h in range(MEM_HEADS):
        lo, hi = h * MEM_HEAD_DIM, (h + 1) * MEM_HEAD_DIM
        q = (_rms(zm_ref[:, lo:hi], gq_ref[...]) * scale).astype(BF16)
        k = _rms(kv_ref[:, lo:hi], gk_ref[...]).astype(BF16)
        v = kv_ref[:, MEM_WIDTH + lo:MEM_WIDTH + hi].astype(BF16)
        s = lax.dot_general(q, k, (((1,), (1,)), ((), ())), preferred_element_type=F32)
        p = jnp.exp(s - jnp.max(s, axis=-1, keepdims=True))
        l = jnp.sum(p, axis=-1, keepdims=True)
        o = jnp.dot(p.astype(BF16), v, preferred_element_type=F32) / l
        o_ref[:, lo:hi] = o.astype(o_ref.dtype)


def _mem_attn(z, kv, gq, gk, tm=512):
    vec = pl.BlockSpec((1, MEM_HEAD_DIM), lambda i: (0, 0))
    return pl.pallas_call(
        _mem_attn_kernel,
        grid=(SEQ // tm,),
        in_specs=[pl.BlockSpec((tm, MEM_WIDTH), lambda i: (i, 0)),
                  pl.BlockSpec((MEM_LEN, 2 * MEM_WIDTH), lambda i: (0, 0)), vec, vec],
        out_specs=pl.BlockSpec((tm, MEM_WIDTH), lambda i: (i, 0)),
        out_shape=jax.ShapeDtypeStruct((SEQ, MEM_WIDTH), BF16),
        compiler_params=_params(("arbitrary",), VMEM_SMALL_MIB),
        name="mem_attn",
    )(z, kv, gq.reshape(1, -1), gk.reshape(1, -1))


def _router_kernel(x_ref, g_ref, wr_ref, br_ref, h_ref, gate_ref, exp_ref, rank_ref, cnt_ref, carry_ref, *, tm):
    @pl.when(pl.program_id(0) == 0)
    def _():
        carry_ref[...] = jnp.zeros(carry_ref.shape, F32)

    h = _rms(x_ref[...], g_ref[...])
    _store_slabs(h_ref, h, tm)
    w = wr_ref[...]
    h_hi = h.astype(BF16)
    h_lo = (h - h_hi.astype(F32)).astype(BF16)
    w_hi = w.astype(BF16)
    w_lo = (w - w_hi.astype(F32)).astype(BF16)
    parts = jnp.dot(jnp.concatenate([h_hi, h_lo], axis=0), jnp.concatenate([w_hi, w_lo], axis=1),
                    preferred_element_type=F32)
    logits = ((parts[:tm, :N_EXPERTS] + parts[:tm, N_EXPERTS:]) + (parts[tm:, :N_EXPERTS] + parts[tm:, N_EXPERTS:])
              + br_ref[...])
    lane = lax.broadcasted_iota(jnp.int32, logits.shape, 1)
    slot = lax.broadcasted_iota(jnp.int32, (tm, TOP_K), 1)
    work = logits
    sel = jnp.zeros(logits.shape, jnp.bool_)
    hits = []
    top = None
    nums = jnp.zeros((tm, TOP_K), F32)
    experts = jnp.zeros((tm, TOP_K), jnp.int32)
    denom = jnp.zeros((tm, 1), F32)
    for k in range(TOP_K):
        m = jnp.max(work, axis=-1, keepdims=True)
        idx = jnp.min(jnp.where(work == m, lane, N_EXPERTS), axis=-1, keepdims=True)
        hit = lane == idx
        if k == 0:
            top = m
        e = jnp.exp(m - top)
        nums = jnp.where(slot == k, e, nums)
        experts = jnp.where(slot == k, idx, experts)
        denom = denom + e
        work = jnp.where(hit, -jnp.inf, work)
        sel = jnp.logical_or(sel, hit)
        hits.append(hit)
    gate_ref[...] = nums / denom
    exp_ref[...] = experts
    r = lax.broadcasted_iota(jnp.int32, (tm, tm), 0)
    c = lax.broadcasted_iota(jnp.int32, (tm, tm), 1)
    strict = jnp.where(c < r, 1.0, 0.0).astype(BF16)
    chosen = jnp.where(sel, 1.0, 0.0)
    rank = jnp.dot(strict, chosen.astype(BF16), preferred_element_type=F32) + carry_ref[...]
    ranks = jnp.zeros((tm, TOP_K), F32)
    for k in range(TOP_K):
        ranks = jnp.where(slot == k, jnp.sum(jnp.where(hits[k], rank, 0.0), axis=-1, keepdims=True), ranks)
    rank_ref[...] = ranks.astype(jnp.int32)
    carry_ref[...] = carry_ref[...] + jnp.sum(chosen, axis=0, keepdims=True)
    cnt_ref[...] = carry_ref[...].astype(jnp.int32)


def _router(x, g, w_router, b_router, tm=256):
    small = lambda: pl.BlockSpec((tm, TOP_K), lambda i: (i, 0))
    return pl.pallas_call(
        functools.partial(_router_kernel, tm=tm),
        grid=(SEQ // tm,),
        in_specs=[pl.BlockSpec((tm, D_MODEL), lambda i: (i, 0)), pl.BlockSpec((1, D_MODEL), lambda i: (0, 0)),
                  pl.BlockSpec((D_MODEL, N_EXPERTS), lambda i: (0, 0)), pl.BlockSpec((1, N_EXPERTS), lambda i: (0, 0))],
        out_specs=[pl.BlockSpec((tm * SLAB_ROWS, SLAB_LANES), lambda i: (i, 0)), small(), small(), small(),
                   pl.BlockSpec((1, N_EXPERTS), lambda i: (0, 0))],
        out_shape=[jax.ShapeDtypeStruct((SEQ * SLAB_ROWS, SLAB_LANES), jnp.uint32),
                   jax.ShapeDtypeStruct((SEQ, TOP_K), F32),
                   jax.ShapeDtypeStruct((SEQ, TOP_K), jnp.int32),
                   jax.ShapeDtypeStruct((SEQ, TOP_K), jnp.int32),
                   jax.ShapeDtypeStruct((1, N_EXPERTS), jnp.int32)],
        scratch_shapes=[pltpu.VMEM((1, N_EXPERTS), F32)],
        compiler_params=_params(("arbitrary",), VMEM_SMALL_MIB),
        name="router",
    )(x, g.reshape(1, -1), w_router, b_router.reshape(1, -1))


DMA_PRIORITIES = 2


def _slot_copy(src_ref, src_row, dst_ref, dst_row, sem):
    return pltpu.make_async_copy(src_ref.at[pl.ds(src_row * SLAB_ROWS, SLAB_ROWS)],
                                 dst_ref.at[pl.ds(dst_row * SLAB_ROWS, SLAB_ROWS)], sem)


def _tile_fill_copy(zero_ref, dst_ref, tile, sem):
    rows = MOE_TILE * SLAB_ROWS
    return pltpu.make_async_copy(zero_ref, dst_ref.at[pl.ds(tile * rows, rows)], sem)


def _dispatch_kernel(pstart_ref, fill_ref, exp_ref, rank_ref, h_hbm, o_ref, slot_ref,
                     zero_ref, buf_a, buf_b, fill_sem, in_sem, out_sem, *, tc):
    i = pl.program_id(0)
    n = pl.num_programs(0)
    rows = tc * SLAB_ROWS

    def load(tile, buf, s):
        return pltpu.make_async_copy(h_hbm.at[pl.ds(tile * rows, rows)], buf, in_sem.at[s])

    def wait_rows(buf, s):
        for _ in range(TOP_K):
            pltpu.make_async_copy(buf, o_ref.at[pl.ds(0, rows)], out_sem.at[s]).wait()

    @pl.when(i == 0)
    def _():
        load(0, buf_a, 0).start()
        zero_ref[...] = jnp.zeros(zero_ref.shape, zero_ref.dtype)

        def fill(t, carry):
            @pl.when(fill_ref[t] != 0)
            def _():
                _tile_fill_copy(zero_ref, o_ref, t, fill_sem).start()
            return carry

        def fill_wait(t, carry):
            @pl.when(fill_ref[t] != 0)
            def _():
                _tile_fill_copy(zero_ref, o_ref, t, fill_sem).wait()
            return carry

        lax.fori_loop(0, MOE_NT, fill, 0)
        lax.fori_loop(0, MOE_NT, fill_wait, 0)

    def step(cur_buf, cs, nxt_buf, ns):
        @pl.when(i > 0)
        def _():
            wait_rows(nxt_buf, ns)

        @pl.when(i + 1 < n)
        def _():
            load(i + 1, nxt_buf, ns).start()

        load(i, cur_buf, cs).wait()

        def issue(r, carry):
            for k in range(TOP_K):
                a = r * TOP_K + k
                slot = pstart_ref[exp_ref[0, 0, a]] + rank_ref[0, 0, a]
                slot_ref[0, 0, a] = slot
                _slot_copy(cur_buf, r, o_ref, slot, out_sem.at[cs]).start(priority=k % DMA_PRIORITIES)
            return carry

        lax.fori_loop(0, tc, issue, 0, unroll=4)

        @pl.when(i == n - 1)
        def _():
            wait_rows(cur_buf, cs)

    @pl.when(i % 2 == 0)
    def _():
        step(buf_a, 0, buf_b, 1)

    @pl.when(i % 2 == 1)
    def _():
        step(buf_b, 1, buf_a, 0)


def _dispatch(hp, experts4, rank4, pstart, fill_tiles, tc=128):
    nb = SEQ // tc
    idx_spec = pl.BlockSpec((1, 1, tc * TOP_K), lambda i, ps, ft: (i, 0, 0), memory_space=pltpu.SMEM)
    return pl.pallas_call(
        functools.partial(_dispatch_kernel, tc=tc),
        grid_spec=pltpu.PrefetchScalarGridSpec(
            num_scalar_prefetch=2,
            grid=(nb,),
            in_specs=[idx_spec, idx_spec, pl.BlockSpec(memory_space=pl.ANY)],
            out_specs=[pl.BlockSpec(memory_space=pl.ANY), idx_spec],
            scratch_shapes=[pltpu.VMEM((MOE_TILE * SLAB_ROWS, SLAB_LANES), jnp.uint32),
                            pltpu.VMEM((tc * SLAB_ROWS, SLAB_LANES), jnp.uint32),
                            pltpu.VMEM((tc * SLAB_ROWS, SLAB_LANES), jnp.uint32),
                            pltpu.SemaphoreType.DMA(()), pltpu.SemaphoreType.DMA((2,)), pltpu.SemaphoreType.DMA((2,))],
        ),
        out_shape=[jax.ShapeDtypeStruct((MOE_ROWS * SLAB_ROWS, SLAB_LANES), jnp.uint32),
                   jax.ShapeDtypeStruct((nb, 1, tc * TOP_K), jnp.int32)],
        compiler_params=_params(("arbitrary",), VMEM_SMALL_MIB),
        name="moe_dispatch",
    )(pstart, fill_tiles, experts4.reshape(nb, 1, tc * TOP_K), rank4.reshape(nb, 1, tc * TOP_K), hp)


def _group_edges(blk_ref, t):
    e = blk_ref[t]
    first = jnp.logical_or(t == 0, blk_ref[jnp.maximum(t - 1, 0)] != e)
    last = jnp.logical_or(t == MOE_NT - 1, blk_ref[jnp.minimum(t + 1, MOE_NT - 1)] != e)
    return e, first, last


def _stream_weights(blk_ref, nxt_ref, slot_ref, n_pass, copies, t):
    j = pl.program_id(0)
    e, first, last = _group_edges(blk_ref, t)

    @pl.when(jnp.logical_and(j == 0, t == 0))
    def _():
        slot_ref[0] = 0
        for c in copies(e, j, 0):
            c.start()

    @pl.when(first)
    def _():
        slot = slot_ref[0]
        for c in copies(e, j, slot):
            c.wait()
        nt = nxt_ref[t]
        same_pass = nt < MOE_NT
        ne = blk_ref[jnp.where(same_pass, nt, 0)]
        nj = jnp.where(same_pass, j, j + 1)

        @pl.when(jnp.logical_or(same_pass, j + 1 < n_pass))
        def _():
            for c in copies(ne, nj, 1 - slot):
                c.start()

    slot = slot_ref[0]

    @pl.when(last)
    def _():
        slot_ref[0] = 1 - slot

    return slot


def _pair_step(blk_ref, nxt_ref, nused_ref, slot_ref, n_pass, copies, compute, clear):
    ta = 2 * pl.program_id(1)
    tb = ta + 1
    ea = blk_ref[ta]
    eb = blk_ref[tb]
    va = ta < nused_ref[0]
    vb = tb < nused_ref[0]
    merged = jnp.logical_and(vb, ea == eb)

    slot_a = _stream_weights(blk_ref, nxt_ref, slot_ref, n_pass, copies, ta)

    @pl.when(merged)
    def _():
        compute(0, 2 * MOE_TILE, ea, slot_a)

    @pl.when(jnp.logical_and(va, jnp.logical_not(merged)))
    def _():
        compute(0, MOE_TILE, ea, slot_a)

    slot_b = _stream_weights(blk_ref, nxt_ref, slot_ref, n_pass, copies, tb)

    @pl.when(jnp.logical_and(vb, jnp.logical_not(merged)))
    def _():
        compute(MOE_TILE, MOE_TILE, eb, slot_b)

    @pl.when(jnp.logical_not(va))
    def _():
        clear(0, MOE_TILE)

    @pl.when(jnp.logical_not(vb))
    def _():
        clear(MOE_TILE, MOE_TILE)


def _moe_up_kernel(blk_ref, nxt_ref, nused_ref, x_ref, w_hbm, b_ref, o_ref, wbuf, sem, slot_ref, *, tn):
    j = pl.program_id(0)

    def copies(e, j, slot):
        return [pltpu.make_async_copy(w_hbm.at[e, :, pl.ds(pl.multiple_of(part * D_FF + j * tn, tn), tn)],
                                      wbuf.at[slot, part], sem.at[slot, part]) for part in range(2)]

    def compute(row0, rows, e, slot):
        x = _load_slabs(x_ref, row0, rows)

        def proj(part):
            bias = b_ref[pl.ds(e, 1), pl.ds(pl.multiple_of(part * D_FF + j * tn, tn), tn)]
            return jnp.dot(x, wbuf[slot, part].astype(BF16), preferred_element_type=F32) + bias

        gate = jnp.minimum(proj(0), SWIGLU_LIMIT)
        up = jnp.clip(proj(1), -SWIGLU_LIMIT, SWIGLU_LIMIT)
        act = (up + 1.0) * (gate * jax.nn.sigmoid(SWIGLU_ALPHA * gate))
        o_ref[row0:row0 + rows, :] = act.astype(o_ref.dtype)

    def clear(row0, rows):
        o_ref[row0:row0 + rows, :] = jnp.zeros((rows, tn), o_ref.dtype)

    _pair_step(blk_ref, nxt_ref, nused_ref, slot_ref, D_FF // tn, copies, compute, clear)


def _moe_up(xs, blk_e, nxt, n_used, w_gate_up, b_gate_up, tn=512):
    nj = D_FF // tn
    last = lambda u, nu: jnp.minimum(u, (nu[0] - 1) // 2)
    return pl.pallas_call(
        functools.partial(_moe_up_kernel, tn=tn),
        grid_spec=pltpu.PrefetchScalarGridSpec(
            num_scalar_prefetch=3,
            grid=(nj, MOE_NT // 2),
            in_specs=[
                pl.BlockSpec((2 * MOE_TILE * SLAB_ROWS, SLAB_LANES), lambda j, u, be, nx, nu: (last(u, nu), 0)),
                pl.BlockSpec(memory_space=pl.ANY),
                pl.BlockSpec((N_EXPERTS, 2 * D_FF), lambda j, u, be, nx, nu: (0, 0)),
            ],
            out_specs=pl.BlockSpec((2 * MOE_TILE, tn), lambda j, u, be, nx, nu: (u, j)),
            scratch_shapes=[pltpu.VMEM((2, 2, D_MODEL, tn), F32), pltpu.SemaphoreType.DMA((2, 2)),
                            pltpu.SMEM((1,), jnp.int32)],
        ),
        out_shape=jax.ShapeDtypeStruct((MOE_ROWS, D_FF), BF16),
        compiler_params=_params(("arbitrary", "arbitrary"), VMEM_LIMIT_MIB),
        name="moe_up",
    )(blk_e, nxt, n_used, xs, w_gate_up, b_gate_up)


DOWN_TN = D_MODEL // 2
DOWN_WORDS = DOWN_TN // 2
DOWN_ROWS = DOWN_WORDS // SLAB_LANES


def _moe_down_kernel(blk_ref, nxt_ref, nused_ref, a_ref, w_hbm, b_ref, o_ref, wbuf, sem, slot_ref):
    def copies(e, j, slot):
        return [pltpu.make_async_copy(w_hbm.at[e, :, pl.ds(pl.multiple_of(j * DOWN_TN, DOWN_TN), DOWN_TN)],
                                      wbuf.at[slot], sem.at[slot])]

    def compute(row0, rows, e, slot):
        bias = b_ref[pl.ds(e, 1), pl.ds(pl.multiple_of(pl.program_id(0) * DOWN_TN, DOWN_TN), DOWN_TN)]
        y = jnp.dot(a_ref[row0:row0 + rows, :], wbuf[slot].astype(BF16), preferred_element_type=F32) + bias
        words = pltpu.pack_elementwise([y[:, :DOWN_WORDS], y[:, DOWN_WORDS:]], packed_dtype=BF16)
        flat = o_ref.reshape(2 * MOE_TILE * DOWN_ROWS, SLAB_LANES)
        for s in range(DOWN_ROWS):
            flat[pl.ds(row0 * DOWN_ROWS + s, rows, stride=DOWN_ROWS), :] = words[:, s * SLAB_LANES:(s + 1) * SLAB_LANES]

    def clear(row0, rows):
        o_ref[row0:row0 + rows] = jnp.zeros((rows, DOWN_ROWS, SLAB_LANES), o_ref.dtype)

    _pair_step(blk_ref, nxt_ref, nused_ref, slot_ref, D_MODEL // DOWN_TN, copies, compute, clear)


def _moe_down(act, blk_e, nxt, n_used, w_down, b_down):
    nj = D_MODEL // DOWN_TN
    last = lambda u, nu: jnp.minimum(u, (nu[0] - 1) // 2)
    return pl.pallas_call(
        _moe_down_kernel,
        grid_spec=pltpu.PrefetchScalarGridSpec(
            num_scalar_prefetch=3,
            grid=(nj, MOE_NT // 2),
            in_specs=[
                pl.BlockSpec((2 * MOE_TILE, D_FF), lambda j, u, be, nx, nu: (last(u, nu), 0)),
                pl.BlockSpec(memory_space=pl.ANY),
                pl.BlockSpec((N_EXPERTS, D_MODEL), lambda j, u, be, nx, nu: (0, 0)),
            ],
            out_specs=pl.BlockSpec((2 * MOE_TILE, DOWN_ROWS, SLAB_LANES), lambda j, u, be, nx, nu: (u, j, 0)),
            scratch_shapes=[pltpu.VMEM((2, D_FF, DOWN_TN), F32), pltpu.SemaphoreType.DMA((2,)),
                            pltpu.SMEM((1,), jnp.int32)],
        ),
        out_shape=jax.ShapeDtypeStruct((MOE_ROWS, SLAB_ROWS, SLAB_LANES), jnp.uint32),
        compiler_params=_params(("arbitrary", "arbitrary"), VMEM_LIMIT_MIB),
        name="moe_down",
    )(blk_e, nxt, n_used, act, w_down, b_down).reshape(MOE_ROWS * SLAB_ROWS, SLAB_LANES)


COMBINE_ROWS = 32


def _combine_kernel(cur_ref, nxt_ref, ys_ref, x_ref, g_ref, o_ref, buf_a, buf_b, sem, *, tc):
    i = pl.program_id(0)

    def fetch(idx_ref, a, buf, slot):
        r, k = divmod(a, TOP_K)
        return _slot_copy(ys_ref, idx_ref[0, 0, a], buf.at[k], r, sem.at[slot])

    def wait_tile(buf, slot):
        for k in range(TOP_K):
            pltpu.make_async_copy(ys_ref.at[pl.ds(0, tc * SLAB_ROWS)], buf.at[k], sem.at[slot]).wait()

    @pl.when(i == 0)
    def _():
        def issue(r, carry):
            for k in range(TOP_K):
                _slot_copy(ys_ref, cur_ref[0, 0, r * TOP_K + k], buf_a.at[k], r, sem.at[0]).start(
                    priority=k % DMA_PRIORITIES)
            return carry

        lax.fori_loop(0, tc, issue, 0, unroll=4)

    def step(cur_buf, cur_slot, nxt_buf, nxt_slot):
        wait_tile(cur_buf, cur_slot)
        n_chunks = (tc // COMBINE_ROWS) * SLAB_ROWS
        per_chunk = tc * TOP_K // n_chunks
        a = 0
        for t0 in range(0, tc, COMBINE_ROWS):
            rows = slice(t0, t0 + COMBINE_ROWS)
            gates = [jnp.broadcast_to(g_ref[rows, k:k + 1], (COMBINE_ROWS, SLAB_LANES)) for k in range(TOP_K)]
            for s in range(SLAB_ROWS):
                for _ in range(per_chunk):
                    fetch(nxt_ref, a, nxt_buf, nxt_slot).start(priority=a % DMA_PRIORITIES)
                    a += 1
                lo_sum = None
                hi_sum = None
                for k in range(TOP_K):
                    words = cur_buf[k, pl.ds(t0 * SLAB_ROWS + s, COMBINE_ROWS, stride=SLAB_ROWS), :]
                    lo, hi = _unpack_bf16(words)
                    lo_sum = gates[k] * lo if lo_sum is None else lo_sum + gates[k] * lo
                    hi_sum = gates[k] * hi if hi_sum is None else hi_sum + gates[k] * hi
                c = (s // DOWN_ROWS) * DOWN_TN + (s % DOWN_ROWS) * SLAB_LANES
                o_ref[rows, c:c + SLAB_LANES] = x_ref[rows, c:c + SLAB_LANES] + lo_sum
                c += DOWN_WORDS
                o_ref[rows, c:c + SLAB_LANES] = x_ref[rows, c:c + SLAB_LANES] + hi_sum

        @pl.when(i == pl.num_programs(0) - 1)
        def _():
            wait_tile(nxt_buf, nxt_slot)

    @pl.when(i % 2 == 0)
    def _():
        step(buf_a, 0, buf_b, 1)

    @pl.when(i % 2 == 1)
    def _():
        step(buf_b, 1, buf_a, 0)


def _combine(ys, x2, slots, gates4, tc=128):
    nb = SEQ // tc
    idx_spec = lambda f: pl.BlockSpec((1, 1, tc * TOP_K), lambda i: (f(i), 0, 0), memory_space=pltpu.SMEM)
    return pl.pallas_call(
        functools.partial(_combine_kernel, tc=tc),
        grid=(nb,),
        in_specs=[idx_spec(lambda i: i), idx_spec(lambda i: jnp.minimum(i + 1, nb - 1)),
                  pl.BlockSpec(memory_space=pl.ANY),
                  pl.BlockSpec((tc, D_MODEL), lambda i: (i, 0)),
                  pl.BlockSpec((tc, TOP_K), lambda i: (i, 0))],
        out_specs=pl.BlockSpec((tc, D_MODEL), lambda i: (i, 0)),
        out_shape=jax.ShapeDtypeStruct((SEQ, D_MODEL), F32),
        scratch_shapes=[pltpu.VMEM((TOP_K, tc * SLAB_ROWS, SLAB_LANES), jnp.uint32),
                        pltpu.VMEM((TOP_K, tc * SLAB_ROWS, SLAB_LANES), jnp.uint32), pltpu.SemaphoreType.DMA((2,))],
        compiler_params=_params(("arbitrary",), VMEM_SMALL_MIB),
        name="moe_combine",
    )(slots, slots, ys, x2, gates4)


def kernel(x, mem, positions, attn_norm_g, w_in, gmlp_ln_g, gmlp_ln_b, gmlp_w_s, gmlp_b_s, diff_q_norm_g, diff_k_norm_g, diff_lambda_q1, diff_lambda_k1, diff_lambda_q2, diff_lambda_k2, diff_subln_g, mem_norm_g, w_mem_kv, mem_q_norm_g, mem_k_norm_g, w_out, ffn_norm_g, w_router, b_router, w_gate_up, b_gate_up, w_down, b_down):
    depth = attn_norm_g.shape[0]
    xs = x.reshape(SEQ, D_MODEL)
    mem2 = mem.reshape(MEM_LEN, D_MODEL)
    for i in range(depth):
        lam_init = 0.8 - 0.6 * math.exp(-0.3 * i)
        h = _rmsnorm_rows(xs, attn_norm_g[i], tm=256)

        def in_proj(name, col0, width, dtype):
            return _matmul([h], w_in[i], None, dtype, tm=PROJ_TM, tn=PROJ_TN, name=name, cols=(col0, width))

        z_gqk = in_proj("in_proj_gqk", 0, COL_V, F32)
        v = in_proj("in_proj_v", COL_V, DIFF_WIDTH, BF16)
        z_m = in_proj("in_proj_mem", COL_M, MEM_WIDTH, F32)

        y_g = _gmlp(z_gqk, gmlp_ln_g[i], gmlp_ln_b[i], gmlp_w_s[i], gmlp_b_s[i])

        q, k = _qk_prep(z_gqk, positions, diff_q_norm_g[i], diff_k_norm_g[i])
        y_d = _diff_attn(q, k, v, diff_lambda_q1[i], diff_lambda_k1[i], diff_lambda_q2[i], diff_lambda_k2[i],
                         diff_subln_g[i], lam_init)

        hm = _rmsnorm_rows(mem2, mem_norm_g[i], tm=256)
        kv = _matmul([hm], w_mem_kv[i], None, F32, tm=MEM_LEN, tn=PROJ_TN, name="mem_kv_proj")
        y_m = _mem_attn(z_m, kv, mem_q_norm_g[i], mem_k_norm_g[i])

        x2 = _matmul([y_g, y_d, y_m], w_out[i], xs, F32, tm=PROJ_TM, tn=PROJ_TN, name="out_proj")

        hp, gates4, experts4, rank4, counts = _router(x2, ffn_norm_g[i], w_router[i], b_router[i])

        counts = counts[0]
        padded = (counts + MOE_TILE - 1) // MOE_TILE * MOE_TILE
        pend = jnp.cumsum(padded)
        pstart = pend - padded
        n_used = (pend[-1] // MOE_TILE).astype(jnp.int32).reshape(1)
        blk_start = jnp.arange(MOE_NT, dtype=jnp.int32) * MOE_TILE
        blk_e = jnp.minimum(jnp.sum(pend[None, :] <= blk_start[:, None], axis=1), N_EXPERTS - 1).astype(jnp.int32)
        blk_end = blk_start + MOE_TILE
        fill_tiles = jnp.logical_or(jnp.any(blk_end[:, None] == pend[None, :], axis=1),
                                    blk_start >= pend[-1]).astype(jnp.int32)

        tile_ids = jnp.arange(MOE_NT, dtype=jnp.int32)
        run_start = jnp.concatenate([jnp.ones((1,), jnp.bool_), blk_e[1:] != blk_e[:-1]])
        starts_from = lax.cummin(jnp.where(run_start, tile_ids, MOE_NT), axis=0, reverse=True)
        nxt = jnp.concatenate([starts_from[1:], jnp.full((1,), MOE_NT, jnp.int32)]).astype(jnp.int32)

        xg, slots = _dispatch(hp, experts4, rank4, pstart, fill_tiles)
        act = _moe_up(xg, blk_e, nxt, n_used, w_gate_up[i], b_gate_up[i])
        ys = _moe_down(act, blk_e, nxt, n_used, w_down[i], b_down[i])
        xs = _combine(ys, x2, slots, gates4)
    return xs.reshape(x.shape)
```

```python
import functools
import math

import jax
import jax.numpy as jnp
from jax import lax
from jax.experimental import pallas as pl
from jax.experimental.pallas import tpu as pltpu

F32 = jnp.float32
BF16 = jnp.bfloat16

D_MODEL = 4096
SEQ = 8192
HEAD_DIM = 128
CHUNK = 128
GMLP_WIDTH = 1024
GMLP_HEADS = 8
DIFF_HEADS = 8
DIFF_V_DIM = 256
DIFF_QK_WIDTH = 2048
DIFF_WIDTH = 2048
MEM_WIDTH = 1024
MEM_HEADS = 4
MEM_HEAD_DIM = 256
MEM_LEN = 256
N_EXPERTS = 32
TOP_K = 4
D_FF = 1536
SWIGLU_LIMIT = 7.0
SWIGLU_ALPHA = 1.702
ROPE_THETA = 10000.0
EPS = 1e-6

COL_Q = 2 * GMLP_WIDTH
COL_K = COL_Q + DIFF_QK_WIDTH
COL_V = COL_K + DIFF_QK_WIDTH
COL_M = COL_V + DIFF_WIDTH

MOE_TILE = 256
MOE_ROWS = ((SEQ * TOP_K + N_EXPERTS * (MOE_TILE - 1) + MOE_TILE - 1) // MOE_TILE) * MOE_TILE
MOE_NT = MOE_ROWS // MOE_TILE
assert MOE_NT % 2 == 0

LANES = 128
MIB = 1 << 20
VMEM_LIMIT_MIB = 56
VMEM_SMALL_MIB = 40

PROJ_TM = 512
PROJ_TN = 1024


def _params(sem, vmem_mib):
    return pltpu.CompilerParams(dimension_semantics=sem, vmem_limit_bytes=vmem_mib * MIB)


def _rms(x, g):
    ms = jnp.mean(x * x, axis=-1, keepdims=True)
    return x * lax.rsqrt(ms + EPS) * g


SLAB_LANES = LANES
SLAB_ROWS = D_MODEL // 2 // SLAB_LANES


def _unpack_bf16(words):
    lo = pltpu.unpack_elementwise(words, index=0, packed_dtype=BF16, unpacked_dtype=F32)
    hi = pltpu.unpack_elementwise(words, index=1, packed_dtype=BF16, unpacked_dtype=F32)
    return lo, hi


def _store_slabs(o_ref, h, row0, rows):
    half = D_MODEL // 2
    for s in range(SLAB_ROWS):
        lo = h[:, s * SLAB_LANES:(s + 1) * SLAB_LANES]
        hi = h[:, half + s * SLAB_LANES:half + (s + 1) * SLAB_LANES]
        o_ref[pl.ds(row0 * SLAB_ROWS + s, rows, stride=SLAB_ROWS), :] = pltpu.pack_elementwise(
            [lo, hi], packed_dtype=BF16)


def _load_slabs(x_ref, row0, rows):
    los, his = [], []
    for s in range(SLAB_ROWS):
        lo, hi = _unpack_bf16(x_ref[pl.ds(row0 * SLAB_ROWS + s, rows, stride=SLAB_ROWS), :])
        los.append(lo.astype(BF16))
        his.append(hi.astype(BF16))
    return jnp.concatenate(los + his, axis=1)


def _rmsnorm_kernel(x_ref, g_ref, o_ref):
    o_ref[...] = _rms(x_ref[...], g_ref[...]).astype(o_ref.dtype)


def _rmsnorm_rows(x, g, tm):
    m, d = x.shape
    return pl.pallas_call(
        _rmsnorm_kernel,
        grid=(m // tm,),
        in_specs=[pl.BlockSpec((tm, d), lambda i: (i, 0)), pl.BlockSpec((1, d), lambda i: (0, 0))],
        out_specs=pl.BlockSpec((tm, d), lambda i: (i, 0)),
        out_shape=jax.ShapeDtypeStruct((m, d), BF16),
        compiler_params=_params(("arbitrary",), VMEM_SMALL_MIB),
        name="rmsnorm_rows",
    )(x, g.reshape(1, d))


def _matmul_kernel(*refs, widths, has_res):
    n = len(widths)
    lhs = refs[:n]
    w_ref = refs[n]
    res_ref = refs[n + 1] if has_res else None
    o_ref = refs[n + 1 + has_res]

    acc = None
    off = 0
    for r, kw in zip(lhs, widths):
        part = jnp.dot(r[...], w_ref[off:off + kw, :].astype(BF16), preferred_element_type=F32)
        acc = part if acc is None else acc + part
        off += kw
    if has_res:
        acc = acc + res_ref[...]
    o_ref[...] = acc.astype(o_ref.dtype)


def _matmul(lhs_list, w, res, out_dtype, tm, tn, name, cols=None):
    m = lhs_list[0].shape[0]
    k = w.shape[0]
    col0, n = cols if cols is not None else (0, w.shape[1])
    assert col0 % tn == 0 and n % tn == 0
    widths = tuple(a.shape[1] for a in lhs_list)
    assert sum(widths) == k
    in_specs = [pl.BlockSpec((tm, kw), lambda j, i: (i, 0)) for kw in widths]
    in_specs.append(pl.BlockSpec((k, tn), lambda j, i: (0, j + col0 // tn)))
    args = list(lhs_list) + [w]
    if res is not None:
        in_specs.append(pl.BlockSpec((tm, tn), lambda j, i: (i, j)))
        args.append(res)
    return pl.pallas_call(
        functools.partial(_matmul_kernel, widths=widths, has_res=res is not None),
        grid=(n // tn, m // tm),
        in_specs=in_specs,
        out_specs=pl.BlockSpec((tm, tn), lambda j, i: (i, j)),
        out_shape=jax.ShapeDtypeStruct((m, n), out_dtype),
        compiler_params=_params(("arbitrary", "arbitrary"), VMEM_LIMIT_MIB),
        name=name,
    )(*args)


def _gmlp_kernel(zu_ref, zv_ref, lng_ref, lnb_ref, ws_ref, bst_ref, o_ref, *, tm):
    inv_sqrt2 = 1.0 / math.sqrt(2.0)

    def gelu(x):
        return 0.5 * x * (1.0 + lax.erf(x * inv_sqrt2))

    u = gelu(zu_ref[...])
    v = gelu(zv_ref[...])
    mu = jnp.mean(v, axis=-1, keepdims=True)
    vc = v - mu
    var = jnp.mean(vc * vc, axis=-1, keepdims=True)
    vn = (vc * lax.rsqrt(var + EPS) * lng_ref[...] + lnb_ref[...]).astype(BF16)
    row = lax.broadcasted_iota(jnp.int32, (CHUNK, CHUNK), 0)
    col = lax.broadcasted_iota(jnp.int32, (CHUNK, CHUNK), 1)
    causal = col <= row
    bst = bst_ref[...]
    for h in range(GMLP_HEADS):
        w = jnp.where(causal, ws_ref[h], 0.0).astype(BF16)
        b = bst[:, h:h + 1]
        for c in range(tm // CHUNK):
            vs = vn[c * CHUNK:(c + 1) * CHUNK, h * HEAD_DIM:(h + 1) * HEAD_DIM]
            mixed = jnp.dot(w, vs, preferred_element_type=F32) + b
            o_ref[c * CHUNK:(c + 1) * CHUNK, h * HEAD_DIM:(h + 1) * HEAD_DIM] = (
                u[c * CHUNK:(c + 1) * CHUNK, h * HEAD_DIM:(h + 1) * HEAD_DIM] * mixed).astype(o_ref.dtype)


def _gmlp(z, ln_g, ln_b, w_s, b_s, tm=512):
    return pl.pallas_call(
        functools.partial(_gmlp_kernel, tm=tm),
        grid=(SEQ // tm,),
        in_specs=[
            pl.BlockSpec((tm, GMLP_WIDTH), lambda i: (i, 0)),
            pl.BlockSpec((tm, GMLP_WIDTH), lambda i: (i, 1)),
            pl.BlockSpec((1, GMLP_WIDTH), lambda i: (0, 0)),
            pl.BlockSpec((1, GMLP_WIDTH), lambda i: (0, 0)),
            pl.BlockSpec((GMLP_HEADS, CHUNK, CHUNK), lambda i: (0, 0, 0)),
            pl.BlockSpec((CHUNK, GMLP_HEADS), lambda i: (0, 0)),
        ],
        out_specs=pl.BlockSpec((tm, GMLP_WIDTH), lambda i: (i, 0)),
        out_shape=jax.ShapeDtypeStruct((SEQ, GMLP_WIDTH), BF16),
        compiler_params=_params(("arbitrary",), VMEM_SMALL_MIB),
        name="gmlp",
    )(z, z, ln_g.reshape(1, -1), ln_b.reshape(1, -1), w_s, b_s.T)


def _qk_prep_kernel(pos_ref, freq_ref, sign_ref, gq_ref, gk_ref, zq_ref, zk_ref, q_ref, k_ref):
    ang = pos_ref[...].astype(F32) * freq_ref[...]
    cos = jnp.cos(ang)
    sin = jnp.sin(ang) * sign_ref[...]
    scale = HEAD_DIM ** -0.5 * math.log2(math.e)

    pair = 2 * HEAD_DIM
    r2 = lax.broadcasted_iota(jnp.int32, (pair, pair), 0) // HEAD_DIM
    c2 = lax.broadcasted_iota(jnp.int32, (pair, pair), 1) // HEAD_DIM
    ones = jnp.where(r2 == c2, 1.0, 0.0).astype(BF16)

    def prep(z_ref, g_ref, o_ref, mult):
        g = g_ref[...]
        a = cos * (g * mult)
        b = sin * (pltpu.roll(g, HEAD_DIM // 2, 1) * mult)
        for s in range(DIFF_QK_WIDTH // pair):
            x = z_ref[:, s * pair:(s + 1) * pair]
            sq = x * x
            hi = sq.astype(BF16)
            lo = (sq - hi.astype(F32)).astype(BF16)
            ssum = jnp.dot(hi, ones, preferred_element_type=F32) + jnp.dot(lo, ones, preferred_element_type=F32)
            for t in range(2):
                cols = slice(t * HEAD_DIM, (t + 1) * HEAD_DIM)
                seg = x[:, cols]
                r = lax.rsqrt(ssum[:, cols] * (1.0 / HEAD_DIM) + EPS)
                out = (seg * a + pltpu.roll(seg, HEAD_DIM // 2, 1) * b) * r
                o_ref[:, s * pair + t * HEAD_DIM:s * pair + (t + 1) * HEAD_DIM] = out.astype(o_ref.dtype)

    prep(zq_ref, gq_ref, q_ref, scale)
    prep(zk_ref, gk_ref, k_ref, 1.0)


def _qk_prep(z, positions, gq, gk, tm=256):
    half = jnp.arange(0, HEAD_DIM, 2, dtype=F32) / HEAD_DIM
    inv_freq = 1.0 / (ROPE_THETA ** half)
    freq = jnp.concatenate([inv_freq, inv_freq]).reshape(1, HEAD_DIM)
    sign = jnp.concatenate([-jnp.ones((HEAD_DIM // 2,), F32), jnp.ones((HEAD_DIM // 2,), F32)]).reshape(1, HEAD_DIM)
    wide = DIFF_QK_WIDTH
    row_spec = lambda cb: pl.BlockSpec((tm, wide), lambda i: (i, cb))
    vec = pl.BlockSpec((1, HEAD_DIM), lambda i: (0, 0))
    out_spec = pl.BlockSpec((tm, wide), lambda i: (i, 0))
    shp = jax.ShapeDtypeStruct((SEQ, wide), BF16)
    return pl.pallas_call(
        _qk_prep_kernel,
        grid=(SEQ // tm,),
        in_specs=[pl.BlockSpec((tm, 1), lambda i: (i, 0)), vec, vec, vec, vec,
                  row_spec(COL_Q // wide), row_spec(COL_K // wide)],
        out_specs=[out_spec, out_spec],
        out_shape=[shp, shp],
        compiler_params=_params(("arbitrary",), VMEM_SMALL_MIB),
        name="qk_prep",
    )(positions.reshape(SEQ, 1), freq, sign, gq.reshape(1, -1), gk.reshape(1, -1), z, z)


KV_UNROLL = 4
def _diff_attn_kernel(lq1_ref, lk1_ref, lq2_ref, lk2_ref, sg_ref, q_ref, k_ref, v_ref, o_ref,
                      m_ref, l_ref, acc_ref, *, tq, lam_init):
    i = pl.program_id(1)
    m_ref[...] = jnp.full(m_ref.shape, -jnp.inf, F32)
    l_ref[...] = jnp.zeros(l_ref.shape, F32)
    acc_ref[...] = jnp.zeros(acc_ref.shape, F32)
    lanes = tq // LANES

    def block(j, masked):
        start = pl.multiple_of(j * tq, tq)
        kb = k_ref[pl.ds(start, tq), :]
        vb = v_ref[pl.ds(start, tq), :]
        for a in range(2):
            q = q_ref[:, a * HEAD_DIM:(a + 1) * HEAD_DIM]
            k = kb[:, a * HEAD_DIM:(a + 1) * HEAD_DIM]
            s = lax.dot_general(q, k, (((1,), (1,)), ((), ())), preferred_element_type=F32)
            if masked:
                row = lax.broadcasted_iota(jnp.int32, (tq, tq), 0)
                col = lax.broadcasted_iota(jnp.int32, (tq, tq), 1)
                s = jnp.where(col <= row, s, -jnp.inf)
            m_prev = m_ref[a]
            m_new = jnp.maximum(m_prev, jnp.max(s, axis=-1, keepdims=True))
            alpha = jnp.exp2(m_prev - m_new)
            p = jnp.exp2(s - jnp.concatenate([m_new] * lanes, axis=1))
            psum = p[:, 0:LANES]
            for c in range(1, lanes):
                psum = psum + p[:, c * LANES:(c + 1) * LANES]
            l_ref[a] = alpha * l_ref[a] + psum
            acc_ref[a] = (jnp.concatenate([alpha] * (DIFF_V_DIM // LANES), axis=1) * acc_ref[a]
                          + jnp.dot(p.astype(BF16), vb, preferred_element_type=F32))
            m_ref[a] = m_new

    def group(g, carry):
        for u in range(KV_UNROLL):
            block(KV_UNROLL * g + u, False)
        return carry

    groups = i // KV_UNROLL
    lax.fori_loop(0, groups, group, 0)
    rem = i - groups * KV_UNROLL
    for r in range(KV_UNROLL):
        @pl.when(rem == r)
        def _(r=r):
            for u in range(r):
                block(groups * KV_UNROLL + u, False)
            block(i, True)

    lam = (jnp.exp(jnp.sum(lq1_ref[...] * lk1_ref[...], axis=-1, keepdims=True))
           - jnp.exp(jnp.sum(lq2_ref[...] * lk2_ref[...], axis=-1, keepdims=True)) + lam_init)
    l1 = jnp.sum(l_ref[0], axis=-1, keepdims=True)
    l2 = jnp.sum(l_ref[1], axis=-1, keepdims=True)
    o = acc_ref[0] / l1 - lam * (acc_ref[1] / l2)
    o_ref[...] = (_rms(o, sg_ref[...]) * (1.0 - lam_init)).astype(o_ref.dtype)


def _diff_attn(q, k, v, lq1, lk1, lq2, lk2, subln_g, lam_init, tq=512):
    vec = pl.BlockSpec((1, HEAD_DIM), lambda h, i: (0, 0))
    kv_spec = pl.BlockSpec((SEQ, DIFF_V_DIM), lambda h, i: (0, h))
    return pl.pallas_call(
        functools.partial(_diff_attn_kernel, tq=tq, lam_init=lam_init),
        grid=(DIFF_HEADS, SEQ // tq),
        in_specs=[vec, vec, vec, vec, pl.BlockSpec((1, DIFF_V_DIM), lambda h, i: (0, 0)),
                  pl.BlockSpec((tq, DIFF_V_DIM), lambda h, i: (i, h)), kv_spec, kv_spec],
        out_specs=pl.BlockSpec((tq, DIFF_V_DIM), lambda h, i: (i, h)),
        out_shape=jax.ShapeDtypeStruct((SEQ, DIFF_WIDTH), BF16),
        scratch_shapes=[pltpu.VMEM((2, tq, LANES), F32), pltpu.VMEM((2, tq, LANES), F32),
                        pltpu.VMEM((2, tq, DIFF_V_DIM), F32)],
        compiler_params=_params(("arbitrary", "arbitrary"), VMEM_SMALL_MIB),
        name="diff_attn",
    )(lq1.reshape(1, -1), lk1.reshape(1, -1), lq2.reshape(1, -1), lk2.reshape(1, -1),
      subln_g.reshape(1, -1), q, k, v)


def _mem_attn_kernel(zm_ref, kv_ref, gq_ref, gk_ref, o_ref):
    scale = MEM_HEAD_DIM ** -0.5
    for h in range(MEM_HEADS):
        lo, hi = h * MEM_HEAD_DIM, (h + 1) * MEM_HEAD_DIM
        q = (_rms(zm_ref[:, lo:hi], gq_ref[...]) * scale).astype(BF16)
        k = _rms(kv_ref[:, lo:hi], gk_ref[...]).astype(BF16)
        v = kv_ref[:, MEM_WIDTH + lo:MEM_WIDTH + hi].astype(BF16)
        s = lax.dot_general(q, k, (((1,), (1,)), ((), ())), preferred_element_type=F32)
        p = jnp.exp(s - jnp.max(s, axis=-1, keepdims=True))
        l = jnp.sum(p, axis=-1, keepdims=True)
        o = jnp.dot(p.astype(BF16), v, preferred_element_type=F32) / l
        o_ref[:, lo:hi] = o.astype(o_ref.dtype)


def _mem_attn(z, kv, gq, gk, tm=512):
    vec = pl.BlockSpec((1, MEM_HEAD_DIM), lambda i: (0, 0))
    return pl.pallas_call(
        _mem_attn_kernel,
        grid=(SEQ // tm,),
        in_specs=[pl.BlockSpec((tm, MEM_WIDTH), lambda i: (i, 0)),
                  pl.BlockSpec((MEM_LEN, 2 * MEM_WIDTH), lambda i: (0, 0)), vec, vec],
        out_specs=pl.BlockSpec((tm, MEM_WIDTH), lambda i: (i, 0)),
        out_shape=jax.ShapeDtypeStruct((SEQ, MEM_WIDTH), BF16),
        compiler_params=_params(("arbitrary",), VMEM_SMALL_MIB),
        name="mem_attn",
    )(z, kv, gq.reshape(1, -1), gk.reshape(1, -1))


def _router_kernel(x_ref, g_ref, wr_ref, br_ref, h_ref, gate_ref, exp_ref, rank_ref, cnt_ref, carry_ref, *, tm):
    @pl.when(pl.program_id(0) == 0)
    def _():
        carry_ref[...] = jnp.zeros(carry_ref.shape, F32)

    h = _rms(x_ref[...], g_ref[...])
    _store_slabs(h_ref, h, 0, tm)
    w = wr_ref[...]
    h_hi = h.astype(BF16)
    h_lo = (h - h_hi.astype(F32)).astype(BF16)
    w_hi = w.astype(BF16)
    w_lo = (w - w_hi.astype(F32)).astype(BF16)
    parts = jnp.dot(jnp.concatenate([h_hi, h_lo], axis=0), jnp.concatenate([w_hi, w_lo], axis=1),
                    preferred_element_type=F32)
    logits = ((parts[:tm, :N_EXPERTS] + parts[:tm, N_EXPERTS:]) + (parts[tm:, :N_EXPERTS] + parts[tm:, N_EXPERTS:])
              + br_ref[...])
    lane = lax.broadcasted_iota(jnp.int32, logits.shape, 1)
    slot = lax.broadcasted_iota(jnp.int32, (tm, TOP_K), 1)
    work = logits
    sel = jnp.zeros(logits.shape, jnp.bool_)
    hits = []
    top = None
    nums = jnp.zeros((tm, TOP_K), F32)
    experts = jnp.zeros((tm, TOP_K), jnp.int32)
    denom = jnp.zeros((tm, 1), F32)
    for k in range(TOP_K):
        m = jnp.max(work, axis=-1, keepdims=True)
        idx = jnp.min(jnp.where(work == m, lane, N_EXPERTS), axis=-1, keepdims=True)
        hit = lane == idx
        if k == 0:
            top = m
        e = jnp.exp(m - top)
        nums = jnp.where(slot == k, e, nums)
        experts = jnp.where(slot == k, idx, experts)
        denom = denom + e
        work = jnp.where(hit, -jnp.inf, work)
        sel = jnp.logical_or(sel, hit)
        hits.append(hit)
    gate_ref[...] = nums / denom
    exp_ref[...] = experts
    r = lax.broadcasted_iota(jnp.int32, (tm, tm), 0)
    c = lax.broadcasted_iota(jnp.int32, (tm, tm), 1)
    strict = jnp.where(c < r, 1.0, 0.0).astype(BF16)
    chosen = jnp.where(sel, 1.0, 0.0)
    rank = jnp.dot(strict, chosen.astype(BF16), preferred_element_type=F32) + carry_ref[...]
    ranks = jnp.zeros((tm, TOP_K), F32)
    for k in range(TOP_K):
        ranks = jnp.where(slot == k, jnp.sum(jnp.where(hits[k], rank, 0.0), axis=-1, keepdims=True), ranks)
    rank_ref[...] = ranks.astype(jnp.int32)
    carry_ref[...] = carry_ref[...] + jnp.sum(chosen, axis=0, keepdims=True)
    cnt_ref[...] = carry_ref[...].astype(jnp.int32)


def _router(x, g, w_router, b_router, tm=256):
    small = lambda: pl.BlockSpec((tm, TOP_K), lambda i: (i, 0))
    return pl.pallas_call(
        functools.partial(_router_kernel, tm=tm),
        grid=(SEQ // tm,),
        in_specs=[pl.BlockSpec((tm, D_MODEL), lambda i: (i, 0)), pl.BlockSpec((1, D_MODEL), lambda i: (0, 0)),
                  pl.BlockSpec((D_MODEL, N_EXPERTS), lambda i: (0, 0)), pl.BlockSpec((1, N_EXPERTS), lambda i: (0, 0))],
        out_specs=[pl.BlockSpec((tm * SLAB_ROWS, SLAB_LANES), lambda i: (i, 0)), small(), small(), small(),
                   pl.BlockSpec((1, N_EXPERTS), lambda i: (0, 0))],
        out_shape=[jax.ShapeDtypeStruct((SEQ * SLAB_ROWS, SLAB_LANES), jnp.uint32),
                   jax.ShapeDtypeStruct((SEQ, TOP_K), F32),
                   jax.ShapeDtypeStruct((SEQ, TOP_K), jnp.int32),
                   jax.ShapeDtypeStruct((SEQ, TOP_K), jnp.int32),
                   jax.ShapeDtypeStruct((1, N_EXPERTS), jnp.int32)],
        scratch_shapes=[pltpu.VMEM((1, N_EXPERTS), F32)],
        compiler_params=_params(("arbitrary",), VMEM_SMALL_MIB),
        name="router",
    )(x, g.reshape(1, -1), w_router, b_router.reshape(1, -1))


DMA_PRIORITIES = 2


def _slot_copy(src_ref, src_row, dst_ref, dst_row, sem):
    return pltpu.make_async_copy(src_ref.at[pl.ds(src_row * SLAB_ROWS, SLAB_ROWS)],
                                 dst_ref.at[pl.ds(dst_row * SLAB_ROWS, SLAB_ROWS)], sem)


def _tile_fill_copy(zero_ref, dst_ref, tile, sem):
    rows = MOE_TILE * SLAB_ROWS
    return pltpu.make_async_copy(zero_ref, dst_ref.at[pl.ds(tile * rows, rows)], sem)


def _dispatch_kernel(pstart_ref, fill_ref, exp_ref, rank_ref, h_hbm, o_ref, slot_ref,
                     zero_ref, buf_a, buf_b, fill_sem, in_sem, out_sem, *, tc):
    i = pl.program_id(0)
    n = pl.num_programs(0)
    rows = tc * SLAB_ROWS

    def load(tile, buf, s):
        return pltpu.make_async_copy(h_hbm.at[pl.ds(tile * rows, rows)], buf, in_sem.at[s])

    def wait_rows(buf, s):
        for _ in range(TOP_K):
            pltpu.make_async_copy(buf, o_ref.at[pl.ds(0, rows)], out_sem.at[s]).wait()

    @pl.when(i == 0)
    def _():
        load(0, buf_a, 0).start()
        zero_ref[...] = jnp.zeros(zero_ref.shape, zero_ref.dtype)

        def fill(t, carry):
            @pl.when(fill_ref[t] != 0)
            def _():
                _tile_fill_copy(zero_ref, o_ref, t, fill_sem).start()
            return carry

        def fill_wait(t, carry):
            @pl.when(fill_ref[t] != 0)
            def _():
                _tile_fill_copy(zero_ref, o_ref, t, fill_sem).wait()
            return carry

        lax.fori_loop(0, MOE_NT, fill, 0)
        lax.fori_loop(0, MOE_NT, fill_wait, 0)

    def step(cur_buf, cs, nxt_buf, ns):
        @pl.when(i > 0)
        def _():
            wait_rows(nxt_buf, ns)

        @pl.when(i + 1 < n)
        def _():
            load(i + 1, nxt_buf, ns).start()

        load(i, cur_buf, cs).wait()

        def issue(r, carry):
            for k in range(TOP_K):
                a = r * TOP_K + k
                slot = pstart_ref[exp_ref[0, 0, a]] + rank_ref[0, 0, a]
                slot_ref[0, 0, a] = slot
                _slot_copy(cur_buf, r, o_ref, slot, out_sem.at[cs]).start(priority=k % DMA_PRIORITIES)
            return carry

        lax.fori_loop(0, tc, issue, 0, unroll=4)

        @pl.when(i == n - 1)
        def _():
            wait_rows(cur_buf, cs)

    @pl.when(i % 2 == 0)
    def _():
        step(buf_a, 0, buf_b, 1)

    @pl.when(i % 2 == 1)
    def _():
        step(buf_b, 1, buf_a, 0)


def _dispatch(hp, experts4, rank4, pstart, fill_tiles, tc=128):
    nb = SEQ // tc
    idx_spec = pl.BlockSpec((1, 1, tc * TOP_K), lambda i, ps, ft: (i, 0, 0), memory_space=pltpu.SMEM)
    return pl.pallas_call(
        functools.partial(_dispatch_kernel, tc=tc),
        grid_spec=pltpu.PrefetchScalarGridSpec(
            num_scalar_prefetch=2,
            grid=(nb,),
            in_specs=[idx_spec, idx_spec, pl.BlockSpec(memory_space=pl.ANY)],
            out_specs=[pl.BlockSpec(memory_space=pl.ANY), idx_spec],
            scratch_shapes=[pltpu.VMEM((MOE_TILE * SLAB_ROWS, SLAB_LANES), jnp.uint32),
                            pltpu.VMEM((tc * SLAB_ROWS, SLAB_LANES), jnp.uint32),
                            pltpu.VMEM((tc * SLAB_ROWS, SLAB_LANES), jnp.uint32),
                            pltpu.SemaphoreType.DMA(()), pltpu.SemaphoreType.DMA((2,)), pltpu.SemaphoreType.DMA((2,))],
        ),
        out_shape=[jax.ShapeDtypeStruct((MOE_ROWS * SLAB_ROWS, SLAB_LANES), jnp.uint32),
                   jax.ShapeDtypeStruct((nb, 1, tc * TOP_K), jnp.int32)],
        compiler_params=_params(("arbitrary",), VMEM_SMALL_MIB),
        name="moe_dispatch",
    )(pstart, fill_tiles, experts4.reshape(nb, 1, tc * TOP_K), rank4.reshape(nb, 1, tc * TOP_K), hp)


def _group_edges(blk_ref, t):
    e = blk_ref[t]
    first = jnp.logical_or(t == 0, blk_ref[jnp.maximum(t - 1, 0)] != e)
    last = jnp.logical_or(t == MOE_NT - 1, blk_ref[jnp.minimum(t + 1, MOE_NT - 1)] != e)
    return e, first, last


def _stream_weights(blk_ref, nxt_ref, slot_ref, n_pass, copies, t):
    j = pl.program_id(0)
    e, first, last = _group_edges(blk_ref, t)

    @pl.when(jnp.logical_and(j == 0, t == 0))
    def _():
        slot_ref[0] = 0
        for c in copies(e, j, 0):
            c.start()

    @pl.when(first)
    def _():
        slot = slot_ref[0]
        for c in copies(e, j, slot):
            c.wait()
        nt = nxt_ref[t]
        same_pass = nt < MOE_NT
        ne = blk_ref[jnp.where(same_pass, nt, 0)]
        nj = jnp.where(same_pass, j, j + 1)

        @pl.when(jnp.logical_or(same_pass, j + 1 < n_pass))
        def _():
            for c in copies(ne, nj, 1 - slot):
                c.start()

    slot = slot_ref[0]

    @pl.when(last)
    def _():
        slot_ref[0] = 1 - slot

    return slot


def _pair_step(blk_ref, nxt_ref, nused_ref, slot_ref, n_pass, copies, compute, clear):
    ta = 2 * pl.program_id(1)
    tb = ta + 1
    ea = blk_ref[ta]
    eb = blk_ref[tb]
    va = ta < nused_ref[0]
    vb = tb < nused_ref[0]
    merged = jnp.logical_and(vb, ea == eb)

    slot_a = _stream_weights(blk_ref, nxt_ref, slot_ref, n_pass, copies, ta)

    @pl.when(merged)
    def _():
        compute(0, 2 * MOE_TILE, ea, slot_a)

    @pl.when(jnp.logical_and(va, jnp.logical_not(merged)))
    def _():
        compute(0, MOE_TILE, ea, slot_a)

    slot_b = _stream_weights(blk_ref, nxt_ref, slot_ref, n_pass, copies, tb)

    @pl.when(jnp.logical_and(vb, jnp.logical_not(merged)))
    def _():
        compute(MOE_TILE, MOE_TILE, eb, slot_b)

    @pl.when(jnp.logical_not(va))
    def _():
        clear(0, MOE_TILE)

    @pl.when(jnp.logical_not(vb))
    def _():
        clear(MOE_TILE, MOE_TILE)


def _moe_up_kernel(blk_ref, nxt_ref, nused_ref, x_ref, w_hbm, b_ref, o_ref, wbuf, sem, slot_ref, *, tn):
    j = pl.program_id(0)

    def copies(e, j, slot):
        return [pltpu.make_async_copy(w_hbm.at[e, :, pl.ds(pl.multiple_of(part * D_FF + j * tn, tn), tn)],
                                      wbuf.at[slot, part], sem.at[slot, part]) for part in range(2)]

    def compute(row0, rows, e, slot):
        x = _load_slabs(x_ref, row0, rows)

        def proj(part):
            bias = b_ref[pl.ds(e, 1), pl.ds(pl.multiple_of(part * D_FF + j * tn, tn), tn)]
            return jnp.dot(x, wbuf[slot, part].astype(BF16), preferred_element_type=F32) + bias

        gate = jnp.minimum(proj(0), SWIGLU_LIMIT)
        up = jnp.clip(proj(1), -SWIGLU_LIMIT, SWIGLU_LIMIT)
        act = (up + 1.0) * (gate * jax.nn.sigmoid(SWIGLU_ALPHA * gate))
        o_ref[row0:row0 + rows, :] = act.astype(o_ref.dtype)

    def clear(row0, rows):
        o_ref[row0:row0 + rows, :] = jnp.zeros((rows, tn), o_ref.dtype)

    _pair_step(blk_ref, nxt_ref, nused_ref, slot_ref, D_FF // tn, copies, compute, clear)


def _moe_up(xs, blk_e, nxt, n_used, w_gate_up, b_gate_up, tn=512):
    nj = D_FF // tn
    last = lambda u, nu: jnp.minimum(u, (nu[0] - 1) // 2)
    return pl.pallas_call(
        functools.partial(_moe_up_kernel, tn=tn),
        grid_spec=pltpu.PrefetchScalarGridSpec(
            num_scalar_prefetch=3,
            grid=(nj, MOE_NT // 2),
            in_specs=[
                pl.BlockSpec((2 * MOE_TILE * SLAB_ROWS, SLAB_LANES), lambda j, u, be, nx, nu: (last(u, nu), 0)),
                pl.BlockSpec(memory_space=pl.ANY),
                pl.BlockSpec((N_EXPERTS, 2 * D_FF), lambda j, u, be, nx, nu: (0, 0)),
            ],
            out_specs=pl.BlockSpec((2 * MOE_TILE, tn), lambda j, u, be, nx, nu: (u, j)),
            scratch_shapes=[pltpu.VMEM((2, 2, D_MODEL, tn), F32), pltpu.SemaphoreType.DMA((2, 2)),
                            pltpu.SMEM((1,), jnp.int32)],
        ),
        out_shape=jax.ShapeDtypeStruct((MOE_ROWS, D_FF), BF16),
        compiler_params=_params(("arbitrary", "arbitrary"), VMEM_LIMIT_MIB),
        name="moe_up",
    )(blk_e, nxt, n_used, xs, w_gate_up, b_gate_up)


DOWN_TN = D_MODEL // 2
DOWN_WORDS = DOWN_TN // 2
DOWN_ROWS = DOWN_WORDS // SLAB_LANES


def _moe_down_kernel(blk_ref, nxt_ref, nused_ref, a_ref, w_hbm, b_ref, o_ref, wbuf, sem, slot_ref):
    def copies(e, j, slot):
        return [pltpu.make_async_copy(w_hbm.at[e, :, pl.ds(pl.multiple_of(j * DOWN_TN, DOWN_TN), DOWN_TN)],
                                      wbuf.at[slot], sem.at[slot])]

    def compute(row0, rows, e, slot):
        bias = b_ref[pl.ds(e, 1), pl.ds(pl.multiple_of(pl.program_id(0) * DOWN_TN, DOWN_TN), DOWN_TN)]
        y = jnp.dot(a_ref[row0:row0 + rows, :], wbuf[slot].astype(BF16), preferred_element_type=F32) + bias
        words = pltpu.pack_elementwise([y[:, :DOWN_WORDS], y[:, DOWN_WORDS:]], packed_dtype=BF16)
        flat = o_ref.reshape(2 * MOE_TILE * DOWN_ROWS, SLAB_LANES)
        for s in range(DOWN_ROWS):
            flat[pl.ds(row0 * DOWN_ROWS + s, rows, stride=DOWN_ROWS), :] = words[:, s * SLAB_LANES:(s + 1) * SLAB_LANES]

    def clear(row0, rows):
        o_ref[row0:row0 + rows] = jnp.zeros((rows, DOWN_ROWS, SLAB_LANES), o_ref.dtype)

    _pair_step(blk_ref, nxt_ref, nused_ref, slot_ref, D_MODEL // DOWN_TN, copies, compute, clear)


def _moe_down(act, blk_e, nxt, n_used, w_down, b_down):
    nj = D_MODEL // DOWN_TN
    last = lambda u, nu: jnp.minimum(u, (nu[0] - 1) // 2)
    return pl.pallas_call(
        _moe_down_kernel,
        grid_spec=pltpu.PrefetchScalarGridSpec(
            num_scalar_prefetch=3,
            grid=(nj, MOE_NT // 2),
            in_specs=[
                pl.BlockSpec((2 * MOE_TILE, D_FF), lambda j, u, be, nx, nu: (last(u, nu), 0)),
                pl.BlockSpec(memory_space=pl.ANY),
                pl.BlockSpec((N_EXPERTS, D_MODEL), lambda j, u, be, nx, nu: (0, 0)),
            ],
            out_specs=pl.BlockSpec((2 * MOE_TILE, DOWN_ROWS, SLAB_LANES), lambda j, u, be, nx, nu: (u, j, 0)),
            scratch_shapes=[pltpu.VMEM((2, D_FF, DOWN_TN), F32), pltpu.SemaphoreType.DMA((2,)),
                            pltpu.SMEM((1,), jnp.int32)],
        ),
        out_shape=jax.ShapeDtypeStruct((MOE_ROWS, SLAB_ROWS, SLAB_LANES), jnp.uint32),
        compiler_params=_params(("arbitrary", "arbitrary"), VMEM_LIMIT_MIB),
        name="moe_down",
    )(blk_e, nxt, n_used, act, w_down, b_down).reshape(MOE_ROWS * SLAB_ROWS, SLAB_LANES)


COMBINE_ROWS = 32


def _combine_kernel(cur_ref, nxt_ref, ys_ref, x_ref, g_ref, o_ref, buf_a, buf_b, sem, *, tc):
    i = pl.program_id(0)

    def fetch(idx_ref, a, buf, slot):
        r, k = divmod(a, TOP_K)
        return _slot_copy(ys_ref, idx_ref[0, 0, a], buf.at[k], r, sem.at[slot])

    def wait_tile(buf, slot):
        for k in range(TOP_K):
            pltpu.make_async_copy(ys_ref.at[pl.ds(0, tc * SLAB_ROWS)], buf.at[k], sem.at[slot]).wait()

    @pl.when(i == 0)
    def _():
        def issue(r, carry):
            for k in range(TOP_K):
                _slot_copy(ys_ref, cur_ref[0, 0, r * TOP_K + k], buf_a.at[k], r, sem.at[0]).start(
                    priority=k % DMA_PRIORITIES)
            return carry

        lax.fori_loop(0, tc, issue, 0, unroll=4)

    def step(cur_buf, cur_slot, nxt_buf, nxt_slot):
        wait_tile(cur_buf, cur_slot)
        n_chunks = (tc // COMBINE_ROWS) * SLAB_ROWS
        per_chunk = tc * TOP_K // n_chunks
        a = 0
        for t0 in range(0, tc, COMBINE_ROWS):
            rows = slice(t0, t0 + COMBINE_ROWS)
            gates = [jnp.broadcast_to(g_ref[rows, k:k + 1], (COMBINE_ROWS, SLAB_LANES)) for k in range(TOP_K)]
            for s in range(SLAB_ROWS):
                for _ in range(per_chunk):
                    fetch(nxt_ref, a, nxt_buf, nxt_slot).start(priority=a % DMA_PRIORITIES)
                    a += 1
                lo_sum = None
                hi_sum = None
                for k in range(TOP_K):
                    words = cur_buf[k, pl.ds(t0 * SLAB_ROWS + s, COMBINE_ROWS, stride=SLAB_ROWS), :]
                    lo, hi = _unpack_bf16(words)
                    lo_sum = gates[k] * lo if lo_sum is None else lo_sum + gates[k] * lo
                    hi_sum = gates[k] * hi if hi_sum is None else hi_sum + gates[k] * hi
                c = (s // DOWN_ROWS) * DOWN_TN + (s % DOWN_ROWS) * SLAB_LANES
                o_ref[rows, c:c + SLAB_LANES] = x_ref[rows, c:c + SLAB_LANES] + lo_sum
                c += DOWN_WORDS
                o_ref[rows, c:c + SLAB_LANES] = x_ref[rows, c:c + SLAB_LANES] + hi_sum

        @pl.when(i == pl.num_programs(0) - 1)
        def _():
            wait_tile(nxt_buf, nxt_slot)

    @pl.when(i % 2 == 0)
    def _():
        step(buf_a, 0, buf_b, 1)

    @pl.when(i % 2 == 1)
    def _():
        step(buf_b, 1, buf_a, 0)


def _combine(ys, x2, slots, gates4, tc=128):
    nb = SEQ // tc
    idx_spec = lambda f: pl.BlockSpec((1, 1, tc * TOP_K), lambda i: (f(i), 0, 0), memory_space=pltpu.SMEM)
    return pl.pallas_call(
        functools.partial(_combine_kernel, tc=tc),
        grid=(nb,),
        in_specs=[idx_spec(lambda i: i), idx_spec(lambda i: jnp.minimum(i + 1, nb - 1)),
                  pl.BlockSpec(memory_space=pl.ANY),
                  pl.BlockSpec((tc, D_MODEL), lambda i: (i, 0)),
                  pl.BlockSpec((tc, TOP_K), lambda i: (i, 0))],
        out_specs=pl.BlockSpec((tc, D_MODEL), lambda i: (i, 0)),
        out_shape=jax.ShapeDtypeStruct((SEQ, D_MODEL), F32),
        scratch_shapes=[pltpu.VMEM((TOP_K, tc * SLAB_ROWS, SLAB_LANES), jnp.uint32),
                        pltpu.VMEM((TOP_K, tc * SLAB_ROWS, SLAB_LANES), jnp.uint32), pltpu.SemaphoreType.DMA((2,))],
        compiler_params=_params(("arbitrary",), VMEM_SMALL_MIB),
        name="moe_combine",
    )(slots, slots, ys, x2, gates4)


def kernel(x, mem, positions, attn_norm_g, w_in, gmlp_ln_g, gmlp_ln_b, gmlp_w_s, gmlp_b_s, diff_q_norm_g, diff_k_norm_g, diff_lambda_q1, diff_lambda_k1, diff_lambda_q2, diff_lambda_k2, diff_subln_g, mem_norm_g, w_mem_kv, mem_q_norm_g, mem_k_norm_g, w_out, ffn_norm_g, w_router, b_router, w_gate_up, b_gate_up, w_down, b_down):
    depth = attn_norm_g.shape[0]
    xs = x.reshape(SEQ, D_MODEL)
    mem2 = mem.reshape(MEM_LEN, D_MODEL)
    for i in range(depth):
        lam_init = 0.8 - 0.6 * math.exp(-0.3 * i)
        h = _rmsnorm_rows(xs, attn_norm_g[i], tm=512)

        def in_proj(name, col0, width, dtype):
            return _matmul([h], w_in[i], None, dtype, tm=PROJ_TM, tn=PROJ_TN, name=name, cols=(col0, width))

        z_gqk = in_proj("in_proj_gqk", 0, COL_V, F32)
        v = in_proj("in_proj_v", COL_V, DIFF_WIDTH, BF16)
        z_m = in_proj("in_proj_mem", COL_M, MEM_WIDTH, F32)

        y_g = _gmlp(z_gqk, gmlp_ln_g[i], gmlp_ln_b[i], gmlp_w_s[i], gmlp_b_s[i])

        q, k = _qk_prep(z_gqk, positions, diff_q_norm_g[i], diff_k_norm_g[i])
        y_d = _diff_attn(q, k, v, diff_lambda_q1[i], diff_lambda_k1[i], diff_lambda_q2[i], diff_lambda_k2[i],
                         diff_subln_g[i], lam_init)

        hm = _rmsnorm_rows(mem2, mem_norm_g[i], tm=256)
        kv = _matmul([hm], w_mem_kv[i], None, F32, tm=MEM_LEN, tn=PROJ_TN, name="mem_kv_proj")
        y_m = _mem_attn(z_m, kv, mem_q_norm_g[i], mem_k_norm_g[i])

        x2 = _matmul([y_g, y_d, y_m], w_out[i], xs, F32, tm=PROJ_TM, tn=PROJ_TN, name="out_proj")

        hp, gates4, experts4, rank4, counts = _router(x2, ffn_norm_g[i], w_router[i], b_router[i])

        counts = counts[0]
        padded = (counts + MOE_TILE - 1) // MOE_TILE * MOE_TILE
        pend = jnp.cumsum(padded)
        pstart = pend - padded
        n_used = (pend[-1] // MOE_TILE).astype(jnp.int32).reshape(1)
        blk_start = jnp.arange(MOE_NT, dtype=jnp.int32) * MOE_TILE
        blk_e = jnp.minimum(jnp.sum(pend[None, :] <= blk_start[:, None], axis=1), N_EXPERTS - 1).astype(jnp.int32)
        blk_end = blk_start + MOE_TILE
        fill_tiles = jnp.logical_or(jnp.any(blk_end[:, None] == pend[None, :], axis=1),
                                    blk_start >= pend[-1]).astype(jnp.int32)

        tile_ids = jnp.arange(MOE_NT, dtype=jnp.int32)
        run_start = jnp.concatenate([jnp.ones((1,), jnp.bool_), blk_e[1:] != blk_e[:-1]])
        starts_from = lax.cummin(jnp.where(run_start, tile_ids, MOE_NT), axis=0, reverse=True)
        nxt = jnp.concatenate([starts_from[1:], jnp.full((1,), MOE_NT, jnp.int32)]).astype(jnp.int32)

        xg, slots = _dispatch(hp, experts4, rank4, pstart, fill_tiles)
        act = _moe_up(xg, blk_e, nxt, n_used, w_gate_up[i], b_gate_up[i])
        ys = _moe_down(act, blk_e, nxt, n_used, w_down[i], b_down[i])
        xs = _combine(ys, x2, slots, gates4)
    return xs.reshape(x.shape)
```
